```python
import jax, jax.numpy as jnp
from jax import lax
import numpy as np

D_MODEL = 2048
BATCH = 8
SEQ = 8192
DEPTH = 1

SB_HEADS = 16
SB_HEAD_DIM = 128
SB_WIDTH = SB_HEADS * SB_HEAD_DIM
SB_BLOCK = 128
GDN_HEADS = 16
GDN_KEY_DIM = 128
GDN_VAL_DIM = 128
GDN_QK_WIDTH = GDN_HEADS * GDN_KEY_DIM
GDN_V_WIDTH = GDN_HEADS * GDN_VAL_DIM
GDN_CONV = 4
GDN_CONV_DIM = 2 * GDN_QK_WIDTH + GDN_V_WIDTH
GDN_CHUNK = 64
IN_WIDTH = 4 * SB_WIDTH + 2 * GDN_QK_WIDTH + 2 * GDN_V_WIDTH + 2 * GDN_HEADS + 2 * D_MODEL
NORM_EPS = 1e-6
L2_EPS = 1e-6

kernel_name = "hybrid_stickbreaking_gated_deltanet_adaln"


def _in_split_points():
    sizes = [SB_WIDTH] * 4 + [GDN_QK_WIDTH, GDN_QK_WIDTH, GDN_V_WIDTH, GDN_V_WIDTH,
                              GDN_HEADS, GDN_HEADS, D_MODEL, D_MODEL]
    return [int(p) for p in np.cumsum(sizes)[:-1]]


def _rmsnorm(x, w):
    xf = x.astype(jnp.float32)
    y = xf * lax.rsqrt(jnp.mean(xf * xf, axis=-1, keepdims=True) + NORM_EPS)
    return (y * w.astype(jnp.float32)).astype(x.dtype)


def _l2norm(x):
    return x * lax.rsqrt(jnp.sum(x * x, axis=-1, keepdims=True) + L2_EPS)


def _heads(x, n_heads):
    b, t, _ = x.shape
    return x.reshape(b, t, n_heads, -1).transpose(0, 2, 1, 3)


def _merge_heads(x):
    b, h, t, d = x.shape
    return x.transpose(0, 2, 1, 3).reshape(b, t, h * d)


def _causal_short_conv(x, w):
    k, ch = w.shape
    return lax.conv_general_dilated(
        x, w[:, None, :].astype(x.dtype), window_strides=(1,), padding=[(k - 1, 0)],
        dimension_numbers=("NWC", "WIO", "NWC"), feature_group_count=ch)


def _stick_breaking_attention(q, k, v):
    t_len = q.shape[2]
    scale = SB_HEAD_DIM ** -0.5
    outs = []
    for blk in range(t_len // SB_BLOCK):
        t0 = blk * SB_BLOCK
        t1 = t0 + SB_BLOCK
        qb = q[:, :, t0:t1].astype(jnp.float32)
        kb = k[:, :, :t1].astype(jnp.float32)
        vb = v[:, :, :t1].astype(jnp.float32)
        z = jnp.einsum("bhtd,bhsd->bhts", qb, kb) * scale
        t_idx = t0 + jnp.arange(SB_BLOCK)[:, None]
        s_idx = jnp.arange(t1)[None, :]
        mask = s_idx < t_idx
        log_beta = jax.nn.log_sigmoid(z)
        log_not_beta = jnp.where(mask, jax.nn.log_sigmoid(-z), 0.0)
        later = lax.cumsum(log_not_beta, axis=3, reverse=True) - log_not_beta
        att = jnp.where(mask, jnp.exp(log_beta + later), 0.0)
        outs.append(jnp.einsum("bhts,bhsd->bhtd", att, vb))
    return jnp.concatenate(outs, axis=2)


def _gated_delta_rule(q, k, v, beta, g):
    b, h, t_len, dk = q.shape
    dv = v.shape[-1]
    c = GDN_CHUNK
    n = t_len // c
    q = q * dk ** -0.5

    def chunks(a):
        return a.reshape(b, h, n, c, *a.shape[3:])

    q, k, v, beta, g = chunks(q), chunks(k), chunks(v), chunks(beta), chunks(g)
    g = jnp.cumsum(g, axis=-1)
    k_beta = k * beta[..., None]
    v_beta = v * beta[..., None]
    tril = jnp.tril(jnp.ones((c, c), dtype=bool))
    strict = jnp.tril(jnp.ones((c, c), dtype=bool), -1)
    decay = jnp.exp(jnp.where(tril, g[..., :, None] - g[..., None, :], -jnp.inf))
    lower = jnp.where(strict, jnp.einsum("bhncd,bhnsd->bhncs", k_beta, k) * decay, 0.0)
    eye = jnp.eye(c, dtype=jnp.float32)
    t_mat = lax.linalg.triangular_solve(eye + lower, jnp.broadcast_to(eye, lower.shape),
                                        left_side=True, lower=True, unit_diagonal=True)
    u = jnp.einsum("bhncs,bhnsv->bhncv", t_mat, v_beta)
    w = jnp.einsum("bhncs,bhnsk->bhnck", t_mat, k_beta * jnp.exp(g)[..., None])
    intra = jnp.where(tril, jnp.einsum("bhncd,bhnsd->bhncs", q, k) * decay, 0.0)
    q_dec = q * jnp.exp(g)[..., None]
    k_dec = k * jnp.exp(g[..., -1:] - g)[..., None]
    g_last = jnp.exp(g[..., -1])

    def step(state, inp):
        q_i, k_i, u_i, w_i, intra_i, gl_i = inp
        v_new = u_i - jnp.einsum("bhck,bhkv->bhcv", w_i, state)
        o_i = (jnp.einsum("bhck,bhkv->bhcv", q_i, state)
               + jnp.einsum("bhcs,bhsv->bhcv", intra_i, v_new))
        state = state * gl_i[..., None, None] + jnp.einsum("bhck,bhcv->bhkv", k_i, v_new)
        return state, o_i

    xs = tuple(jnp.moveaxis(a, 2, 0) for a in (q_dec, k_dec, u, w, intra, g_last))
    s0 = jnp.zeros((b, h, dk, dv), dtype=jnp.float32)
    _, o = lax.scan(step, s0, xs)
    return jnp.moveaxis(o, 0, 2).reshape(b, h, t_len, dv)


def _fwd_setup_inputs(seed: int = 0) -> dict:
    key = jax.random.key(seed)
    ks = jax.random.split(key, 16)
    f32 = jnp.float32
    x = jax.random.normal(ks[0], (BATCH, SEQ, D_MODEL), f32)
    c = jax.random.normal(ks[1], (BATCH, D_MODEL), f32)
    w_ada = jax.random.normal(ks[2], (DEPTH, D_MODEL, 3 * D_MODEL), f32) * D_MODEL ** -0.5
    b_ada = jax.random.normal(ks[3], (DEPTH, 3 * D_MODEL), f32) * 0.02
    norm_w = 1.0 + 0.02 * jax.random.normal(ks[4], (DEPTH, D_MODEL), f32)
    w_in = jax.random.normal(ks[5], (DEPTH, D_MODEL, IN_WIDTH), f32) * D_MODEL ** -0.5
    gdn_conv_w = jax.random.normal(ks[6], (DEPTH, GDN_CONV, GDN_CONV_DIM), f32) * GDN_CONV ** -0.5
    gdn_a_log = jnp.log(jax.random.uniform(ks[7], (DEPTH, GDN_HEADS), f32, 1.0, 16.0))
    dt = jnp.exp(jax.random.uniform(ks[8], (DEPTH, GDN_HEADS), f32, jnp.log(1e-3), jnp.log(1e-1)))
    gdn_dt_bias = dt + jnp.log(-jnp.expm1(-dt))
    gdn_norm_w = 1.0 + 0.02 * jax.random.normal(ks[9], (DEPTH, GDN_VAL_DIM), f32)
    w_proj_sb = jax.random.normal(ks[10], (DEPTH, SB_WIDTH, D_MODEL), f32) * SB_WIDTH ** -0.5
    w_proj_gdn = jax.random.normal(ks[11], (DEPTH, GDN_V_WIDTH, D_MODEL), f32) * GDN_V_WIDTH ** -0.5
    w_out = jax.random.normal(ks[12], (DEPTH, D_MODEL, D_MODEL), f32) * D_MODEL ** -0.5
    final_norm_w = 1.0 + 0.02 * jax.random.normal(ks[13], (D_MODEL,), f32)
    return {"x": x, "c": c, "w_ada": w_ada, "b_ada": b_ada, "norm_w": norm_w, "w_in": w_in,
            "gdn_conv_w": gdn_conv_w, "gdn_a_log": gdn_a_log, "gdn_dt_bias": gdn_dt_bias,
            "gdn_norm_w": gdn_norm_w, "w_proj_sb": w_proj_sb, "w_proj_gdn": w_proj_gdn,
            "w_out": w_out, "final_norm_w": final_norm_w}


def _fwd_reference(x, c, w_ada, b_ada, norm_w, w_in, gdn_conv_w, gdn_a_log, gdn_dt_bias,
              gdn_norm_w, w_proj_sb, w_proj_gdn, w_out, final_norm_w):
    f32 = jnp.float32
    split_points = _in_split_points()
    for l in range(DEPTH):
        mod = jax.nn.silu(c) @ w_ada[l] + b_ada[l]
        shift, scale, gate = jnp.split(mod, 3, axis=-1)
        h = _rmsnorm(x, norm_w[l]) * (1.0 + scale[:, None, :]) + shift[:, None, :]
        proj = h @ w_in[l]
        (sb_q, sb_k, sb_v, sb_z, gq, gk, gv, gz, gb, ga, m_sb, m_gdn) = jnp.split(
            proj, split_points, axis=-1)

        o_sb = _stick_breaking_attention(_heads(sb_q, SB_HEADS), _heads(sb_k, SB_HEADS),
                                         _heads(sb_v, SB_HEADS))
        o_sb = _merge_heads(o_sb).astype(x.dtype) * jax.nn.silu(sb_z)

        qkv = jax.nn.silu(_causal_short_conv(jnp.concatenate([gq, gk, gv], axis=-1), gdn_conv_w[l]))
        cq, ck, cv = jnp.split(qkv, [GDN_QK_WIDTH, 2 * GDN_QK_WIDTH], axis=-1)
        qh = _l2norm(_heads(cq, GDN_HEADS).astype(f32))
        kh = _l2norm(_heads(ck, GDN_HEADS).astype(f32))
        vh = _heads(cv, GDN_HEADS).astype(f32)
        beta = jax.nn.sigmoid(gb.astype(f32)).transpose(0, 2, 1)
        g = (-jnp.exp(gdn_a_log[l].astype(f32))
             * jax.nn.softplus(ga.astype(f32) + gdn_dt_bias[l].astype(f32))).transpose(0, 2, 1)
        o_gdn = _gated_delta_rule(qh, kh, vh, beta, g)
        o_gdn = _rmsnorm(o_gdn, gdn_norm_w[l])
        o_gdn = _merge_heads(o_gdn).astype(x.dtype) * jax.nn.silu(gz)

        y = (jax.nn.sigmoid(m_sb) * (o_sb @ w_proj_sb[l])
             + jax.nn.sigmoid(m_gdn) * (o_gdn @ w_proj_gdn[l]))
        x = x + gate[:, None, :] * (y @ w_out[l])
    return _rmsnorm(x, final_norm_w)


import jax as _jax
import jax.numpy as _jnp

TWIN_FORMAT = 'train_step'
FWD_PARAMS = ['x', 'c', 'w_ada', 'b_ada', 'norm_w', 'w_in', 'gdn_conv_w', 'gdn_a_log', 'gdn_dt_bias', 'gdn_norm_w', 'w_proj_sb', 'w_proj_gdn', 'w_out', 'final_norm_w']
TWIN_WEIGHTS = ['w_ada', 'b_ada', 'norm_w', 'w_in', 'gdn_conv_w', 'gdn_a_log', 'gdn_dt_bias', 'gdn_norm_w', 'w_proj_sb', 'w_proj_gdn', 'w_out', 'final_norm_w']
TWIN_DIFF_INPUT = 'x'
TWIN_INPUTS = ['x', 'c', 'w_ada', 'b_ada', 'norm_w', 'w_in', 'gdn_conv_w', 'gdn_a_log', 'gdn_dt_bias', 'gdn_norm_w', 'w_proj_sb', 'w_proj_gdn', 'w_out', 'final_norm_w', 'loss_target', 'm_w_ada', 'm_b_ada', 'm_norm_w', 'm_w_in', 'm_gdn_conv_w', 'm_gdn_a_log', 'm_gdn_dt_bias', 'm_gdn_norm_w', 'm_w_proj_sb', 'm_w_proj_gdn', 'm_w_out', 'm_final_norm_w', 'v_w_ada', 'v_b_ada', 'v_norm_w', 'v_w_in', 'v_gdn_conv_w', 'v_gdn_a_log', 'v_gdn_dt_bias', 'v_gdn_norm_w', 'v_w_proj_sb', 'v_w_proj_gdn', 'v_w_out', 'v_final_norm_w']
TWIN_OUTPUTS = ['loss', 'grad_x', 'grad_w_ada', 'grad_b_ada', 'grad_norm_w', 'grad_w_in', 'grad_gdn_conv_w', 'grad_gdn_a_log', 'grad_gdn_dt_bias', 'grad_gdn_norm_w', 'grad_w_proj_sb', 'grad_w_proj_gdn', 'grad_w_out', 'grad_final_norm_w', 'delta_w_ada', 'delta_b_ada', 'delta_norm_w', 'delta_w_in', 'delta_gdn_conv_w', 'delta_gdn_a_log', 'delta_gdn_dt_bias', 'delta_gdn_norm_w', 'delta_w_proj_sb', 'delta_w_proj_gdn', 'delta_w_out', 'delta_final_norm_w', 'new_m_w_ada', 'new_m_b_ada', 'new_m_norm_w', 'new_m_w_in', 'new_m_gdn_conv_w', 'new_m_gdn_a_log', 'new_m_gdn_dt_bias', 'new_m_gdn_norm_w', 'new_m_w_proj_sb', 'new_m_w_proj_gdn', 'new_m_w_out', 'new_m_final_norm_w', 'new_v_w_ada', 'new_v_b_ada', 'new_v_norm_w', 'new_v_w_in', 'new_v_gdn_conv_w', 'new_v_gdn_a_log', 'new_v_gdn_dt_bias', 'new_v_gdn_norm_w', 'new_v_w_proj_sb', 'new_v_w_proj_gdn', 'new_v_w_out', 'new_v_final_norm_w']
TWIN_LEAF_KINDS = {'loss': 'loss', 'grad_x': 'grad_x', 'grad_w_ada': 'grad_w', 'grad_b_ada': 'grad_w', 'grad_norm_w': 'grad_w', 'grad_w_in': 'grad_w', 'grad_gdn_conv_w': 'grad_w', 'grad_gdn_a_log': 'grad_w', 'grad_gdn_dt_bias': 'grad_w', 'grad_gdn_norm_w': 'grad_w', 'grad_w_proj_sb': 'grad_w', 'grad_w_proj_gdn': 'grad_w', 'grad_w_out': 'grad_w', 'grad_final_norm_w': 'grad_w', 'delta_w_ada': 'delta_w', 'delta_b_ada': 'delta_w', 'delta_norm_w': 'delta_w', 'delta_w_in': 'delta_w', 'delta_gdn_conv_w': 'delta_w', 'delta_gdn_a_log': 'delta_w', 'delta_gdn_dt_bias': 'delta_w', 'delta_gdn_norm_w': 'delta_w', 'delta_w_proj_sb': 'delta_w', 'delta_w_proj_gdn': 'delta_w', 'delta_w_out': 'delta_w', 'delta_final_norm_w': 'delta_w', 'new_m_w_ada': 'new_m', 'new_m_b_ada': 'new_m', 'new_m_norm_w': 'new_m', 'new_m_w_in': 'new_m', 'new_m_gdn_conv_w': 'new_m', 'new_m_gdn_a_log': 'new_m', 'new_m_gdn_dt_bias': 'new_m', 'new_m_gdn_norm_w': 'new_m', 'new_m_w_proj_sb': 'new_m', 'new_m_w_proj_gdn': 'new_m', 'new_m_w_out': 'new_m', 'new_m_final_norm_w': 'new_m', 'new_v_w_ada': 'new_v', 'new_v_b_ada': 'new_v', 'new_v_norm_w': 'new_v', 'new_v_w_in': 'new_v', 'new_v_gdn_conv_w': 'new_v', 'new_v_gdn_a_log': 'new_v', 'new_v_gdn_dt_bias': 'new_v', 'new_v_gdn_norm_w': 'new_v', 'new_v_w_proj_sb': 'new_v', 'new_v_w_proj_gdn': 'new_v', 'new_v_w_out': 'new_v', 'new_v_final_norm_w': 'new_v'}


def _forward(args):
    return _fwd_reference(*[args[k] for k in FWD_PARAMS])


def _output_shape():
    def fwd():
        inp = _fwd_setup_inputs(0)
        return _fwd_reference(*[inp[k] for k in FWD_PARAMS])
    out = _jax.eval_shape(fwd)
    return out.shape, out.dtype

N_MICROBATCH = 1
ADAM_LR = 0.001
ADAM_B1 = 0.9
ADAM_B2 = 0.999
ADAM_EPS = 1e-08
ADAM_WD = 0.01
ADAM_STEP = 10
PER_EXAMPLE_BATCH_AXIS = {'x': 0, 'c': 0, 'loss_target': 0}
SHARED_INPUTS = []
_WEIGHT_DTYPES = {'w_ada': _jnp.float32, 'b_ada': _jnp.float32, 'norm_w': _jnp.float32, 'w_in': _jnp.float32, 'gdn_conv_w': _jnp.float32, 'gdn_a_log': _jnp.float32, 'gdn_dt_bias': _jnp.float32, 'gdn_norm_w': _jnp.float32, 'w_proj_sb': _jnp.float32, 'w_proj_gdn': _jnp.float32, 'w_out': _jnp.float32, 'final_norm_w': _jnp.float32}
MOMENT_SCALE = {'w_ada': 4.441116e-02, 'b_ada': 7.829924e-02, 'norm_w': 6.562447e-02, 'w_in': 2.433349e-02, 'gdn_conv_w': 2.267828e-02, 'gdn_a_log': 1.115781e-01, 'gdn_dt_bias': 1.088783e-01, 'gdn_norm_w': 1.559079e-01, 'w_proj_sb': 3.274963e-02, 'w_proj_gdn': 3.169193e-02, 'w_out': 4.593540e-02, 'final_norm_w': 3.207540e+01}


def _to_microbatches(a, axis):
    t = _jnp.moveaxis(a, axis, 0)
    t = t.reshape((N_MICROBATCH, t.shape[0] // N_MICROBATCH) + t.shape[1:])
    return _jnp.moveaxis(t, 1, axis + 1)


def setup_inputs(seed: int = 0) -> dict:
    inp = _fwd_setup_inputs(seed)
    key = _jax.random.fold_in(_jax.random.key(seed), 7919)
    shape, _ = _output_shape()
    out = dict(inp)
    out["loss_target"] = _jax.random.normal(_jax.random.fold_in(key, 0), shape, _jnp.float32)
    for i, name in enumerate(TWIN_WEIGHTS):
        w = inp[name].astype(_jnp.float32)
        if MOMENT_SCALE is None:
            s = _jnp.sqrt(_jnp.mean(_jnp.square(w)) + 1e-30)
        else:
            s = MOMENT_SCALE[name]
        km, kv = _jax.random.split(_jax.random.fold_in(key, i + 1))
        out[name] = w
        out["m_" + name] = s * _jax.random.normal(km, w.shape, _jnp.float32)
        out["v_" + name] = (s * s) * _jax.random.uniform(kv, w.shape, _jnp.float32, 0.5, 1.5)
    if N_MICROBATCH > 1:
        for name, axis in PER_EXAMPLE_BATCH_AXIS.items():
            out[name] = _to_microbatches(out[name], axis)
    return {'x': out['x'], 'c': out['c'], 'w_ada': out['w_ada'], 'b_ada': out['b_ada'], 'norm_w': out['norm_w'], 'w_in': out['w_in'], 'gdn_conv_w': out['gdn_conv_w'], 'gdn_a_log': out['gdn_a_log'], 'gdn_dt_bias': out['gdn_dt_bias'], 'gdn_norm_w': out['gdn_norm_w'], 'w_proj_sb': out['w_proj_sb'], 'w_proj_gdn': out['w_proj_gdn'], 'w_out': out['w_out'], 'final_norm_w': out['final_norm_w'], 'loss_target': out['loss_target'], 'm_w_ada': out['m_w_ada'], 'm_b_ada': out['m_b_ada'], 'm_norm_w': out['m_norm_w'], 'm_w_in': out['m_w_in'], 'm_gdn_conv_w': out['m_gdn_conv_w'], 'm_gdn_a_log': out['m_gdn_a_log'], 'm_gdn_dt_bias': out['m_gdn_dt_bias'], 'm_gdn_norm_w': out['m_gdn_norm_w'], 'm_w_proj_sb': out['m_w_proj_sb'], 'm_w_proj_gdn': out['m_w_proj_gdn'], 'm_w_out': out['m_w_out'], 'm_final_norm_w': out['m_final_norm_w'], 'v_w_ada': out['v_w_ada'], 'v_b_ada': out['v_b_ada'], 'v_norm_w': out['v_norm_w'], 'v_w_in': out['v_w_in'], 'v_gdn_conv_w': out['v_gdn_conv_w'], 'v_gdn_a_log': out['v_gdn_a_log'], 'v_gdn_dt_bias': out['v_gdn_dt_bias'], 'v_gdn_norm_w': out['v_gdn_norm_w'], 'v_w_proj_sb': out['v_w_proj_sb'], 'v_w_proj_gdn': out['v_w_proj_gdn'], 'v_w_out': out['v_w_out'], 'v_final_norm_w': out['v_final_norm_w']}


def _loss(weights, diff, rest, loss_target):
    with _jax.named_scope("forward"):
        args = {**rest, TWIN_DIFF_INPUT: diff, **{k: w.astype(_WEIGHT_DTYPES[k]) for k, w in weights.items()}}
        y = _forward(args)
    with _jax.named_scope("loss_head"):
        err = _jnp.square(y.astype(_jnp.float32) - loss_target)
        return 0.5 * _jnp.sum(_jnp.mean(err, axis=-1)) if err.ndim else 0.5 * err


def _adamw(w, g, m, v):
    m = ADAM_B1 * m + (1.0 - ADAM_B1) * g
    v = ADAM_B2 * v + (1.0 - ADAM_B2) * _jnp.square(g)
    m_hat = m / (1.0 - ADAM_B1 ** ADAM_STEP)
    v_hat = v / (1.0 - ADAM_B2 ** ADAM_STEP)
    delta = -ADAM_LR * (m_hat / (_jnp.sqrt(v_hat) + ADAM_EPS) + ADAM_WD * w)
    return delta, m, v


def reference(x, c, w_ada, b_ada, norm_w, w_in, gdn_conv_w, gdn_a_log, gdn_dt_bias, gdn_norm_w, w_proj_sb, w_proj_gdn, w_out, final_norm_w, loss_target, m_w_ada, m_b_ada, m_norm_w, m_w_in, m_gdn_conv_w, m_gdn_a_log, m_gdn_dt_bias, m_gdn_norm_w, m_w_proj_sb, m_w_proj_gdn, m_w_out, m_final_norm_w, v_w_ada, v_b_ada, v_norm_w, v_w_in, v_gdn_conv_w, v_gdn_a_log, v_gdn_dt_bias, v_gdn_norm_w, v_w_proj_sb, v_w_proj_gdn, v_w_out, v_final_norm_w):
    given = dict(x=x, c=c, w_ada=w_ada, b_ada=b_ada, norm_w=norm_w, w_in=w_in, gdn_conv_w=gdn_conv_w, gdn_a_log=gdn_a_log, gdn_dt_bias=gdn_dt_bias, gdn_norm_w=gdn_norm_w, w_proj_sb=w_proj_sb, w_proj_gdn=w_proj_gdn, w_out=w_out, final_norm_w=final_norm_w, loss_target=loss_target, m_w_ada=m_w_ada, m_b_ada=m_b_ada, m_norm_w=m_norm_w, m_w_in=m_w_in, m_gdn_conv_w=m_gdn_conv_w, m_gdn_a_log=m_gdn_a_log, m_gdn_dt_bias=m_gdn_dt_bias, m_gdn_norm_w=m_gdn_norm_w, m_w_proj_sb=m_w_proj_sb, m_w_proj_gdn=m_w_proj_gdn, m_w_out=m_w_out, m_final_norm_w=m_final_norm_w, v_w_ada=v_w_ada, v_b_ada=v_b_ada, v_norm_w=v_norm_w, v_w_in=v_w_in, v_gdn_conv_w=v_gdn_conv_w, v_gdn_a_log=v_gdn_a_log, v_gdn_dt_bias=v_gdn_dt_bias, v_gdn_norm_w=v_gdn_norm_w, v_w_proj_sb=v_w_proj_sb, v_w_proj_gdn=v_w_proj_gdn, v_w_out=v_w_out, v_final_norm_w=v_final_norm_w)
    weights = {n: given[n] for n in TWIN_WEIGHTS}
    shared = {n: given[n] for n in SHARED_INPUTS}
    per_example = {n: given[n] for n in ['x', 'c']}
    grad_fn = _jax.value_and_grad(_loss, argnums=(0, 1))

    def one_microbatch(ex, loss_target):
        ex = dict(ex)
        diff = ex.pop(TWIN_DIFF_INPUT)
        return grad_fn(weights, diff, {**shared, **ex}, loss_target)

    if N_MICROBATCH == 1:
        loss, (grad_w, grad_x) = one_microbatch(per_example, given["loss_target"])
    else:
        def body(carry, xs):
            loss_sum, grad_sum = carry
            l_k, (gw_k, gx_k) = one_microbatch(xs[0], xs[1])
            with _jax.named_scope("update"):
                return (loss_sum + l_k, _jax.tree.map(_jnp.add, grad_sum, gw_k)), gx_k

        init = (_jnp.zeros((), _jnp.float32), _jax.tree.map(_jnp.zeros_like, weights))
        (loss, grad_w), grad_x = _jax.lax.scan(body, init, (per_example, given["loss_target"]))
    with _jax.named_scope("update"):
        delta_w, new_m, new_v = {}, {}, {}
        for n in TWIN_WEIGHTS:
            delta_w[n], new_m[n], new_v[n] = _adamw(weights[n], grad_w[n], given["m_" + n], given["v_" + n])
    return (loss, grad_x, *[grad_w[n] for n in TWIN_WEIGHTS], *[delta_w[n] for n in TWIN_WEIGHTS],
            *[new_m[n] for n in TWIN_WEIGHTS], *[new_v[n] for n in TWIN_WEIGHTS])
```

```python
import functools

import jax
import jax.numpy as jnp
from jax import lax
from jax.experimental import pallas as pl
from jax.experimental.pallas import tpu as pltpu

F32 = jnp.float32
BF16 = jnp.bfloat16
HD = 128
CH = 64
CONV_K = 4
NORM_EPS = 1e-6
L2_EPS = 1e-6
ADAM_LR, ADAM_B1, ADAM_B2, ADAM_EPS, ADAM_WD, ADAM_STEP = 0.001, 0.9, 0.999, 1e-08, 0.01, 10

VMEM_LIMIT = 56 * 1024 * 1024
SB_BQ, SB_BK = 256, 128
GDN_TB = 512
EW_TM = 256
HEAD_TM = 512
CONV_CW = 512
MM_TM, MM_TN, MM_TK = 1024, 1024, 512
MESH = pl.DeviceIdType.MESH

NN = (((1,), (0,)), ((), ()))
NT = (((1,), (1,)), ((), ()))
TN = (((0,), (0,)), ((), ()))


def _pcall(body, **kw):
    return pl.pallas_call(body, **kw)


def _cparams(sem=None):
    return pltpu.CompilerParams(dimension_semantics=sem, vmem_limit_bytes=VMEM_LIMIT)


def _dot(a, b, dims=NN):
    return lax.dot_general(a, b, dims, preferred_element_type=F32)


def _dotf(a, b, dims=NN):
    return lax.dot_general(a, b, dims, precision=lax.Precision.HIGHEST, preferred_element_type=F32)


def _bdot_make(dims, da_rule, db_rule):
    @jax.custom_vjp
    def f(a, b):
        return _dot(a.astype(BF16), b.astype(BF16), dims)

    def fwd(a, b):
        return f(a, b), (a, b)

    def bwd(res, g):
        a, b = res
        return da_rule(g, a, b), db_rule(g, a, b)

    f.defvjp(fwd, bwd)
    return f


def _rdot(a, b, dims):
    return _dot(a.astype(BF16), b.astype(BF16), dims)


_bdot_nn = _bdot_make(NN, lambda g, a, b: _rdot(g, b, NT), lambda g, a, b: _rdot(a, g, TN))
_bdot_nt = _bdot_make(NT, lambda g, a, b: _rdot(g, b, NN), lambda g, a, b: _rdot(g, a, TN))
_bdot_tn = _bdot_make(TN, lambda g, a, b: _rdot(b, g, NT), lambda g, a, b: _rdot(a, g, NN))


def _iota(shape, axis):
    return lax.broadcasted_iota(jnp.int32, shape, axis)


def _sigmoid(x):
    e = jnp.exp(-jnp.abs(x))
    return jnp.where(x >= 0, 1.0 / (1.0 + e), e / (1.0 + e))


def _silu(x):
    return x * _sigmoid(x)


def _softplus(x):
    return jnp.maximum(x, 0.0) + jnp.log(1.0 + jnp.exp(-jnp.abs(x)))


def _rms(x, w):
    return x * lax.rsqrt(jnp.mean(x * x, axis=-1, keepdims=True) + NORM_EPS) * w


def _ew(fn, grid, ins, in_specs, outs, out_specs, acc=None, name=None):
    n_in = len(ins)
    acc = acc or [None] * len(outs)

    def body(*refs):
        vals = fn(*[r[...] for r in refs[:n_in]])
        if not isinstance(vals, (tuple, list)):
            vals = (vals,)
        for r, v, a in zip(refs[n_in:], vals, acc):
            if a is None:
                r[...] = v.astype(r.dtype)
                continue
            first = pl.program_id(len(grid) - 1) == 0
            if a == 'all':
                for ax in range(len(grid) - 1):
                    first = jnp.logical_and(first, pl.program_id(ax) == 0)

            @pl.when(first)
            def _():
                r[...] = v.astype(r.dtype)

            @pl.when(jnp.logical_not(first))
            def _():
                r[...] += v.astype(r.dtype)

    res = _pcall(body, grid=grid, in_specs=in_specs, out_specs=out_specs,
                 out_shape=[jax.ShapeDtypeStruct(s, d) for s, d in outs],
                 compiler_params=_cparams(("arbitrary",) * len(grid)), name=name)(*ins)
    return res


def _mm(pairs, mode, out_dtype, name, add=None, tm=None, tn=None, tk=None):
    a0, b0 = pairs[0][0], pairs[0][1]
    M = a0.shape[1] if mode == 'tn' else a0.shape[0]
    N = b0.shape[0] if mode == 'nt' else b0.shape[1]
    tm = min(tm or MM_TM, M)
    tn = min(tn or MM_TN, N)
    tk = tk or MM_TK
    tks = [min(tk, p[4]) for p in pairs]
    nks = [p[4] // t for p, t in zip(pairs, tks)]
    offs = [sum(nks[:i]) for i in range(len(pairs))]
    total = sum(nks)
    assert M % tm == 0 and N % tn == 0 and all(p[4] % t == 0 for p, t in zip(pairs, tks)), (name, M, N)
    dims = {'nn': NN, 'nt': NT, 'tn': TN}[mode]

    in_specs, ins = [], []
    for (a, b, ao, bo, kl), t, nk, off in zip(pairs, tks, nks, offs):
        def kidx(kk, nk=nk, off=off):
            return jnp.minimum(jnp.maximum(kk - off, 0), nk - 1)
        if mode == 'tn':
            in_specs.append(pl.BlockSpec((t, tm), lambda i, j, kk, f=kidx, o=ao // t: (o + f(kk), i)))
        else:
            in_specs.append(pl.BlockSpec((tm, t), lambda i, j, kk, f=kidx, o=ao // t: (i, o + f(kk))))
        if mode == 'nt':
            in_specs.append(pl.BlockSpec((tn, t), lambda i, j, kk, f=kidx, o=bo // t: (j, o + f(kk))))
        else:
            in_specs.append(pl.BlockSpec((t, tn), lambda i, j, kk, f=kidx, o=bo // t: (o + f(kk), j)))
        ins += [a, b]
    if add is not None:
        in_specs.append(pl.BlockSpec((tm, tn), lambda i, j, kk: (i, j)))
        ins.append(add)
    npairs = len(pairs)

    def body(*refs):
        out_ref, acc_ref = refs[-2], refs[-1]
        kk = pl.program_id(2)

        @pl.when(kk == 0)
        def _():
            acc_ref[...] = jnp.zeros_like(acc_ref)

        for p in range(npairs):
            def upd(p=p):
                acc_ref[...] += _dot(refs[2 * p][...].astype(BF16), refs[2 * p + 1][...].astype(BF16), dims)
            if npairs == 1:
                upd()
            else:
                pl.when(jnp.logical_and(kk >= offs[p], kk < offs[p] + nks[p]))(upd)

        @pl.when(kk == total - 1)
        def _():
            r = acc_ref[...]
            if add is not None:
                r = r + refs[2 * npairs][...]
            out_ref[...] = r.astype(out_ref.dtype)

    return _pcall(body, grid=(M // tm, N // tn, total), in_specs=in_specs,
                  out_specs=pl.BlockSpec((tm, tn), lambda i, j, kk: (i, j)),
                  out_shape=jax.ShapeDtypeStruct((M, N), out_dtype),
                  scratch_shapes=[pltpu.VMEM((tm, tn), F32)],
                  compiler_params=_cparams(("parallel", "parallel", "arbitrary")), name=name)(*ins)


def _mm1(a, b, mode, out_dtype, name, **kw):
    k = a.shape[0] if mode == 'tn' else a.shape[1]
    return _mm([(a, b, 0, 0, k)], mode, out_dtype, name, **kw)


def _sb_tiles(T):
    bq = min(SB_BQ, T)
    bk = min(SB_BK, bq)
    return bq, bk, bq // bk


def _sb_block(i, j, masked, q, k_ref, v_ref, accL, bq, bk, UU):
    r = pl.ds(pl.multiple_of(j * bk, bk), bk)
    kj = k_ref[r, :]
    vj = v_ref[r, :]
    z = _dot(q, kj, NT) * (HD ** -0.5)
    lb = jnp.minimum(z, 0.0) - jnp.log(1.0 + jnp.exp(-jnp.abs(z)))
    ln = lb - z
    mask = None
    if masked:
        mask = (j * bk + _iota((bq, bk), 1)) < (i * bq + _iota((bq, bk), 0))
        ln = jnp.where(mask, ln, 0.0)
    hi = ln.astype(BF16)
    lo = (ln - hi.astype(F32)).astype(BF16)
    later = _dot(jnp.concatenate([hi, lo], axis=1), UU) + accL
    a = jnp.exp(lb + later)
    if masked:
        a = jnp.where(mask, a, 0.0)
    return r, kj, vj, mask, ln, lb, a


def _sb_fwd(proj, T, H):
    bq, bk, nd = _sb_tiles(T)

    assert T // bk <= HD

    def body(q_ref, kf_ref, vf_ref, o_ref, l_ref, k_ref, v_ref):
        i = pl.program_id(1)

        @pl.when(i == 0)
        def _():
            k_ref[...] = kf_ref[...].astype(BF16)
            v_ref[...] = vf_ref[...].astype(BF16)

        q = q_ref[...].astype(BF16)
        U = (_iota((bk, bk), 0) > _iota((bk, bk), 1)).astype(BF16)
        UU = jnp.concatenate([U, U], axis=0)
        lane = _iota((bq, HD), 1)

        def blk(j, carry, masked):
            acc, accL, saved = carry
            _, _, vj, _, ln, _, a = _sb_block(i, j, masked, q, k_ref, v_ref, accL, bq, bk, UU)
            acc = acc + _dot(a.astype(BF16), vj)
            return acc, accL + jnp.sum(ln, axis=1, keepdims=True), jnp.where(lane == j, accL, saved)

        carry = (jnp.zeros((bq, HD), F32), jnp.zeros((bq, 1), F32), jnp.zeros((bq, HD), F32))
        for jj in range(nd):
            carry = blk(i * nd + (nd - 1 - jj), carry, True)
        carry = lax.fori_loop(0, i * nd, lambda it, c: blk(i * nd - 1 - it, c, False), carry)
        o_ref[...] = carry[0]
        l_ref[...] = carry[2]

    blk_q = pl.BlockSpec((bq, HD), lambda h, i: (i, h))
    return _pcall(
        body, grid=(H, T // bq),
        in_specs=[blk_q, pl.BlockSpec((T, HD), lambda h, i: (0, H + h)),
                  pl.BlockSpec((T, HD), lambda h, i: (0, 2 * H + h))],
        out_specs=[blk_q, blk_q],
        out_shape=[jax.ShapeDtypeStruct((T, H * HD), F32)] * 2,
        scratch_shapes=[pltpu.VMEM((T, HD), BF16), pltpu.VMEM((T, HD), BF16)],
        compiler_params=_cparams(("parallel", "arbitrary")), name="sb_fwd")(proj, proj, proj)


def _sb_bwd(proj, l_sb, do_sb, T, H):
    bq, bk, nd = _sb_tiles(T)
    nq = T // bq

    def body(q_ref, kf_ref, vf_ref, l_ref, do_ref, dq_ref, dk_ref, dv_ref, dk_acc, dv_acc, k_ref, v_ref):
        i = pl.program_id(1)

        @pl.when(i == 0)
        def _():
            dk_acc[...] = jnp.zeros_like(dk_acc)
            dv_acc[...] = jnp.zeros_like(dv_acc)
            k_ref[...] = kf_ref[...].astype(BF16)
            v_ref[...] = vf_ref[...].astype(BF16)

        q = q_ref[...].astype(BF16)
        dob = do_ref[...].astype(BF16)
        saved = l_ref[...]
        lane = _iota((bq, HD), 1)
        U = (_iota((bk, bk), 0) > _iota((bk, bk), 1)).astype(BF16)
        UU = jnp.concatenate([U, U], axis=0)
        Ue = (_iota((bk, bk), 0) < _iota((bk, bk), 1)).astype(BF16)
        UUe = jnp.concatenate([Ue, Ue], axis=0)

        def blk(j, carry, masked):
            dq, accP = carry
            accL = jnp.sum(jnp.where(lane == j, saved, 0.0), axis=1, keepdims=True)
            r, kj, vj, mask, _, lb, a = _sb_block(i, j, masked, q, k_ref, v_ref, accL, bq, bk, UU)
            p = a * _dot(dob, vj, NT)
            hi = p.astype(BF16)
            lo = (p - hi.astype(F32)).astype(BF16)
            pre = _dot(jnp.concatenate([hi, lo], axis=1), UUe) + accP
            sig = jnp.exp(lb)
            dz = p * (1.0 - sig) - pre * sig
            if masked:
                dz = jnp.where(mask, dz, 0.0)
            dzb = (dz * (HD ** -0.5)).astype(BF16)
            dq = dq + _dot(dzb, kj)
            dk_acc[r, :] += _dot(dzb, q, TN)
            dv_acc[r, :] += _dot(a.astype(BF16), dob, TN)
            return dq, accP + jnp.sum(p, axis=1, keepdims=True)

        carry = (jnp.zeros((bq, HD), F32), jnp.zeros((bq, 1), F32))
        carry = lax.fori_loop(0, i * nd, lambda j, c: blk(j, c, False), carry)
        for jj in range(nd):
            carry = blk(i * nd + jj, carry, True)
        dq_ref[...] = carry[0].astype(BF16)

        @pl.when(i == nq - 1)
        def _():
            dk_ref[...] = dk_acc[...].astype(BF16)
            dv_ref[...] = dv_acc[...].astype(BF16)

    W = H * HD
    blk_q = pl.BlockSpec((bq, HD), lambda h, i: (i, h))
    blk_t = pl.BlockSpec((T, HD), lambda h, i: (0, h))
    return _pcall(
        body, grid=(H, nq),
        in_specs=[blk_q, pl.BlockSpec((T, HD), lambda h, i: (0, H + h)),
                  pl.BlockSpec((T, HD), lambda h, i: (0, 2 * H + h)), blk_q, blk_q],
        out_specs=[blk_q, blk_t, blk_t],
        out_shape=[jax.ShapeDtypeStruct((T, W), BF16)] * 3,
        scratch_shapes=[pltpu.VMEM((T, HD), F32), pltpu.VMEM((T, HD), F32),
                        pltpu.VMEM((T, HD), BF16), pltpu.VMEM((T, HD), BF16)],
        compiler_params=_cparams(("parallel", "arbitrary")), name="sb_bwd")(proj, proj, proj, l_sb, do_sb)


def _gdn_chunk(q, k, v, bb, gb, S):
    C = q.shape[0]
    ri, ci = _iota((C, C), 0), _iota((C, C), 1)
    tril, strict = ri >= ci, ri > ci
    e0 = (_iota((C, HD), 1) == 0).astype(F32)
    g_col = _dotf(gb, e0, NT)
    g_row = _dotf(e0, gb, NT)
    decay = jnp.where(tril, jnp.exp(jnp.where(tril, g_col - g_row, 0.0)), 0.0)
    eg = jnp.exp(gb)
    qs = q * (HD ** -0.5)
    kb = k * bb
    lw = jnp.where(strict, _bdot_nt(kb, k) * decay, 0.0)
    x = (ri == ci).astype(F32) - lw
    pw = lw
    for _ in range(C.bit_length() - 2):
        pw = _dotf(pw, pw)
        x = x + _dotf(x, pw)
    u = _bdot_nn(x, v * bb)
    w = _bdot_nn(x, kb * eg)
    aq = jnp.where(tril, _bdot_nt(qs, k) * decay, 0.0)
    vnew = u - _bdot_nn(w, S)
    o = _bdot_nn(qs * eg, S) + _bdot_nn(aq, vnew)
    g_last = _dotf((_iota((C, C), 1) == C - 1).astype(F32), gb)
    g_last_s = _dotf((_iota((HD, C), 1) == C - 1).astype(F32), gb)
    s_new = S * jnp.exp(g_last_s) + _bdot_tn(k * jnp.exp(g_last - gb), vnew)
    return o, s_new


def _gdn_fwd(qn, kn, vc, beta_b, g_b, T, H):
    tb = min(GDN_TB, T)
    nc = tb // CH

    def body(q_ref, k_ref, v_ref, b_ref, g_ref, o_ref, s_ref, s_scr):
        @pl.when(pl.program_id(1) == 0)
        def _():
            s_scr[...] = jnp.zeros_like(s_scr)

        def step(c, carry):
            r = pl.ds(pl.multiple_of(c * CH, CH), CH)
            s = s_scr[...]
            s_ref[c] = s
            o, s2 = _gdn_chunk(q_ref[r, :], k_ref[r, :], v_ref[r, :], b_ref[r, :], g_ref[r, :], s)
            o_ref[r, :] = o
            s_scr[...] = s2
            return carry

        lax.fori_loop(0, nc, step, 0)

    blk = pl.BlockSpec((tb, HD), lambda h, t: (t, h))
    blk3 = pl.BlockSpec((None, tb, HD), lambda h, t: (h, t, 0))
    return _pcall(
        body, grid=(H, T // tb), in_specs=[blk, blk, blk, blk3, blk3],
        out_specs=[blk, pl.BlockSpec((None, nc, HD, HD), lambda h, t: (h, t, 0, 0))],
        out_shape=[jax.ShapeDtypeStruct((T, H * HD), F32), jax.ShapeDtypeStruct((H, T // CH, HD, HD), F32)],
        scratch_shapes=[pltpu.VMEM((HD, HD), F32)],
        compiler_params=_cparams(("parallel", "arbitrary")), name="gdn_fwd")(qn, kn, vc, beta_b, g_b)


def _gdn_bwd(qn, kn, vc, beta_b, g_b, s_all, do, T, H):
    tb = min(GDN_TB, T)
    nc = tb // CH
    nt = T // tb

    def body(q_ref, k_ref, v_ref, b_ref, g_ref, s_ref, do_ref, dq_ref, dk_ref, dv_ref, db_ref, dg_ref, ds_scr):
        @pl.when(pl.program_id(1) == 0)
        def _():
            ds_scr[...] = jnp.zeros_like(ds_scr)

        def step(it, carry):
            c = nc - 1 - it
            r = pl.ds(pl.multiple_of(c * CH, CH), CH)
            _, vjp = jax.vjp(_gdn_chunk, q_ref[r, :], k_ref[r, :], v_ref[r, :], b_ref[r, :], g_ref[r, :], s_ref[c])
            dq, dk, dv, db, dg, ds = vjp((do_ref[r, :], ds_scr[...]))
            dq_ref[r, :] = dq
            dk_ref[r, :] = dk
            dv_ref[r, :] = dv
            db_ref[r, :] = db
            dg_ref[r, :] = dg
            ds_scr[...] = ds
            return carry

        lax.fori_loop(0, nc, step, 0)

    blk = pl.BlockSpec((tb, HD), lambda h, t: (nt - 1 - t, h))
    blk3 = pl.BlockSpec((None, tb, HD), lambda h, t: (h, nt - 1 - t, 0))
    W = H * HD
    return _pcall(
        body, grid=(H, nt),
        in_specs=[blk, blk, blk, blk3, blk3, pl.BlockSpec((None, nc, HD, HD), lambda h, t: (h, nt - 1 - t, 0, 0)), blk],
        out_specs=[blk, blk, blk, blk3, blk3],
        out_shape=[jax.ShapeDtypeStruct((T, W), F32)] * 3 + [jax.ShapeDtypeStruct((H, T, HD), F32)] * 2,
        scratch_shapes=[pltpu.VMEM((HD, HD), F32)],
        compiler_params=_cparams(("parallel", "arbitrary")), name="gdn_bwd")(qn, kn, vc, beta_b, g_b, s_all, do)


def _conv_tiles(T, C):
    return min(HEAD_TM, T), min(CONV_CW, C)


def _shift_down(main, halo, first, d):
    halo = jnp.where(first, 0.0, halo)
    ext = jnp.concatenate([halo, main], axis=0)
    return pltpu.roll(ext, d, 0)[8:]


def _conv_fwd(proj, conv_w, T, H):
    W = H * HD
    C = 3 * W
    tm, cw = _conv_tiles(T, C)
    col0 = 4 * W // cw

    def body(x_ref, h_ref, w_ref, o_ref):
        first = pl.program_id(1) == 0
        main = x_ref[...]
        halo = h_ref[...]
        w = w_ref[...]
        out = w[CONV_K - 1:CONV_K, :] * main
        for d in range(1, CONV_K):
            out = out + w[CONV_K - 1 - d:CONV_K - d, :] * _shift_down(main, halo, first, d)
        o_ref[...] = out

    return _pcall(
        body, grid=(C // cw, T // tm),
        in_specs=[pl.BlockSpec((tm, cw), lambda cb, i: (i, col0 + cb)),
                  pl.BlockSpec((8, cw), lambda cb, i: (jnp.maximum(i * (tm // 8) - 1, 0), col0 + cb)),
                  pl.BlockSpec((CONV_K, cw), lambda cb, i: (0, cb))],
        out_specs=pl.BlockSpec((tm, cw), lambda cb, i: (i, cb)),
        out_shape=jax.ShapeDtypeStruct((T, C), F32),
        compiler_params=_cparams(("parallel", "arbitrary")), name="conv_fwd")(proj, proj, conv_w)


def _conv_bwd(proj, conv_w, dconv, T, H):
    W = H * HD
    C = 3 * W
    tm, cw = _conv_tiles(T, C)
    col0 = 4 * W // cw
    nt = T // tm

    def body(x_ref, h_ref, w_ref, d_ref, dn_ref, dx_ref, dw_ref):
        i = pl.program_id(1)
        first = i == 0
        main = x_ref[...]
        halo = h_ref[...]
        w = w_ref[...]
        dmain = d_ref[...]
        dnext = jnp.where(i == nt - 1, 0.0, dn_ref[...])
        dext = jnp.concatenate([dmain, dnext], axis=0)
        dx = w[CONV_K - 1:CONV_K, :] * dmain
        rows = [jnp.sum(dmain * main, axis=0, keepdims=True)]
        for d in range(1, CONV_K):
            dx = dx + w[CONV_K - 1 - d:CONV_K - d, :] * pltpu.roll(dext, tm + 8 - d, 0)[:tm]
            rows.append(jnp.sum(dmain * _shift_down(main, halo, first, d), axis=0, keepdims=True))
        dx_ref[...] = dx.astype(BF16)

        @pl.when(first)
        def _():
            dw_ref[...] = jnp.zeros_like(dw_ref)

        for d in range(CONV_K):
            dw_ref[CONV_K - 1 - d:CONV_K - d, :] += rows[d]

    return _pcall(
        body, grid=(C // cw, nt),
        in_specs=[pl.BlockSpec((tm, cw), lambda cb, i: (i, col0 + cb)),
                  pl.BlockSpec((8, cw), lambda cb, i: (jnp.maximum(i * (tm // 8) - 1, 0), col0 + cb)),
                  pl.BlockSpec((CONV_K, cw), lambda cb, i: (0, cb)),
                  pl.BlockSpec((tm, cw), lambda cb, i: (i, cb)),
                  pl.BlockSpec((8, cw), lambda cb, i: (jnp.minimum((i + 1) * (tm // 8), T // 8 - 1), cb))],
        out_specs=[pl.BlockSpec((tm, cw), lambda cb, i: (i, cb)), pl.BlockSpec((CONV_K, cw), lambda cb, i: (0, cb))],
        out_shape=[jax.ShapeDtypeStruct((T, C), BF16), jax.ShapeDtypeStruct((CONV_K, C), F32)],
        compiler_params=_cparams(("parallel", "arbitrary")), name="conv_bwd")(proj, proj, conv_w, dconv, dconv)


def _prep_fn(H, tm):
    def fn(cq, ck, cv, small, al, dtb):
        h = pl.program_id(1)
        lane = _iota((HD, HD), 0)
        oh_b = (lane == h).astype(F32)
        oh_a = (lane == H + h).astype(F32)
        q, k, v = _silu(cq), _silu(ck), _silu(cv)
        qn = q * lax.rsqrt(jnp.sum(q * q, axis=-1, keepdims=True) + L2_EPS)
        kn = k * lax.rsqrt(jnp.sum(k * k, axis=-1, keepdims=True) + L2_EPS)
        beta = _dotf(_sigmoid(small), oh_b)
        g = _dotf(-jnp.exp(al) * _softplus(small + dtb), oh_a)
        ri, ci = _iota((tm, tm), 0), _iota((tm, tm), 1)
        sh = CH.bit_length() - 1
        tri = jnp.logical_and(ri >= ci, jnp.right_shift(ri, sh) == jnp.right_shift(ci, sh)).astype(F32)
        return qn, kn, v, beta, _dotf(tri, g)
    return fn


def _prep_specs(T, H):
    tm = min(HEAD_TM, T)
    W = H * HD
    row = lambda s: pl.BlockSpec((tm, HD), lambda i, h, s=s: (i, s * H + h))
    small = pl.BlockSpec((tm, HD), lambda i, h: (i, 0))
    par = pl.BlockSpec((1, HD), lambda i, h: (0, 0))
    blk = pl.BlockSpec((tm, HD), lambda i, h: (i, h))
    blk3 = pl.BlockSpec((None, tm, HD), lambda i, h: (h, i, 0))
    return tm, W, row, small, par, blk, blk3


def _prep_fwd(conv, small, al, dtb, T, H):
    tm, W, row, sm, par, blk, blk3 = _prep_specs(T, H)
    return _ew(_prep_fn(H, tm), (T // tm, H), [conv, conv, conv, small, al, dtb],
               [row(0), row(1), row(2), sm, par, par],
               [((T, W), F32)] * 3 + [((H, T, HD), F32)] * 2, [blk, blk, blk, blk3, blk3], name="gdn_prep")


def _prep_bwd(conv, small, al, dtb, dqn, dkn, dvc, dbeta, dg, T, H):
    tm, W, row, sm, par, blk, blk3 = _prep_specs(T, H)
    f = _prep_fn(H, tm)

    def fn(cq, ck, cv, small_, al_, dtb_, a, b, c, d, e):
        _, vjp = jax.vjp(f, cq, ck, cv, small_, al_, dtb_)
        return vjp((a, b, c, d, e))

    return _ew(fn, (T // tm, H), [conv, conv, conv, small, al, dtb, dqn, dkn, dvc, dbeta, dg],
               [row(0), row(1), row(2), sm, par, par, blk, blk, blk, blk3, blk3],
               [((T, W), F32)] * 3 + [((T, HD), F32), ((1, HD), F32), ((1, HD), F32)],
               [blk, blk, blk, sm, par, par], acc=[None, None, None, 'inner', 'all', 'all'], name="gdn_prep_bwd")


def _gate_fn(o_sb, z_sb, o_g, z_g, gnw):
    a_sb = o_sb * _silu(z_sb)
    a_g = _rms(o_g, gnw) * _silu(z_g)
    return a_sb, a_g


def _gate_specs(T, H):
    tm = min(HEAD_TM, T)
    blk = pl.BlockSpec((tm, HD), lambda i, h: (i, h))
    sec = lambda s: pl.BlockSpec((tm, HD), lambda i, h, s=s: (i, s * H + h))
    par = pl.BlockSpec((1, HD), lambda i, h: (0, 0))
    return tm, blk, sec, par


def _gate_fwd(o_sb, proj, o_g, gnw, T, H):
    tm, blk, sec, par = _gate_specs(T, H)
    W = H * HD
    return _ew(_gate_fn, (T // tm, H), [o_sb, proj, o_g, proj, gnw], [blk, sec(3), blk, sec(7), par],
               [((T, W), BF16)] * 2, [blk, blk], name="gate_fwd")


def _gate_bwd(o_sb, proj, o_g, gnw, da_sb, da_g, T, H):
    tm, blk, sec, par = _gate_specs(T, H)
    W = H * HD

    def fn(o_sb_, z_sb, o_g_, z_g, gnw_, da, db):
        _, vjp = jax.vjp(_gate_fn, o_sb_, z_sb, o_g_, z_g, gnw_)
        return vjp((da, db))

    return _ew(fn, (T // tm, H), [o_sb, proj, o_g, proj, gnw, da_sb, da_g], [blk, sec(3), blk, sec(7), par, blk, blk],
               [((T, W), F32), ((T, W), BF16), ((T, W), F32), ((T, W), BF16), ((1, HD), F32)],
               [blk, blk, blk, blk, par], acc=[None, None, None, None, 'all'], name="gate_bwd")


def _ada_fn(x, nw, scale, shift):
    return _rms(x, nw) * (1.0 + scale) + shift


def _row_specs(T, D):
    tm = min(EW_TM, T)
    return tm, pl.BlockSpec((tm, D), lambda i: (i, 0)), pl.BlockSpec((1, D), lambda i: (0, 0))


def _ada_fwd(x, nw, scale, shift):
    T, D = x.shape
    tm, row, par = _row_specs(T, D)
    return _ew(_ada_fn, (T // tm,), [x, nw, scale, shift], [row, par, par, par], [((T, D), BF16)], [row],
               name="ada_norm")[0]


def _ada_bwd(x, nw, scale, shift, dh, dx2):
    T, D = x.shape
    tm, row, par = _row_specs(T, D)

    def fn(x_, nw_, sc_, sh_, dh_, dx2_):
        _, vjp = jax.vjp(_ada_fn, x_, nw_, sc_, sh_)
        dx, dnw, dsc, dsh = vjp(dh_)
        return dx + dx2_, dnw, dsc, dsh

    return _ew(fn, (T // tm,), [x, nw, scale, shift, dh, dx2], [row, par, par, par, row, row],
               [((T, D), F32)] + [((1, D), F32)] * 3, [row, par, par, par], acc=[None, 'all', 'all', 'all'],
               name="ada_norm_bwd")


def _merge_fn(m_sb, m_g, p_sb, p_g):
    return _sigmoid(m_sb) * p_sb + _sigmoid(m_g) * p_g


def _merge_specs(T, D, H):
    tm, tc = min(HEAD_TM, T), min(512, D)
    nb = D // tc
    blk = pl.BlockSpec((tm, tc), lambda i, j: (i, j))
    sec = lambda s: pl.BlockSpec((tm, tc), lambda i, j, s=s: (i, s * nb + j))
    return tm, tc, blk, sec


def _merge_fwd(proj, p_sb, p_g, T, D, H):
    tm, tc, blk, sec = _merge_specs(T, D, H)
    return _ew(_merge_fn, (T // tm, D // tc), [proj, proj, p_sb, p_g], [sec(8), sec(9), blk, blk],
               [((T, D), BF16)], [blk], name="merge")[0]


def _merge_bwd(proj, p_sb, p_g, dy, T, D, H):
    tm, tc, blk, sec = _merge_specs(T, D, H)

    def fn(m_sb, m_g, p_sb_, p_g_, dy_):
        _, vjp = jax.vjp(_merge_fn, m_sb, m_g, p_sb_, p_g_)
        return vjp(dy_)

    return _ew(fn, (T // tm, D // tc), [proj, proj, p_sb, p_g, dy], [sec(8), sec(9), blk, blk, blk],
               [((T, D), BF16)] * 4, [blk] * 4, name="merge_bwd")


def _loss_head(x, u, gate, fnw, tgt):
    T, D = x.shape
    tm, row, par = _row_specs(T, D)

    def loss(x_, u_, gate_, fnw_, tgt_):
        y = _rms(x_ + gate_ * u_, fnw_)
        return 0.5 * jnp.sum(jnp.mean(jnp.square(y - tgt_), axis=-1))

    def fn(x_, u_, gate_, fnw_, tgt_):
        val, (dx, du, dgate, dfnw) = jax.value_and_grad(loss, argnums=(0, 1, 2, 3))(x_, u_, gate_, fnw_, tgt_)
        return jnp.full((1, HD), val, F32), dx, du, dgate, dfnw

    return _ew(fn, (T // tm,), [x, u, gate, fnw, tgt], [row, row, par, par, row],
               [((1, HD), F32), ((T, D), F32), ((T, D), BF16), ((1, D), F32), ((1, D), F32)],
               [pl.BlockSpec((1, HD), lambda i: (0, 0)), row, row, par, par],
               acc=['all', None, None, 'all', 'all'], name="loss_head")


def _mod_part(c_all, w_ada, b_ada):
    D, N = w_ada.shape
    tn = min(512, N)

    def fn(c, w, b):
        return _dot(_silu(c).astype(BF16), w.astype(BF16)) + b

    return _ew(fn, (N // tn,), [c_all, w_ada, b_ada],
               [pl.BlockSpec((8, D), lambda j: (0, 0)), pl.BlockSpec((D, tn), lambda j: (0, j)),
                pl.BlockSpec((1, tn), lambda j: (0, j))],
               [((8, N), F32)], [pl.BlockSpec((8, tn), lambda j: (0, j))], name="ada_mod")[0]


def _w_ada_grad(c_all, dmod):
    D, N = c_all.shape[1], dmod.shape[1]
    tn = min(512, N)

    def fn(c, dm):
        return _dot(_silu(c).astype(BF16), dm.astype(BF16), TN)

    return _ew(fn, (N // tn,), [c_all, dmod],
               [pl.BlockSpec((8, D), lambda j: (0, 0)), pl.BlockSpec((8, tn), lambda j: (0, j))],
               [((D, N), F32)], [pl.BlockSpec((D, tn), lambda j: (0, j))], name="w_ada_grad")[0]


def _sum8(packs):
    N = packs.shape[1]

    def fn(p):
        s = p[0:8]
        for d in range(1, 8):
            s = s + p[8 * d:8 * d + 8]
        return s

    return _ew(fn, (1,), [packs], [pl.BlockSpec((64, N), lambda i: (0, 0))], [((8, N), F32)],
               [pl.BlockSpec((8, N), lambda i: (0, 0))], name="sum_devices")[0]


def _sum_chips(q, name):
    _, R, C = q.shape
    tr = min(64, R)

    def fn(p):
        p = p.astype(F32)
        return (p[0] + p[1]) + (p[2] + p[3])

    return _ew(fn, (R // tr,), [q], [pl.BlockSpec((4, tr, C), lambda i: (0, i, 0))], [((R, C), F32)],
               [pl.BlockSpec((tr, C), lambda i: (i, 0))], name=name)[0]


def _add_halves(mine, rsib, name):
    _, rh, C = mine.shape
    tr = min(64, rh)
    blk = pl.BlockSpec((None, tr, C), lambda j, i: (j, i, 0))

    def fn(a, b):
        return a + b

    return _ew(fn, (4, rh // tr), [mine, rsib], [blk, blk], [((4, rh, C), BF16)], [blk], name=name)[0]


def _adamw(w, g, m, v, name):
    R, C = w.shape
    tr = R if R <= 64 else 64
    blk = pl.BlockSpec((tr, C), lambda i: (i, 0))

    def fn(w_, g_, m_, v_):
        m2 = ADAM_B1 * m_ + (1.0 - ADAM_B1) * g_
        v2 = ADAM_B2 * v_ + (1.0 - ADAM_B2) * jnp.square(g_)
        m_hat = m2 / (1.0 - ADAM_B1 ** ADAM_STEP)
        v_hat = v2 / (1.0 - ADAM_B2 ** ADAM_STEP)
        delta = -ADAM_LR * (m_hat / (jnp.sqrt(v_hat) + ADAM_EPS) + ADAM_WD * w_)
        return delta, m2, v2

    return _ew(fn, (R // tr,), [w, g, m, v], [blk] * 4, [((R, C), F32)] * 3, [blk] * 3, name=name)


def _place():
    x, y, c = lax.axis_index("x"), lax.axis_index("y"), lax.axis_index("c")
    return x, y, c, [(1 - x, y), (x, 1 - y), (1 - x, 1 - y)]


ANY = pl.BlockSpec(memory_space=pl.ANY)


def _allgather8(blk):
    m_per, n = blk.shape

    def body(x_ref, out_ref, send_sems, recv_sems, local_sem):
        x, y, c, chips = _place()
        me, sibling = (x, y, c), (x, y, 1 - c)

        def rows(px, py, pc):
            return out_ref.at[pl.ds((4 * px + 2 * py + pc) * m_per, m_per), :]

        def copy(k, block, to, src=None):
            return pltpu.make_async_remote_copy(
                src_ref=rows(*block) if src is None else src, dst_ref=rows(*block),
                send_sem=send_sems.at[k], recv_sem=recv_sems.at[k], device_id=to, device_id_type=MESH)

        mine = pltpu.make_async_copy(x_ref, rows(*me), local_sem)
        mine.start()
        first = [copy(0, me, sibling, src=x_ref)]
        first += [copy(1 + j, me, (*chip, c), src=x_ref) for j, chip in enumerate(chips)]
        for cp in first:
            cp.start()
        passed = [copy(4 + j, (*chip, c), sibling) for j, chip in enumerate(chips)]
        for j, chip in enumerate(chips):
            copy(1 + j, (*chip, c), me).wait_recv()
            passed[j].start()
        copy(0, sibling, me).wait_recv()
        for j, chip in enumerate(chips):
            copy(4 + j, (*chip, 1 - c), me).wait_recv()
        for cp in first + passed:
            cp.wait_send()
        mine.wait()

    vm = pl.BlockSpec(memory_space=pltpu.VMEM)
    return _pcall(body, out_shape=jax.ShapeDtypeStruct((8 * m_per, n), blk.dtype), in_specs=[vm], out_specs=vm,
                  scratch_shapes=[pltpu.SemaphoreType.DMA((7,)), pltpu.SemaphoreType.DMA((7,)),
                                  pltpu.SemaphoreType.DMA],
                  compiler_params=pltpu.CompilerParams(vmem_limit_bytes=VMEM_LIMIT), name="allgather8")(blk)


def _gather_weights(shards):
    n = len(shards)

    def body(*refs):
        ins, outs = refs[:n], refs[n:2 * n]
        send_sems, recv_sems, fsend_sems, frecv_sems, local_sems = refs[2 * n:]
        x, y, c, chips = _place()
        mychip = 2 * x + y
        locs, sends, fwds = [], [], []
        for a in range(n):
            rh = shards[a].shape[0] // 2
            loc = pltpu.make_async_copy(ins[a], outs[a].at[mychip], local_sems.at[a])
            loc.start()
            locs.append(loc)
            for k, chip in enumerate(chips):
                cp = pltpu.make_async_remote_copy(
                    src_ref=ins[a].at[pl.ds(c * rh, rh), :], dst_ref=outs[a].at[mychip, pl.ds(c * rh, rh), :],
                    send_sem=send_sems.at[3 * a + k], recv_sem=recv_sems.at[3 * a + k],
                    device_id=(*chip, c), device_id_type=MESH)
                cp.start()
                sends.append(cp)
        for a in range(n):
            rh = shards[a].shape[0] // 2
            for k, (px, py) in enumerate(chips):
                land = outs[a].at[2 * px + py, pl.ds(c * rh, rh), :]
                pltpu.make_async_remote_copy(
                    src_ref=land, dst_ref=land, send_sem=send_sems.at[3 * a + k], recv_sem=recv_sems.at[3 * a + k],
                    device_id=(px, py, c), device_id_type=MESH).wait_recv()
                fw = pltpu.make_async_remote_copy(
                    src_ref=land, dst_ref=land, send_sem=fsend_sems.at[3 * a + k], recv_sem=frecv_sems.at[3 * a + k],
                    device_id=(x, y, 1 - c), device_id_type=MESH)
                fw.start()
                fwds.append(fw)
        for a in range(n):
            rh = shards[a].shape[0] // 2
            for k, (px, py) in enumerate(chips):
                land = outs[a].at[2 * px + py, pl.ds((1 - c) * rh, rh), :]
                pltpu.make_async_remote_copy(
                    src_ref=land, dst_ref=land, send_sem=fsend_sems.at[3 * a + k], recv_sem=frecv_sems.at[3 * a + k],
                    device_id=(x, y, 1 - c), device_id_type=MESH).wait_recv()
        for cp in sends + fwds:
            cp.wait_send()
        for loc in locs:
            loc.wait()

    return _pcall(body, out_shape=[jax.ShapeDtypeStruct((4,) + s.shape, s.dtype) for s in shards],
                  in_specs=[ANY] * n, out_specs=[ANY] * n,
                  scratch_shapes=[pltpu.SemaphoreType.DMA((3 * n,))] * 4 + [pltpu.SemaphoreType.DMA((n,))],
                  name="gather_weights")(*shards)


def _swap_partial_halves(gs):
    n = len(gs)

    def body(*refs):
        ins, mine, outs = refs[:n], refs[n:2 * n], refs[2 * n:3 * n]
        send_sems, recv_sems, local_sems = refs[3 * n:]
        x, y, c, _ = _place()
        cps = []
        for a in range(n):
            rh = gs[a].shape[1] // 2
            loc = pltpu.make_async_copy(ins[a].at[:, pl.ds(c * rh, rh), :], mine[a], local_sems.at[a])
            loc.start()
            cp = pltpu.make_async_remote_copy(
                src_ref=ins[a].at[:, pl.ds((1 - c) * rh, rh), :], dst_ref=outs[a],
                send_sem=send_sems.at[a], recv_sem=recv_sems.at[a], device_id=(x, y, 1 - c), device_id_type=MESH)
            cp.start()
            cps.append((loc, cp))
        for loc, cp in cps:
            cp.wait()
            loc.wait()

    half = [jax.ShapeDtypeStruct((4, g.shape[1] // 2, g.shape[2]), g.dtype) for g in gs]
    res = _pcall(body, out_shape=half + half, in_specs=[ANY] * n, out_specs=[ANY] * (2 * n),
                 scratch_shapes=[pltpu.SemaphoreType.DMA((n,))] * 3, name="swap_partial_halves")(*gs)
    return res[:n], res[n:]


def _chip_scatter(ps):
    n = len(ps)

    def body(*refs):
        ins, outs = refs[:n], refs[n:2 * n]
        send_sems, recv_sems, local_sems = refs[2 * n:]
        x, y, c, chips = _place()
        mychip = 2 * x + y
        cps = []
        for a in range(n):
            loc = pltpu.make_async_copy(ins[a].at[mychip], outs[a].at[mychip], local_sems.at[a])
            loc.start()
            cps.append(loc)
            for k, (px, py) in enumerate(chips):
                cp = pltpu.make_async_remote_copy(
                    src_ref=ins[a].at[2 * px + py], dst_ref=outs[a].at[mychip],
                    send_sem=send_sems.at[3 * a + k], recv_sem=recv_sems.at[3 * a + k],
                    device_id=(px, py, c), device_id_type=MESH)
                cp.start()
                cps.append(cp)
        for a in range(n):
            for k, (px, py) in enumerate(chips):
                land = outs[a].at[2 * px + py]
                pltpu.make_async_remote_copy(
                    src_ref=land, dst_ref=land, send_sem=send_sems.at[3 * a + k], recv_sem=recv_sems.at[3 * a + k],
                    device_id=(px, py, c), device_id_type=MESH).wait_recv()
        for a in range(n):
            cps[4 * a].wait()
            for k in range(3):
                cps[4 * a + 1 + k].wait_send()

    return _pcall(body, out_shape=[jax.ShapeDtypeStruct(p.shape, p.dtype) for p in ps],
                  in_specs=[ANY] * n, out_specs=[ANY] * n,
                  scratch_shapes=[pltpu.SemaphoreType.DMA((3 * n,))] * 2 + [pltpu.SemaphoreType.DMA((n,))],
                  name="chip_scatter")(*ps)


def _swap_final_halves(hs):
    n = len(hs)

    def body(*refs):
        ins, outs = refs[:n], refs[n:2 * n]
        send_sems, recv_sems, local_sems = refs[2 * n:]
        x, y, c, _ = _place()
        cps = []
        for a in range(n):
            loc = pltpu.make_async_copy(ins[a], outs[a].at[c], local_sems.at[a])
            loc.start()
            cp = pltpu.make_async_remote_copy(
                src_ref=ins[a], dst_ref=outs[a].at[c], send_sem=send_sems.at[a], recv_sem=recv_sems.at[a],
                device_id=(x, y, 1 - c), device_id_type=MESH)
            cp.start()
            cps.append((loc, cp))
        for a in range(n):
            land = outs[a].at[1 - c]
            pltpu.make_async_remote_copy(
                src_ref=land, dst_ref=land, send_sem=send_sems.at[a], recv_sem=recv_sems.at[a],
                device_id=(x, y, 1 - c), device_id_type=MESH).wait_recv()
        for loc, cp in cps:
            loc.wait()
            cp.wait_send()

    return _pcall(body, out_shape=[jax.ShapeDtypeStruct((2,) + h.shape, h.dtype) for h in hs],
                  in_specs=[ANY] * n, out_specs=[ANY] * n,
                  scratch_shapes=[pltpu.SemaphoreType.DMA((n,))] * 3, name="swap_final_halves")(*hs)


def _pad_cols(a, n):
    return jnp.pad(a, ((0, 0), (0, n - a.shape[1])))


def kernel(x, c, w_ada, b_ada, norm_w, w_in, gdn_conv_w, gdn_a_log, gdn_dt_bias, gdn_norm_w, w_proj_sb, w_proj_gdn, w_out, final_norm_w, loss_target, m_w_ada, m_b_ada, m_norm_w, m_w_in, m_gdn_conv_w, m_gdn_a_log, m_gdn_dt_bias, m_gdn_norm_w, m_w_proj_sb, m_w_proj_gdn, m_w_out, m_final_norm_w, v_w_ada, v_b_ada, v_norm_w, v_w_in, v_gdn_conv_w, v_gdn_a_log, v_gdn_dt_bias, v_gdn_norm_w, v_w_proj_sb, v_w_proj_gdn, v_w_out, v_final_norm_w):
    T, D = x.shape[1], x.shape[2]
    H = gdn_a_log.shape[1]
    W = H * HD
    assert W == D and T % CH == 0
    NA = w_ada.shape[2]
    NI = w_in.shape[2]
    CW = gdn_conv_w.shape[2]
    px, py, pc = lax.axis_index("x"), lax.axis_index("y"), lax.axis_index("c")
    chip = 2 * px + py
    me = 2 * chip + pc
    x2d, tgt = x[0], loss_target[0]
    PW = 3 * D

    pack1 = jnp.concatenate([c, _pad_cols(gdn_conv_w[0], D), jnp.zeros((3, D), F32)], axis=0)
    got1 = _allgather8(pack1).reshape(8, 8, D)
    c_all = got1[:, 0, :]
    conv_w = jnp.concatenate([got1[2 * j, 1:1 + CONV_K, :CW] for j in range(4)], axis=1)

    b_shard = lax.dynamic_slice_in_dim(b_ada, chip * NA, NA, axis=1)
    mod_part = _mod_part(c_all, w_ada[0], b_shard)
    got2 = _allgather8(mod_part).reshape(8, 8, NA)
    mod = jnp.concatenate([lax.dynamic_index_in_dim(got2[2 * j], me, 0) for j in range(4)], axis=1)
    shift, scale, gate = mod[:, :D], mod[:, D:2 * D], mod[:, 2 * D:]

    wg = _gather_weights([w_in[0].astype(BF16), w_proj_sb[0].astype(BF16), w_proj_gdn[0].astype(BF16),
                          w_out[0].astype(BF16)])
    w_in_full = jnp.concatenate([wg[0][j] for j in range(4)], axis=1)
    w_big = jnp.concatenate([w_in_full[:, :8 * W], w_in_full[:, 8 * W + 2 * H:]], axis=1)
    w_small = _pad_cols(w_in_full[:, 8 * W:8 * W + 2 * H], HD)
    w_psb, w_pg, w_o = (wg[i].reshape(D, D) for i in (1, 2, 3))

    h = _ada_fwd(x2d, norm_w, scale, shift)
    proj = _mm1(h, w_big, 'nn', F32, "proj_big")
    small = _mm1(h, w_small, 'nn', F32, "proj_small")
    o_sb, l_sb = _sb_fwd(proj, T, H)
    conv = _conv_fwd(proj, conv_w, T, H)
    al = jnp.pad(gdn_a_log, ((0, 0), (H, HD - 2 * H)))
    dtb = jnp.pad(gdn_dt_bias, ((0, 0), (H, HD - 2 * H)))
    qn, kn, vc, beta_b, g_b = _prep_fwd(conv, small, al, dtb, T, H)
    o_g, s_all = _gdn_fwd(qn, kn, vc, beta_b, g_b, T, H)
    a_sb, a_g = _gate_fwd(o_sb, proj, o_g, gdn_norm_w, T, H)
    p_sb = _mm1(a_sb, w_psb, 'nn', F32, "proj_sb")
    p_g = _mm1(a_g, w_pg, 'nn', F32, "proj_gdn")
    y = _merge_fwd(proj, p_sb, p_g, T, D, H)
    u = _mm1(y, w_o, 'nn', F32, "proj_out")
    loss_p, dx2, du, dgate, g_fnw = _loss_head(x2d, u, gate, final_norm_w.reshape(1, D), tgt)

    dy = _mm1(du, w_o, 'nt', F32, "d_merge")
    g_w_out = _mm1(y, du, 'tn', F32, "g_w_out")
    dm_sb, dm_g, dp_sb, dp_g = _merge_bwd(proj, p_sb, p_g, dy, T, D, H)
    da_sb = _mm1(dp_sb, w_psb, 'nt', F32, "d_a_sb")
    g_w_psb = _mm1(a_sb, dp_sb, 'tn', F32, "g_w_proj_sb")
    da_g = _mm1(dp_g, w_pg, 'nt', F32, "d_a_gdn")
    g_w_pg = _mm1(a_g, dp_g, 'tn', F32, "g_w_proj_gdn")
    do_sb, dz_sb, do_g, dz_g, g_gnw = _gate_bwd(o_sb, proj, o_g, gdn_norm_w, da_sb, da_g, T, H)
    dq_sb, dk_sb, dv_sb = _sb_bwd(proj, l_sb, do_sb, T, H)
    dqn, dkn, dvc, dbeta_b, dg_b = _gdn_bwd(qn, kn, vc, beta_b, g_b, s_all, do_g, T, H)
    dcq, dck, dcv, dsmall, dal, ddtb = _prep_bwd(conv, small, al, dtb, dqn, dkn, dvc, dbeta_b, dg_b, T, H)
    dpre, g_conv = _conv_bwd(proj, conv_w, jnp.concatenate([dcq, dck, dcv], axis=1), T, H)

    secs = [(dq_sb, 0), (dk_sb, W), (dv_sb, 2 * W), (dz_sb, 3 * W), (dpre, 4 * W), (dz_g, 7 * W),
            (dm_sb, 8 * W), (dm_g, 9 * W)]
    dh = _mm([(a, w_big, 0, off, a.shape[1]) for a, off in secs] + [(dsmall, w_small, 0, 0, HD)],
             'nt', F32, "d_h", tk=256)
    g_secs = [_mm1(h, a, 'tn', F32, "g_w_in_%d" % i) for i, (a, _) in enumerate(secs)]
    g_small = _mm1(h, dsmall, 'tn', F32, "g_w_in_small")
    grad_x, g_nw, dscale, dshift = _ada_bwd(x2d, norm_w, scale, shift, dh, dx2)

    dmod = jnp.concatenate([dshift, dscale, dgate], axis=1)
    misc = jnp.concatenate([g_nw, g_fnw, g_gnw, dal[:, H:2 * H], ddtb[:, H:2 * H], loss_p[:, :1]], axis=1)
    pack3 = jnp.concatenate([dmod, g_conv, _pad_cols(misc, PW), jnp.zeros((2, PW), F32)], axis=0)
    got3 = _allgather8(pack3)
    tot = _sum8(got3)
    dmod_all = got3.reshape(8, 8, PW)[:, 0, :]
    g_w_ada = _w_ada_grad(c_all, lax.dynamic_slice_in_dim(dmod_all, chip * NA, NA, axis=1))
    g_conv_sh = lax.dynamic_slice_in_dim(tot[1:1 + CONV_K], chip * CW, CW, axis=1)
    loss = tot[5, 2 * D + HD + 2 * H]

    g_in = jnp.concatenate(g_secs[:6] + [g_small[:, :2 * H]] + g_secs[6:], axis=1)
    g_full = [jnp.stack([g_in[:, j * NI:(j + 1) * NI] for j in range(4)]),
              g_w_psb.reshape(4, D // 4, D), g_w_pg.reshape(4, D // 4, D), g_w_out.reshape(4, D // 4, D)]
    g_mine, g_sib = _swap_partial_halves(g_full)
    parts = [_add_halves(g, r, "add_halves_%d" % i) for i, (g, r) in enumerate(zip(g_mine, g_sib))]
    got = _chip_scatter(parts)
    halves = [_sum_chips(q, "sum_chips_%d" % i) for i, q in enumerate(got)]
    g_red = [f.reshape(f.shape[1] * 2, f.shape[2]) for f in _swap_final_halves(halves)]

    out = {}

    def upd(name, w, g, m, v, shape):
        d_, m_, v_ = _adamw(w, g, m, v, "adamw_" + name)
        out[name] = (g.reshape(shape), d_.reshape(shape), m_.reshape(shape), v_.reshape(shape))

    upd("w_ada", w_ada[0], g_w_ada, m_w_ada[0], v_w_ada[0], w_ada.shape)
    upd("w_in", w_in[0], g_red[0], m_w_in[0], v_w_in[0], w_in.shape)
    upd("gdn_conv_w", gdn_conv_w[0], g_conv_sh, m_gdn_conv_w[0], v_gdn_conv_w[0], gdn_conv_w.shape)
    upd("w_proj_sb", w_proj_sb[0], g_red[1], m_w_proj_sb[0], v_w_proj_sb[0], w_proj_sb.shape)
    upd("w_proj_gdn", w_proj_gdn[0], g_red[2], m_w_proj_gdn[0], v_w_proj_gdn[0], w_proj_gdn.shape)
    upd("w_out", w_out[0], g_red[3], m_w_out[0], v_w_out[0], w_out.shape)

    def packs(b, nw, fnw, gnw, a, dt):
        row = jnp.concatenate([nw, fnw.reshape(1, D), gnw, a, dt], axis=1)
        return jnp.concatenate([b, _pad_cols(row, PW), jnp.zeros((6, PW), F32)], axis=0)

    g_pack = jnp.concatenate([tot[0:1], tot[5:6], jnp.zeros((6, PW), F32)], axis=0)
    d_, m_, v_ = _adamw(packs(b_ada, norm_w, final_norm_w, gdn_norm_w, gdn_a_log, gdn_dt_bias), g_pack,
                        packs(m_b_ada, m_norm_w, m_final_norm_w, m_gdn_norm_w, m_gdn_a_log, m_gdn_dt_bias),
                        packs(v_b_ada, v_norm_w, v_final_norm_w, v_gdn_norm_w, v_gdn_a_log, v_gdn_dt_bias),
                        "adamw_small")
    offs = {"norm_w": (0, D, (1, D)), "final_norm_w": (D, D, (D,)), "gdn_norm_w": (2 * D, HD, (1, HD)),
            "gdn_a_log": (2 * D + HD, H, (1, H)), "gdn_dt_bias": (2 * D + HD + H, H, (1, H))}
    out["b_ada"] = tuple(a[0:1] for a in (g_pack, d_, m_, v_))
    for name, (o, n_, shp) in offs.items():
        out[name] = tuple(a[1, o:o + n_].reshape(shp) for a in (g_pack, d_, m_, v_))

    names = ['w_ada', 'b_ada', 'norm_w', 'w_in', 'gdn_conv_w', 'gdn_a_log', 'gdn_dt_bias', 'gdn_norm_w',
             'w_proj_sb', 'w_proj_gdn', 'w_out', 'final_norm_w']
    return (loss, grad_x.reshape(x.shape), *[out[n][0] for n in names], *[out[n][1] for n in names],
            *[out[n][2] for n in names], *[out[n][3] for n in names])
```

```python
import functools

import jax
import jax.numpy as jnp
from jax import lax
from jax.experimental import pallas as pl
from jax.experimental.pallas import tpu as pltpu

F32 = jnp.float32
BF16 = jnp.bfloat16
HD = 128
CH = 64
CONV_K = 4
NORM_EPS = 1e-6
L2_EPS = 1e-6
ADAM_LR, ADAM_B1, ADAM_B2, ADAM_EPS, ADAM_WD, ADAM_STEP = 0.001, 0.9, 0.999, 1e-08, 0.01, 10

VMEM_LIMIT = 56 * 1024 * 1024
SB_BQ, SB_BK = 512, 256
GDN_TB = 256
GDN_G = 8
EW_TM = 256
HEAD_TM = 512
CONV_CW = 512
MM_TM, MM_TN, MM_TK = 1024, 1024, 512
SWAP_CHUNKS = 4
MESH = pl.DeviceIdType.MESH

NN = (((1,), (0,)), ((), ()))
NT = (((1,), (1,)), ((), ()))
TN = (((0,), (0,)), ((), ()))


def _pcall(body, **kw):
    return pl.pallas_call(body, **kw)


def _cparams(sem=None):
    return pltpu.CompilerParams(dimension_semantics=sem, vmem_limit_bytes=VMEM_LIMIT)


def _dot(a, b, dims=NN):
    return lax.dot_general(a, b, dims, preferred_element_type=F32)


def _dotf(a, b, dims=NN):
    return lax.dot_general(a, b, dims, precision=lax.Precision.HIGHEST, preferred_element_type=F32)


def _bdot_make(dims, da_rule, db_rule):
    @jax.custom_vjp
    def f(a, b):
        return _dot(a.astype(BF16), b.astype(BF16), dims)

    def fwd(a, b):
        return f(a, b), (a, b)

    def bwd(res, g):
        a, b = res
        return da_rule(g, a, b), db_rule(g, a, b)

    f.defvjp(fwd, bwd)
    return f


def _rdot(a, b, dims):
    return _dot(a.astype(BF16), b.astype(BF16), dims)


NNB = (((2,), (1,)), ((0,), (0,)))
NTB = (((2,), (2,)), ((0,), (0,)))
TNB = (((1,), (1,)), ((0,), (0,)))
_bdot_nn = _bdot_make(NNB, lambda g, a, b: _rdot(g, b, NTB), lambda g, a, b: _rdot(a, g, TNB))
_bdot_nt = _bdot_make(NTB, lambda g, a, b: _rdot(g, b, NNB), lambda g, a, b: _rdot(g, a, TNB))
_bdot_tn = _bdot_make(TNB, lambda g, a, b: _rdot(b, g, NTB), lambda g, a, b: _rdot(a, g, NNB))


def _iota(shape, axis):
    return lax.broadcasted_iota(jnp.int32, shape, axis)


def _sigmoid(x):
    e = jnp.exp(-jnp.abs(x))
    return jnp.where(x >= 0, 1.0 / (1.0 + e), e / (1.0 + e))


def _silu(x):
    return x * _sigmoid(x)


def _softplus(x):
    return jnp.maximum(x, 0.0) + jnp.log(1.0 + jnp.exp(-jnp.abs(x)))


def _rms(x, w):
    return x * lax.rsqrt(jnp.mean(x * x, axis=-1, keepdims=True) + NORM_EPS) * w


def _ew(fn, grid, ins, in_specs, outs, out_specs, acc=None, name=None):
    n_in = len(ins)
    acc = acc or [None] * len(outs)

    def body(*refs):
        vals = fn(*[r[...] for r in refs[:n_in]])
        if not isinstance(vals, (tuple, list)):
            vals = (vals,)
        for r, v, a in zip(refs[n_in:], vals, acc):
            if a is None:
                r[...] = v.astype(r.dtype)
                continue
            first = pl.program_id(len(grid) - 1) == 0
            if a == 'all':
                for ax in range(len(grid) - 1):
                    first = jnp.logical_and(first, pl.program_id(ax) == 0)

            @pl.when(first)
            def _():
                r[...] = v.astype(r.dtype)

            @pl.when(jnp.logical_not(first))
            def _():
                r[...] += v.astype(r.dtype)

    res = _pcall(body, grid=grid, in_specs=in_specs, out_specs=out_specs,
                 out_shape=[jax.ShapeDtypeStruct(s, d) for s, d in outs],
                 compiler_params=_cparams(("arbitrary",) * len(grid)), name=name)(*ins)
    return res


def _mm(pairs, mode, out_dtype, name, add=None, tm=None, tn=None, tk=None):
    a0, b0 = pairs[0][0], pairs[0][1]
    M = a0.shape[1] if mode == 'tn' else a0.shape[0]
    N = b0.shape[0] if mode == 'nt' else b0.shape[1]
    tm = min(tm or MM_TM, M)
    tn = min(tn or MM_TN, N)
    tk = tk or MM_TK
    tks = [min(tk, p[4]) for p in pairs]
    nks = [p[4] // t for p, t in zip(pairs, tks)]
    offs = [sum(nks[:i]) for i in range(len(pairs))]
    total = sum(nks)
    assert M % tm == 0 and N % tn == 0 and all(p[4] % t == 0 for p, t in zip(pairs, tks)), (name, M, N)
    dims = {'nn': NN, 'nt': NT, 'tn': TN}[mode]

    in_specs, ins = [], []
    for (a, b, ao, bo, kl), t, nk, off in zip(pairs, tks, nks, offs):
        def kidx(kk, nk=nk, off=off):
            return jnp.minimum(jnp.maximum(kk - off, 0), nk - 1)
        if mode == 'tn':
            in_specs.append(pl.BlockSpec((t, tm), lambda i, j, kk, f=kidx, o=ao // t: (o + f(kk), i)))
        else:
            in_specs.append(pl.BlockSpec((tm, t), lambda i, j, kk, f=kidx, o=ao // t: (i, o + f(kk))))
        if mode == 'nt':
            in_specs.append(pl.BlockSpec((tn, t), lambda i, j, kk, f=kidx, o=bo // t: (j, o + f(kk))))
        else:
            in_specs.append(pl.BlockSpec((t, tn), lambda i, j, kk, f=kidx, o=bo // t: (o + f(kk), j)))
        ins += [a, b]
    if add is not None:
        in_specs.append(pl.BlockSpec((tm, tn), lambda i, j, kk: (i, j)))
        ins.append(add)
    npairs = len(pairs)

    def body(*refs):
        out_ref, acc_ref = refs[-2], refs[-1]
        kk = pl.program_id(2)

        @pl.when(kk == 0)
        def _():
            acc_ref[...] = jnp.zeros_like(acc_ref)

        for p in range(npairs):
            def upd(p=p):
                acc_ref[...] += _dot(refs[2 * p][...].astype(BF16), refs[2 * p + 1][...].astype(BF16), dims)
            if npairs == 1:
                upd()
            else:
                pl.when(jnp.logical_and(kk >= offs[p], kk < offs[p] + nks[p]))(upd)

        @pl.when(kk == total - 1)
        def _():
            r = acc_ref[...]
            if add is not None:
                r = r + refs[2 * npairs][...]
            out_ref[...] = r.astype(out_ref.dtype)

    return _pcall(body, grid=(M // tm, N // tn, total), in_specs=in_specs,
                  out_specs=pl.BlockSpec((tm, tn), lambda i, j, kk: (i, j)),
                  out_shape=jax.ShapeDtypeStruct((M, N), out_dtype),
                  scratch_shapes=[pltpu.VMEM((tm, tn), F32)],
                  compiler_params=_cparams(("parallel", "parallel", "arbitrary")), name=name)(*ins)


def _mm1(a, b, mode, out_dtype, name, **kw):
    k = a.shape[0] if mode == 'tn' else a.shape[1]
    return _mm([(a, b, 0, 0, k)], mode, out_dtype, name, **kw)


def _sb_tiles(T):
    bq = min(SB_BQ, T)
    bk = min(SB_BK, bq)
    return bq, bk, bq // bk


def _sb_block(i, j, masked, q, k_ref, v_ref, accL, bq, bk, UU):
    r = pl.ds(pl.multiple_of(j * bk, bk), bk)
    kj = k_ref[r, :]
    vj = v_ref[r, :]
    z = _dot(q, kj, NT) * (HD ** -0.5)
    lb = jnp.minimum(z, 0.0) - jnp.log(1.0 + jnp.exp(-jnp.abs(z)))
    ln = lb - z
    mask = None
    if masked:
        mask = (j * bk + _iota((bq, bk), 1)) < (i * bq + _iota((bq, bk), 0))
        ln = jnp.where(mask, ln, 0.0)
    hi = ln.astype(BF16)
    lo = (ln - hi.astype(F32)).astype(BF16)
    later = _dot(jnp.concatenate([hi, lo], axis=1), UU) + accL
    a = jnp.exp(lb + later)
    if masked:
        a = jnp.where(mask, a, 0.0)
    return r, kj, vj, mask, ln, lb, a


def _sb_fwd(proj, T, H):
    bq, bk, nd = _sb_tiles(T)

    assert T // bk <= HD

    def body(q_ref, kf_ref, vf_ref, o_ref, l_ref, k_ref, v_ref):
        i = pl.program_id(1)

        @pl.when(i == 0)
        def _():
            k_ref[...] = kf_ref[...].astype(BF16)
            v_ref[...] = vf_ref[...].astype(BF16)

        q = q_ref[...].astype(BF16)
        U = (_iota((bk, bk), 0) > _iota((bk, bk), 1)).astype(BF16)
        UU = jnp.concatenate([U, U], axis=0)
        lane = _iota((bq, HD), 1)

        def blk(j, carry, masked):
            acc, accL, saved = carry
            _, _, vj, _, ln, _, a = _sb_block(i, j, masked, q, k_ref, v_ref, accL, bq, bk, UU)
            acc = acc + _dot(a.astype(BF16), vj)
            return acc, accL + jnp.sum(ln, axis=1, keepdims=True), jnp.where(lane == j, accL, saved)

        carry = (jnp.zeros((bq, HD), F32), jnp.zeros((bq, 1), F32), jnp.zeros((bq, HD), F32))
        for jj in range(nd):
            carry = blk(i * nd + (nd - 1 - jj), carry, True)
        def group(it, c):
            for jj in range(nd):
                c = blk((i - 1 - it) * nd + (nd - 1 - jj), c, False)
            return c

        carry = lax.fori_loop(0, i, group, carry)
        o_ref[...] = carry[0]
        l_ref[...] = carry[2]

    blk_q = pl.BlockSpec((bq, HD), lambda h, i: (i, h))
    return _pcall(
        body, grid=(H, T // bq),
        in_specs=[blk_q, pl.BlockSpec((T, HD), lambda h, i: (0, H + h)),
                  pl.BlockSpec((T, HD), lambda h, i: (0, 2 * H + h))],
        out_specs=[blk_q, blk_q],
        out_shape=[jax.ShapeDtypeStruct((T, H * HD), F32)] * 2,
        scratch_shapes=[pltpu.VMEM((T, HD), BF16), pltpu.VMEM((T, HD), BF16)],
        compiler_params=_cparams(("parallel", "arbitrary")), name="sb_fwd")(proj, proj, proj)


def _sb_bwd(proj, l_sb, do_sb, T, H):
    bq, bk, nd = _sb_tiles(T)
    nq = T // bq

    def body(q_ref, kf_ref, vf_ref, l_ref, do_ref, dq_ref, dk_ref, dv_ref, dk_acc, dv_acc, k_ref, v_ref):
        i = pl.program_id(1)

        @pl.when(i == 0)
        def _():
            dk_acc[...] = jnp.zeros_like(dk_acc)
            dv_acc[...] = jnp.zeros_like(dv_acc)
            k_ref[...] = kf_ref[...].astype(BF16)
            v_ref[...] = vf_ref[...].astype(BF16)

        q = q_ref[...].astype(BF16)
        dob = do_ref[...].astype(BF16)
        saved = l_ref[...]
        lane = _iota((bq, HD), 1)
        U = (_iota((bk, bk), 0) > _iota((bk, bk), 1)).astype(BF16)
        UU = jnp.concatenate([U, U], axis=0)
        Ue = (_iota((bk, bk), 0) < _iota((bk, bk), 1)).astype(BF16)
        UUe = jnp.concatenate([Ue, Ue], axis=0)

        def blk(j, carry, masked):
            dq, accP = carry
            accL = jnp.sum(jnp.where(lane == j, saved, 0.0), axis=1, keepdims=True)
            r, kj, vj, mask, _, lb, a = _sb_block(i, j, masked, q, k_ref, v_ref, accL, bq, bk, UU)
            p = a * _dot(dob, vj, NT)
            hi = p.astype(BF16)
            lo = (p - hi.astype(F32)).astype(BF16)
            pre = _dot(jnp.concatenate([hi, lo], axis=1), UUe) + accP
            sig = jnp.exp(lb)
            dz = p * (1.0 - sig) - pre * sig
            if masked:
                dz = jnp.where(mask, dz, 0.0)
            dzb = (dz * (HD ** -0.5)).astype(BF16)
            dq = dq + _dot(dzb, kj)
            dk_acc[r, :] += _dot(dzb, q, TN)
            dv_acc[r, :] += _dot(a.astype(BF16), dob, TN)
            return dq, accP + jnp.sum(p, axis=1, keepdims=True)

        carry = (jnp.zeros((bq, HD), F32), jnp.zeros((bq, 1), F32))
        def group(it, c):
            for jj in range(nd):
                c = blk(it * nd + jj, c, False)
            return c

        carry = lax.fori_loop(0, i, group, carry)
        for jj in range(nd):
            carry = blk(i * nd + jj, carry, True)
        dq_ref[...] = carry[0].astype(BF16)

        @pl.when(i == nq - 1)
        def _():
            dk_ref[...] = dk_acc[...].astype(BF16)
            dv_ref[...] = dv_acc[...].astype(BF16)

    W = H * HD
    blk_q = pl.BlockSpec((bq, HD), lambda h, i: (i, h))
    blk_t = pl.BlockSpec((T, HD), lambda h, i: (0, h))
    return _pcall(
        body, grid=(H, nq),
        in_specs=[blk_q, pl.BlockSpec((T, HD), lambda h, i: (0, H + h)),
                  pl.BlockSpec((T, HD), lambda h, i: (0, 2 * H + h)), blk_q, blk_q],
        out_specs=[blk_q, blk_t, blk_t],
        out_shape=[jax.ShapeDtypeStruct((T, W), BF16)] * 3,
        scratch_shapes=[pltpu.VMEM((T, HD), F32), pltpu.VMEM((T, HD), F32),
                        pltpu.VMEM((T, HD), BF16), pltpu.VMEM((T, HD), BF16)],
        compiler_params=_cparams(("parallel", "arbitrary")), name="sb_bwd")(proj, proj, proj, l_sb, do_sb)


def _gdn_chunk(q, k, v, bb, gb, S):
    G, C = q.shape[0], q.shape[1]
    ri, ci = _iota((G, C, C), 1), _iota((G, C, C), 2)
    tril, strict = ri >= ci, ri > ci
    e0 = (_iota((G, C, HD), 2) == 0).astype(F32)
    g_col = _dotf(gb, e0, NTB)
    g_row = _dotf(e0, gb, NTB)
    decay = jnp.where(tril, jnp.exp(jnp.where(tril, g_col - g_row, 0.0)), 0.0)
    eg = jnp.exp(gb)
    qs = q * (HD ** -0.5)
    kb = k * bb
    lw = jnp.where(strict, _bdot_nt(kb, k) * decay, 0.0)
    x = (ri == ci).astype(F32) - lw
    pw = lw
    for _ in range(C.bit_length() - 2):
        pw = _dotf(pw, pw, NNB)
        x = x + _dotf(x, pw, NNB)
    u = _bdot_nn(x, v * bb)
    w = _bdot_nn(x, kb * eg)
    aq = jnp.where(tril, _bdot_nt(qs, k) * decay, 0.0)
    vnew = u - _bdot_nn(w, S)
    o = _bdot_nn(qs * eg, S) + _bdot_nn(aq, vnew)
    g_last = _dotf((ci == C - 1).astype(F32), gb, NNB)
    g_last_s = _dotf((_iota((G, HD, C), 2) == C - 1).astype(F32), gb, NNB)
    s_new = S * jnp.exp(g_last_s) + _bdot_tn(k * jnp.exp(g_last - gb), vnew)
    return o, s_new


def _gdn_fwd(qn, kn, vc, beta_b, g_b, T, H):
    tb = min(GDN_TB, T)
    nc = tb // CH
    G = min(GDN_G, H)

    def body(q_ref, k_ref, v_ref, b_ref, g_ref, o_ref, s_ref, s_scr):
        @pl.when(pl.program_id(1) == 0)
        def _():
            s_scr[...] = jnp.zeros_like(s_scr)

        def step(c, carry):
            r = pl.ds(pl.multiple_of(c * CH, CH), CH)
            s = s_scr[...]
            s_ref[:, c] = s
            o, s2 = _gdn_chunk(q_ref[:, r, :], k_ref[:, r, :], v_ref[:, r, :], b_ref[:, r, :], g_ref[:, r, :], s)
            o_ref[:, r, :] = o
            s_scr[...] = s2
            return carry

        lax.fori_loop(0, nc, step, 0)

    blk3 = pl.BlockSpec((G, tb, HD), lambda h, t: (h, t, 0))
    return _pcall(
        body, grid=(H // G, T // tb), in_specs=[blk3] * 5,
        out_specs=[blk3, pl.BlockSpec((G, nc, HD, HD), lambda h, t: (h, t, 0, 0))],
        out_shape=[jax.ShapeDtypeStruct((H, T, HD), F32), jax.ShapeDtypeStruct((H, T // CH, HD, HD), F32)],
        scratch_shapes=[pltpu.VMEM((G, HD, HD), F32)],
        compiler_params=_cparams(("parallel", "arbitrary")), name="gdn_fwd")(qn, kn, vc, beta_b, g_b)


def _gdn_bwd(qn, kn, vc, beta_b, g_b, s_all, do, T, H):
    tb = min(GDN_TB, T)
    nc = tb // CH
    nt = T // tb
    G = min(GDN_G, H)

    def body(q_ref, k_ref, v_ref, b_ref, g_ref, s_ref, do_ref, dq_ref, dk_ref, dv_ref, db_ref, dg_ref, ds_scr):
        @pl.when(pl.program_id(1) == 0)
        def _():
            ds_scr[...] = jnp.zeros_like(ds_scr)

        def step(it, carry):
            c = nc - 1 - it
            r = pl.ds(pl.multiple_of(c * CH, CH), CH)
            _, vjp = jax.vjp(_gdn_chunk, q_ref[:, r, :], k_ref[:, r, :], v_ref[:, r, :], b_ref[:, r, :],
                             g_ref[:, r, :], s_ref[:, c])
            dq, dk, dv, db, dg, ds = vjp((do_ref[:, r, :], ds_scr[...]))
            dq_ref[:, r, :] = dq
            dk_ref[:, r, :] = dk
            dv_ref[:, r, :] = dv
            db_ref[:, r, :] = db
            dg_ref[:, r, :] = dg
            ds_scr[...] = ds
            return carry

        lax.fori_loop(0, nc, step, 0)

    blk3 = pl.BlockSpec((G, tb, HD), lambda h, t: (h, nt - 1 - t, 0))
    return _pcall(
        body, grid=(H // G, nt),
        in_specs=[blk3] * 5 + [pl.BlockSpec((G, nc, HD, HD), lambda h, t: (h, nt - 1 - t, 0, 0)), blk3],
        out_specs=[blk3] * 5,
        out_shape=[jax.ShapeDtypeStruct((H, T, HD), F32)] * 5,
        scratch_shapes=[pltpu.VMEM((G, HD, HD), F32)],
        compiler_params=_cparams(("parallel", "arbitrary")), name="gdn_bwd")(qn, kn, vc, beta_b, g_b, s_all, do)


def _conv_tiles(T, C):
    return min(HEAD_TM, T), min(CONV_CW, C)


def _shift_down(main, halo, first, d):
    halo = jnp.where(first, 0.0, halo)
    ext = jnp.concatenate([halo, main], axis=0)
    return pltpu.roll(ext, d, 0)[8:]


def _conv_fwd(proj, conv_w, T, H):
    W = H * HD
    C = 3 * W
    tm, cw = _conv_tiles(T, C)
    col0 = 4 * W // cw

    def body(x_ref, h_ref, w_ref, o_ref):
        first = pl.program_id(1) == 0
        main = x_ref[...]
        halo = h_ref[...]
        w = w_ref[...]
        out = w[CONV_K - 1:CONV_K, :] * main
        for d in range(1, CONV_K):
            out = out + w[CONV_K - 1 - d:CONV_K - d, :] * _shift_down(main, halo, first, d)
        o_ref[...] = out

    return _pcall(
        body, grid=(C // cw, T // tm),
        in_specs=[pl.BlockSpec((tm, cw), lambda cb, i: (i, col0 + cb)),
                  pl.BlockSpec((8, cw), lambda cb, i: (jnp.maximum(i * (tm // 8) - 1, 0), col0 + cb)),
                  pl.BlockSpec((CONV_K, cw), lambda cb, i: (0, cb))],
        out_specs=pl.BlockSpec((tm, cw), lambda cb, i: (i, cb)),
        out_shape=jax.ShapeDtypeStruct((T, C), F32),
        compiler_params=_cparams(("parallel", "arbitrary")), name="conv_fwd")(proj, proj, conv_w)


def _conv_bwd(proj, conv_w, dconv, T, H):
    W = H * HD
    C = 3 * W
    tm, cw = _conv_tiles(T, C)
    col0 = 4 * W // cw
    nt = T // tm

    def body(x_ref, h_ref, w_ref, d_ref, dn_ref, dx_ref, dw_ref):
        i = pl.program_id(1)
        first = i == 0
        main = x_ref[...]
        halo = h_ref[...]
        w = w_ref[...]
        dmain = d_ref[...]
        dnext = jnp.where(i == nt - 1, 0.0, dn_ref[...])
        dext = jnp.concatenate([dmain, dnext], axis=0)
        dx = w[CONV_K - 1:CONV_K, :] * dmain
        rows = [jnp.sum(dmain * main, axis=0, keepdims=True)]
        for d in range(1, CONV_K):
            dx = dx + w[CONV_K - 1 - d:CONV_K - d, :] * pltpu.roll(dext, tm + 8 - d, 0)[:tm]
            rows.append(jnp.sum(dmain * _shift_down(main, halo, first, d), axis=0, keepdims=True))
        dx_ref[...] = dx.astype(BF16)

        @pl.when(first)
        def _():
            dw_ref[...] = jnp.zeros_like(dw_ref)

        for d in range(CONV_K):
            dw_ref[CONV_K - 1 - d:CONV_K - d, :] += rows[d]

    return _pcall(
        body, grid=(C // cw, nt),
        in_specs=[pl.BlockSpec((tm, cw), lambda cb, i: (i, col0 + cb)),
                  pl.BlockSpec((8, cw), lambda cb, i: (jnp.maximum(i * (tm // 8) - 1, 0), col0 + cb)),
                  pl.BlockSpec((CONV_K, cw), lambda cb, i: (0, cb)),
                  pl.BlockSpec((tm, cw), lambda cb, i: (i, cb)),
                  pl.BlockSpec((8, cw), lambda cb, i: (jnp.minimum((i + 1) * (tm // 8), T // 8 - 1), cb))],
        out_specs=[pl.BlockSpec((tm, cw), lambda cb, i: (i, cb)), pl.BlockSpec((CONV_K, cw), lambda cb, i: (0, cb))],
        out_shape=[jax.ShapeDtypeStruct((T, C), BF16), jax.ShapeDtypeStruct((CONV_K, C), F32)],
        compiler_params=_cparams(("parallel", "arbitrary")), name="conv_bwd")(proj, proj, conv_w, dconv, dconv)


def _prep_fn(H, tm):
    def fn(cq, ck, cv, small, al, dtb):
        h = pl.program_id(1)
        lane = _iota((HD, HD), 0)
        oh_b = (lane == h).astype(F32)
        oh_a = (lane == H + h).astype(F32)
        q, k, v = _silu(cq), _silu(ck), _silu(cv)
        qn = q * lax.rsqrt(jnp.sum(q * q, axis=-1, keepdims=True) + L2_EPS)
        kn = k * lax.rsqrt(jnp.sum(k * k, axis=-1, keepdims=True) + L2_EPS)
        beta = _dotf(_sigmoid(small), oh_b)
        g = _dotf(-jnp.exp(al) * _softplus(small + dtb), oh_a)
        ri, ci = _iota((tm, tm), 0), _iota((tm, tm), 1)
        sh = CH.bit_length() - 1
        tri = jnp.logical_and(ri >= ci, jnp.right_shift(ri, sh) == jnp.right_shift(ci, sh)).astype(F32)
        return qn, kn, v, beta, _dotf(tri, g)
    return fn


def _prep_specs(T, H):
    tm = min(HEAD_TM, T)
    W = H * HD
    row = lambda s: pl.BlockSpec((tm, HD), lambda i, h, s=s: (i, s * H + h))
    small = pl.BlockSpec((tm, HD), lambda i, h: (i, 0))
    par = pl.BlockSpec((1, HD), lambda i, h: (0, 0))
    blk = pl.BlockSpec((tm, HD), lambda i, h: (i, h))
    blk3 = pl.BlockSpec((None, tm, HD), lambda i, h: (h, i, 0))
    return tm, W, row, small, par, blk, blk3


def _prep_fwd(conv, small, al, dtb, T, H):
    tm, W, row, sm, par, blk, blk3 = _prep_specs(T, H)
    return _ew(_prep_fn(H, tm), (T // tm, H), [conv, conv, conv, small, al, dtb],
               [row(0), row(1), row(2), sm, par, par],
               [((H, T, HD), F32)] * 5, [blk3] * 5, name="gdn_prep")


def _prep_bwd(conv, small, al, dtb, dqn, dkn, dvc, dbeta, dg, T, H):
    tm, W, row, sm, par, blk, blk3 = _prep_specs(T, H)
    f = _prep_fn(H, tm)

    def fn(cq, ck, cv, small_, al_, dtb_, a, b, c, d, e):
        _, vjp = jax.vjp(f, cq, ck, cv, small_, al_, dtb_)
        return vjp((a, b, c, d, e))

    return _ew(fn, (T // tm, H), [conv, conv, conv, small, al, dtb, dqn, dkn, dvc, dbeta, dg],
               [row(0), row(1), row(2), sm, par, par] + [blk3] * 5,
               [((T, W), F32)] * 3 + [((T, HD), F32), ((1, HD), F32), ((1, HD), F32)],
               [blk, blk, blk, sm, par, par], acc=[None, None, None, 'inner', 'all', 'all'], name="gdn_prep_bwd")


def _gate_fn(o_sb, z_sb, o_g, z_g, gnw):
    a_sb = o_sb * _silu(z_sb)
    a_g = _rms(o_g, gnw) * _silu(z_g)
    return a_sb, a_g


def _gate_specs(T, H):
    tm = min(HEAD_TM, T)
    blk = pl.BlockSpec((tm, HD), lambda i, h: (i, h))
    sec = lambda s: pl.BlockSpec((tm, HD), lambda i, h, s=s: (i, s * H + h))
    par = pl.BlockSpec((1, HD), lambda i, h: (0, 0))
    blk3 = pl.BlockSpec((None, tm, HD), lambda i, h: (h, i, 0))
    return tm, blk, blk3, sec, par


def _gate_fwd(o_sb, proj, o_g, gnw, T, H):
    tm, blk, blk3, sec, par = _gate_specs(T, H)
    W = H * HD
    return _ew(_gate_fn, (T // tm, H), [o_sb, proj, o_g, proj, gnw], [blk, sec(3), blk3, sec(7), par],
               [((T, W), BF16)] * 2, [blk, blk], name="gate_fwd")


def _gate_bwd(o_sb, proj, o_g, gnw, da_sb, da_g, T, H):
    tm, blk, blk3, sec, par = _gate_specs(T, H)
    W = H * HD

    def fn(o_sb_, z_sb, o_g_, z_g, gnw_, da, db):
        _, vjp = jax.vjp(_gate_fn, o_sb_, z_sb, o_g_, z_g, gnw_)
        return vjp((da, db))

    return _ew(fn, (T // tm, H), [o_sb, proj, o_g, proj, gnw, da_sb, da_g], [blk, sec(3), blk3, sec(7), par, blk, blk],
               [((T, W), F32), ((T, W), BF16), ((H, T, HD), F32), ((T, W), BF16), ((1, HD), F32)],
               [blk, blk, blk3, blk, par], acc=[None, None, None, None, 'all'], name="gate_bwd")


def _ada_fn(x, nw, scale, shift):
    return _rms(x, nw) * (1.0 + scale) + shift


def _row_specs(T, D):
    tm = min(EW_TM, T)
    return tm, pl.BlockSpec((tm, D), lambda i: (i, 0)), pl.BlockSpec((1, D), lambda i: (0, 0))


def _ada_fwd(x, nw, scale, shift):
    T, D = x.shape
    tm, row, par = _row_specs(T, D)
    return _ew(_ada_fn, (T // tm,), [x, nw, scale, shift], [row, par, par, par], [((T, D), BF16)], [row],
               name="ada_norm")[0]


def _ada_bwd(x, nw, scale, shift, dh, dx2):
    T, D = x.shape
    tm, row, par = _row_specs(T, D)

    def fn(x_, nw_, sc_, sh_, dh_, dx2_):
        _, vjp = jax.vjp(_ada_fn, x_, nw_, sc_, sh_)
        dx, dnw, dsc, dsh = vjp(dh_)
        return dx + dx2_, dnw, dsc, dsh

    return _ew(fn, (T // tm,), [x, nw, scale, shift, dh, dx2], [row, par, par, par, row, row],
               [((T, D), F32)] + [((1, D), F32)] * 3, [row, par, par, par], acc=[None, 'all', 'all', 'all'],
               name="ada_norm_bwd")


def _merge_fn(m_sb, m_g, p_sb, p_g):
    return _sigmoid(m_sb) * p_sb + _sigmoid(m_g) * p_g


def _merge_specs(T, D, H):
    tm, tc = min(HEAD_TM, T), min(512, D)
    nb = D // tc
    blk = pl.BlockSpec((tm, tc), lambda i, j: (i, j))
    sec = lambda s: pl.BlockSpec((tm, tc), lambda i, j, s=s: (i, s * nb + j))
    return tm, tc, blk, sec


def _merge_fwd(proj, p_sb, p_g, T, D, H):
    tm, tc, blk, sec = _merge_specs(T, D, H)
    return _ew(_merge_fn, (T // tm, D // tc), [proj, proj, p_sb, p_g], [sec(8), sec(9), blk, blk],
               [((T, D), BF16)], [blk], name="merge")[0]


def _merge_bwd(proj, p_sb, p_g, dy, T, D, H):
    tm, tc, blk, sec = _merge_specs(T, D, H)

    def fn(m_sb, m_g, p_sb_, p_g_, dy_):
        _, vjp = jax.vjp(_merge_fn, m_sb, m_g, p_sb_, p_g_)
        return vjp(dy_)

    return _ew(fn, (T // tm, D // tc), [proj, proj, p_sb, p_g, dy], [sec(8), sec(9), blk, blk, blk],
               [((T, D), BF16)] * 4, [blk] * 4, name="merge_bwd")


def _loss_head(x, u, gate, fnw, tgt):
    T, D = x.shape
    tm, row, par = _row_specs(T, D)

    def loss(x_, u_, gate_, fnw_, tgt_):
        y = _rms(x_ + gate_ * u_, fnw_)
        return 0.5 * jnp.sum(jnp.mean(jnp.square(y - tgt_), axis=-1))

    def fn(x_, u_, gate_, fnw_, tgt_):
        val, (dx, du, dgate, dfnw) = jax.value_and_grad(loss, argnums=(0, 1, 2, 3))(x_, u_, gate_, fnw_, tgt_)
        return jnp.full((1, HD), val, F32), dx, du, dgate, dfnw

    return _ew(fn, (T // tm,), [x, u, gate, fnw, tgt], [row, row, par, par, row],
               [((1, HD), F32), ((T, D), F32), ((T, D), BF16), ((1, D), F32), ((1, D), F32)],
               [pl.BlockSpec((1, HD), lambda i: (0, 0)), row, row, par, par],
               acc=['all', None, None, 'all', 'all'], name="loss_head")


def _mod_part(c_all, w_ada, b_ada):
    D, N = w_ada.shape
    tn = min(512, N)

    def fn(c, w, b):
        return _dot(_silu(c).astype(BF16), w.astype(BF16)) + b

    return _ew(fn, (N // tn,), [c_all, w_ada, b_ada],
               [pl.BlockSpec((8, D), lambda j: (0, 0)), pl.BlockSpec((D, tn), lambda j: (0, j)),
                pl.BlockSpec((1, tn), lambda j: (0, j))],
               [((8, N), F32)], [pl.BlockSpec((8, tn), lambda j: (0, j))], name="ada_mod")[0]


def _w_ada_grad(c_all, dmod):
    D, N = c_all.shape[1], dmod.shape[1]
    tn = min(512, N)

    def fn(c, dm):
        return _dot(_silu(c).astype(BF16), dm.astype(BF16), TN)

    return _ew(fn, (N // tn,), [c_all, dmod],
               [pl.BlockSpec((8, D), lambda j: (0, 0)), pl.BlockSpec((8, tn), lambda j: (0, j))],
               [((D, N), F32)], [pl.BlockSpec((D, tn), lambda j: (0, j))], name="w_ada_grad")[0]


def _sum8(packs):
    N = packs.shape[1]

    def fn(p):
        s = p[0:8]
        for d in range(1, 8):
            s = s + p[8 * d:8 * d + 8]
        return s

    return _ew(fn, (1,), [packs], [pl.BlockSpec((64, N), lambda i: (0, 0))], [((8, N), F32)],
               [pl.BlockSpec((8, N), lambda i: (0, 0))], name="sum_devices")[0]


def _sum_chips(q, name):
    _, R, C = q.shape
    tr = min(64, R)

    def fn(p):
        p = p.astype(F32)
        return (p[0] + p[1]) + (p[2] + p[3])

    return _ew(fn, (R // tr,), [q], [pl.BlockSpec((4, tr, C), lambda i: (0, i, 0))], [((R, C), F32)],
               [pl.BlockSpec((tr, C), lambda i: (i, 0))], name=name)[0]


def _add_halves(g, rsib, c, name):
    _, R, C = g.shape
    rh = R // 2
    tr = min(64, rh)
    nb = rh // tr

    def body(c_ref, a_ref, b_ref, o_ref):
        o_ref[...] = (a_ref[...] + b_ref[...]).astype(o_ref.dtype)

    spec = pl.BlockSpec((None, tr, C), lambda j, i, c_ref: (j, i, 0))
    grid_spec = pltpu.PrefetchScalarGridSpec(
        num_scalar_prefetch=1, grid=(4, nb),
        in_specs=[pl.BlockSpec((None, tr, C), lambda j, i, c_ref: (j, c_ref[0] * nb + i, 0)), spec], out_specs=spec)
    return _pcall(body, grid_spec=grid_spec, out_shape=jax.ShapeDtypeStruct((4, rh, C), BF16),
                  compiler_params=_cparams(("arbitrary", "arbitrary")), name=name)(
                      jnp.reshape(c, (1,)).astype(jnp.int32), g, rsib)


def _adamw(w, g, m, v, name):
    R, C = w.shape
    tr = R if R <= 64 else 64
    blk = pl.BlockSpec((tr, C), lambda i: (i, 0))

    def fn(w_, g_, m_, v_):
        m2 = ADAM_B1 * m_ + (1.0 - ADAM_B1) * g_
        v2 = ADAM_B2 * v_ + (1.0 - ADAM_B2) * jnp.square(g_)
        m_hat = m2 / (1.0 - ADAM_B1 ** ADAM_STEP)
        v_hat = v2 / (1.0 - ADAM_B2 ** ADAM_STEP)
        delta = -ADAM_LR * (m_hat / (jnp.sqrt(v_hat) + ADAM_EPS) + ADAM_WD * w_)
        return delta, m2, v2

    return _ew(fn, (R // tr,), [w, g, m, v], [blk] * 4, [((R, C), F32)] * 3, [blk] * 3, name=name)


def _place():
    x, y, c = lax.axis_index("x"), lax.axis_index("y"), lax.axis_index("c")
    return x, y, c, [(1 - x, y), (x, 1 - y), (1 - x, 1 - y)]


ANY = pl.BlockSpec(memory_space=pl.ANY)


def _allgather8(blk):
    m_per, n = blk.shape

    def body(x_ref, out_ref, send_sems, recv_sems, local_sem):
        x, y, c, chips = _place()
        me, sibling = (x, y, c), (x, y, 1 - c)

        def rows(px, py, pc):
            return out_ref.at[pl.ds((4 * px + 2 * py + pc) * m_per, m_per), :]

        def copy(k, block, to, src=None):
            return pltpu.make_async_remote_copy(
                src_ref=rows(*block) if src is None else src, dst_ref=rows(*block),
                send_sem=send_sems.at[k], recv_sem=recv_sems.at[k], device_id=to, device_id_type=MESH)

        mine = pltpu.make_async_copy(x_ref, rows(*me), local_sem)
        mine.start()
        first = [copy(0, me, sibling, src=x_ref)]
        first += [copy(1 + j, me, (*chip, c), src=x_ref) for j, chip in enumerate(chips)]
        for cp in first:
            cp.start()
        passed = [copy(4 + j, (*chip, c), sibling) for j, chip in enumerate(chips)]
        for j, chip in enumerate(chips):
            copy(1 + j, (*chip, c), me).wait_recv()
            passed[j].start()
        copy(0, sibling, me).wait_recv()
        for j, chip in enumerate(chips):
            copy(4 + j, (*chip, 1 - c), me).wait_recv()
        for cp in first + passed:
            cp.wait_send()
        mine.wait()

    vm = pl.BlockSpec(memory_space=pltpu.VMEM)
    return _pcall(body, out_shape=jax.ShapeDtypeStruct((8 * m_per, n), blk.dtype), in_specs=[vm], out_specs=vm,
                  scratch_shapes=[pltpu.SemaphoreType.DMA((7,)), pltpu.SemaphoreType.DMA((7,)),
                                  pltpu.SemaphoreType.DMA],
                  compiler_params=pltpu.CompilerParams(vmem_limit_bytes=VMEM_LIMIT), name="allgather8")(blk)


def _gather_weights(shards):
    n = len(shards)

    def body(*refs):
        ins, outs = refs[:n], refs[n:2 * n]
        send_sems, recv_sems, fsend_sems, frecv_sems, local_sems = refs[2 * n:]
        x, y, c, chips = _place()
        mychip = 2 * x + y
        locs, sends, fwds = [], [], []
        for a in range(n):
            rh = shards[a].shape[0] // 2
            loc = pltpu.make_async_copy(ins[a], outs[a].at[mychip], local_sems.at[a])
            loc.start()
            locs.append(loc)
            for k, chip in enumerate(chips):
                cp = pltpu.make_async_remote_copy(
                    src_ref=ins[a].at[pl.ds(c * rh, rh), :], dst_ref=outs[a].at[mychip, pl.ds(c * rh, rh), :],
                    send_sem=send_sems.at[3 * a + k], recv_sem=recv_sems.at[3 * a + k],
                    device_id=(*chip, c), device_id_type=MESH)
                cp.start()
                sends.append(cp)
        for a in range(n):
            rh = shards[a].shape[0] // 2
            for k, (px, py) in enumerate(chips):
                land = outs[a].at[2 * px + py, pl.ds(c * rh, rh), :]
                pltpu.make_async_remote_copy(
                    src_ref=land, dst_ref=land, send_sem=send_sems.at[3 * a + k], recv_sem=recv_sems.at[3 * a + k],
                    device_id=(px, py, c), device_id_type=MESH).wait_recv()
                fw = pltpu.make_async_remote_copy(
                    src_ref=land, dst_ref=land, send_sem=fsend_sems.at[3 * a + k], recv_sem=frecv_sems.at[3 * a + k],
                    device_id=(x, y, 1 - c), device_id_type=MESH)
                fw.start()
                fwds.append(fw)
        for a in range(n):
            rh = shards[a].shape[0] // 2
            for k, (px, py) in enumerate(chips):
                land = outs[a].at[2 * px + py, pl.ds((1 - c) * rh, rh), :]
                pltpu.make_async_remote_copy(
                    src_ref=land, dst_ref=land, send_sem=fsend_sems.at[3 * a + k], recv_sem=frecv_sems.at[3 * a + k],
                    device_id=(x, y, 1 - c), device_id_type=MESH).wait_recv()
        for cp in sends + fwds:
            cp.wait_send()
        for loc in locs:
            loc.wait()

    return _pcall(body, out_shape=[jax.ShapeDtypeStruct((4,) + s.shape, s.dtype) for s in shards],
                  in_specs=[ANY] * n, out_specs=[ANY] * n,
                  scratch_shapes=[pltpu.SemaphoreType.DMA((3 * n,))] * 4 + [pltpu.SemaphoreType.DMA((n,))],
                  name="gather_weights")(*shards)


def _row_chunks(rows, n):
    while n > 1 and rows % (8 * n):
        n //= 2
    return [(k * (rows // n), rows // n) for k in range(n)]


def _swap_partial_halves(gs):
    n = len(gs)
    plan = [(a, j, r0, nr) for a in range(n) for j in range(4)
            for r0, nr in _row_chunks(gs[a].shape[1] // 2, SWAP_CHUNKS)]

    def body(*refs):
        ins, outs = refs[:n], refs[n:2 * n]
        send_sems, recv_sems = refs[2 * n:]
        x, y, c, _ = _place()
        cps = []
        for k, (a, j, r0, nr) in enumerate(plan):
            rh = gs[a].shape[1] // 2
            cp = pltpu.make_async_remote_copy(
                src_ref=ins[a].at[j, pl.ds((1 - c) * rh + r0, nr), :], dst_ref=outs[a].at[j, pl.ds(r0, nr), :],
                send_sem=send_sems.at[k], recv_sem=recv_sems.at[k], device_id=(x, y, 1 - c), device_id_type=MESH)
            cp.start()
            cps.append(cp)
        for cp in cps:
            cp.wait()

    half = [jax.ShapeDtypeStruct((4, g.shape[1] // 2, g.shape[2]), g.dtype) for g in gs]
    return _pcall(body, out_shape=half, in_specs=[ANY] * n, out_specs=[ANY] * n,
                  scratch_shapes=[pltpu.SemaphoreType.DMA((len(plan),))] * 2, name="swap_partial_halves")(*gs)


def _chip_scatter(ps):
    n = len(ps)

    def body(*refs):
        ins, outs = refs[:n], refs[n:2 * n]
        send_sems, recv_sems, local_sems = refs[2 * n:]
        x, y, c, chips = _place()
        mychip = 2 * x + y
        cps = []
        for a in range(n):
            loc = pltpu.make_async_copy(ins[a].at[mychip], outs[a].at[mychip], local_sems.at[a])
            loc.start()
            cps.append(loc)
            for k, (px, py) in enumerate(chips):
                cp = pltpu.make_async_remote_copy(
                    src_ref=ins[a].at[2 * px + py], dst_ref=outs[a].at[mychip],
                    send_sem=send_sems.at[3 * a + k], recv_sem=recv_sems.at[3 * a + k],
                    device_id=(px, py, c), device_id_type=MESH)
                cp.start()
                cps.append(cp)
        for a in range(n):
            for k, (px, py) in enumerate(chips):
                land = outs[a].at[2 * px + py]
                pltpu.make_async_remote_copy(
                    src_ref=land, dst_ref=land, send_sem=send_sems.at[3 * a + k], recv_sem=recv_sems.at[3 * a + k],
                    device_id=(px, py, c), device_id_type=MESH).wait_recv()
        for a in range(n):
            cps[4 * a].wait()
            for k in range(3):
                cps[4 * a + 1 + k].wait_send()

    return _pcall(body, out_shape=[jax.ShapeDtypeStruct(p.shape, p.dtype) for p in ps],
                  in_specs=[ANY] * n, out_specs=[ANY] * n,
                  scratch_shapes=[pltpu.SemaphoreType.DMA((3 * n,))] * 2 + [pltpu.SemaphoreType.DMA((n,))],
                  name="chip_scatter")(*ps)


def _swap_final_halves(hs):
    n = len(hs)
    plan = [(a, r0, nr) for a in range(n) for r0, nr in _row_chunks(hs[a].shape[0], 2 * SWAP_CHUNKS)]

    def body(*refs):
        ins, outs = refs[:n], refs[n:2 * n]
        send_sems, recv_sems, local_sems = refs[2 * n:]
        x, y, c, _ = _place()
        cps = []
        for k, (a, r0, nr) in enumerate(plan):
            rows = pl.ds(r0, nr)
            loc = pltpu.make_async_copy(ins[a].at[rows, :], outs[a].at[c, rows, :], local_sems.at[k])
            loc.start()
            cp = pltpu.make_async_remote_copy(
                src_ref=ins[a].at[rows, :], dst_ref=outs[a].at[c, rows, :], send_sem=send_sems.at[k],
                recv_sem=recv_sems.at[k], device_id=(x, y, 1 - c), device_id_type=MESH)
            cp.start()
            cps.append((loc, cp))
        for k, (a, r0, nr) in enumerate(plan):
            land = outs[a].at[1 - c, pl.ds(r0, nr), :]
            pltpu.make_async_remote_copy(
                src_ref=land, dst_ref=land, send_sem=send_sems.at[k], recv_sem=recv_sems.at[k],
                device_id=(x, y, 1 - c), device_id_type=MESH).wait_recv()
        for loc, cp in cps:
            loc.wait()
            cp.wait_send()

    return _pcall(body, out_shape=[jax.ShapeDtypeStruct((2,) + h.shape, h.dtype) for h in hs],
                  in_specs=[ANY] * n, out_specs=[ANY] * n,
                  scratch_shapes=[pltpu.SemaphoreType.DMA((len(plan),))] * 3, name="swap_final_halves")(*hs)


def _pad_cols(a, n):
    return jnp.pad(a, ((0, 0), (0, n - a.shape[1])))


def kernel(x, c, w_ada, b_ada, norm_w, w_in, gdn_conv_w, gdn_a_log, gdn_dt_bias, gdn_norm_w, w_proj_sb, w_proj_gdn, w_out, final_norm_w, loss_target, m_w_ada, m_b_ada, m_norm_w, m_w_in, m_gdn_conv_w, m_gdn_a_log, m_gdn_dt_bias, m_gdn_norm_w, m_w_proj_sb, m_w_proj_gdn, m_w_out, m_final_norm_w, v_w_ada, v_b_ada, v_norm_w, v_w_in, v_gdn_conv_w, v_gdn_a_log, v_gdn_dt_bias, v_gdn_norm_w, v_w_proj_sb, v_w_proj_gdn, v_w_out, v_final_norm_w):
    T, D = x.shape[1], x.shape[2]
    H = gdn_a_log.shape[1]
    W = H * HD
    assert W == D and T % CH == 0
    NA = w_ada.shape[2]
    NI = w_in.shape[2]
    CW = gdn_conv_w.shape[2]
    px, py, pc = lax.axis_index("x"), lax.axis_index("y"), lax.axis_index("c")
    chip = 2 * px + py
    me = 2 * chip + pc
    x2d, tgt = x[0], loss_target[0]
    PW = 3 * D

    pack1 = jnp.concatenate([c, _pad_cols(gdn_conv_w[0], D), jnp.zeros((3, D), F32)], axis=0)
    got1 = _allgather8(pack1).reshape(8, 8, D)
    c_all = got1[:, 0, :]
    conv_w = jnp.concatenate([got1[2 * j, 1:1 + CONV_K, :CW] for j in range(4)], axis=1)

    b_shard = lax.dynamic_slice_in_dim(b_ada, chip * NA, NA, axis=1)
    mod_part = _mod_part(c_all, w_ada[0], b_shard)
    got2 = _allgather8(mod_part).reshape(8, 8, NA)
    mod = jnp.concatenate([lax.dynamic_index_in_dim(got2[2 * j], me, 0) for j in range(4)], axis=1)
    shift, scale, gate = mod[:, :D], mod[:, D:2 * D], mod[:, 2 * D:]

    wg = _gather_weights([w_in[0].astype(BF16), w_proj_sb[0].astype(BF16), w_proj_gdn[0].astype(BF16),
                          w_out[0].astype(BF16)])
    w_in_full = jnp.concatenate([wg[0][j] for j in range(4)], axis=1)
    w_big = jnp.concatenate([w_in_full[:, :8 * W], w_in_full[:, 8 * W + 2 * H:]], axis=1)
    w_small = _pad_cols(w_in_full[:, 8 * W:8 * W + 2 * H], HD)
    w_psb, w_pg, w_o = (wg[i].reshape(D, D) for i in (1, 2, 3))

    h = _ada_fwd(x2d, norm_w, scale, shift)
    proj = _mm1(h, w_big, 'nn', F32, "proj_big")
    small = _mm1(h, w_small, 'nn', F32, "proj_small")
    o_sb, l_sb = _sb_fwd(proj, T, H)
    conv = _conv_fwd(proj, conv_w, T, H)
    al = jnp.pad(gdn_a_log, ((0, 0), (H, HD - 2 * H)))
    dtb = jnp.pad(gdn_dt_bias, ((0, 0), (H, HD - 2 * H)))
    qn, kn, vc, beta_b, g_b = _prep_fwd(conv, small, al, dtb, T, H)
    o_g, s_all = _gdn_fwd(qn, kn, vc, beta_b, g_b, T, H)
    a_sb, a_g = _gate_fwd(o_sb, proj, o_g, gdn_norm_w, T, H)
    p_sb = _mm1(a_sb, w_psb, 'nn', F32, "proj_sb")
    p_g = _mm1(a_g, w_pg, 'nn', F32, "proj_gdn")
    y = _merge_fwd(proj, p_sb, p_g, T, D, H)
    u = _mm1(y, w_o, 'nn', F32, "proj_out")
    loss_p, dx2, du, dgate, g_fnw = _loss_head(x2d, u, gate, final_norm_w.reshape(1, D), tgt)

    dy = _mm1(du, w_o, 'nt', F32, "d_merge")
    g_w_out = _mm1(y, du, 'tn', F32, "g_w_out")
    dm_sb, dm_g, dp_sb, dp_g = _merge_bwd(proj, p_sb, p_g, dy, T, D, H)
    da_sb = _mm1(dp_sb, w_psb, 'nt', F32, "d_a_sb")
    g_w_psb = _mm1(a_sb, dp_sb, 'tn', F32, "g_w_proj_sb")
    da_g = _mm1(dp_g, w_pg, 'nt', F32, "d_a_gdn")
    g_w_pg = _mm1(a_g, dp_g, 'tn', F32, "g_w_proj_gdn")
    do_sb, dz_sb, do_g, dz_g, g_gnw = _gate_bwd(o_sb, proj, o_g, gdn_norm_w, da_sb, da_g, T, H)
    dq_sb, dk_sb, dv_sb = _sb_bwd(proj, l_sb, do_sb, T, H)
    dqn, dkn, dvc, dbeta_b, dg_b = _gdn_bwd(qn, kn, vc, beta_b, g_b, s_all, do_g, T, H)
    dcq, dck, dcv, dsmall, dal, ddtb = _prep_bwd(conv, small, al, dtb, dqn, dkn, dvc, dbeta_b, dg_b, T, H)
    dpre, g_conv = _conv_bwd(proj, conv_w, jnp.concatenate([dcq, dck, dcv], axis=1), T, H)

    secs = [(dq_sb, 0), (dk_sb, W), (dv_sb, 2 * W), (dz_sb, 3 * W), (dpre, 4 * W), (dz_g, 7 * W),
            (dm_sb, 8 * W), (dm_g, 9 * W)]
    dh = _mm([(a, w_big, 0, off, a.shape[1]) for a, off in secs] + [(dsmall, w_small, 0, 0, HD)],
             'nt', F32, "d_h", tk=256)
    g_secs = [_mm1(h, a, 'tn', F32, "g_w_in_%d" % i) for i, (a, _) in enumerate(secs)]
    g_small = _mm1(h, dsmall, 'tn', F32, "g_w_in_small")
    grad_x, g_nw, dscale, dshift = _ada_bwd(x2d, norm_w, scale, shift, dh, dx2)

    dmod = jnp.concatenate([dshift, dscale, dgate], axis=1)
    misc = jnp.concatenate([g_nw, g_fnw, g_gnw, dal[:, H:2 * H], ddtb[:, H:2 * H], loss_p[:, :1]], axis=1)
    pack3 = jnp.concatenate([dmod, g_conv, _pad_cols(misc, PW), jnp.zeros((2, PW), F32)], axis=0)
    got3 = _allgather8(pack3)
    tot = _sum8(got3)
    dmod_all = got3.reshape(8, 8, PW)[:, 0, :]
    g_w_ada = _w_ada_grad(c_all, lax.dynamic_slice_in_dim(dmod_all, chip * NA, NA, axis=1))
    g_conv_sh = lax.dynamic_slice_in_dim(tot[1:1 + CONV_K], chip * CW, CW, axis=1)
    loss = tot[5, 2 * D + HD + 2 * H]

    g_in = jnp.concatenate(g_secs[:6] + [g_small[:, :2 * H]] + g_secs[6:], axis=1)
    g_full = [jnp.stack([g_in[:, j * NI:(j + 1) * NI] for j in range(4)]),
              g_w_psb.reshape(4, D // 4, D), g_w_pg.reshape(4, D // 4, D), g_w_out.reshape(4, D // 4, D)]
    g_sib = _swap_partial_halves(g_full)
    parts = [_add_halves(g, r, pc, "add_halves_%d" % i) for i, (g, r) in enumerate(zip(g_full, g_sib))]
    got = _chip_scatter(parts)
    halves = [_sum_chips(q, "sum_chips_%d" % i) for i, q in enumerate(got)]
    g_red = [f.reshape(f.shape[1] * 2, f.shape[2]) for f in _swap_final_halves(halves)]

    out = {}

    def upd(name, w, g, m, v, shape):
        d_, m_, v_ = _adamw(w, g, m, v, "adamw_" + name)
        out[name] = (g.reshape(shape), d_.reshape(shape), m_.reshape(shape), v_.reshape(shape))

    upd("w_ada", w_ada[0], g_w_ada, m_w_ada[0], v_w_ada[0], w_ada.shape)
    upd("w_in", w_in[0], g_red[0], m_w_in[0], v_w_in[0], w_in.shape)
    upd("gdn_conv_w", gdn_conv_w[0], g_conv_sh, m_gdn_conv_w[0], v_gdn_conv_w[0], gdn_conv_w.shape)
    upd("w_proj_sb", w_proj_sb[0], g_red[1], m_w_proj_sb[0], v_w_proj_sb[0], w_proj_sb.shape)
    upd("w_proj_gdn", w_proj_gdn[0], g_red[2], m_w_proj_gdn[0], v_w_proj_gdn[0], w_proj_gdn.shape)
    upd("w_out", w_out[0], g_red[3], m_w_out[0], v_w_out[0], w_out.shape)

    def packs(b, nw, fnw, gnw, a, dt):
        row = jnp.concatenate([nw, fnw.reshape(1, D), gnw, a, dt], axis=1)
        return jnp.concatenate([b, _pad_cols(row, PW), jnp.zeros((6, PW), F32)], axis=0)

    g_pack = jnp.concatenate([tot[0:1], tot[5:6], jnp.zeros((6, PW), F32)], axis=0)
    d_, m_, v_ = _adamw(packs(b_ada, norm_w, final_norm_w, gdn_norm_w, gdn_a_log, gdn_dt_bias), g_pack,
                        packs(m_b_ada, m_norm_w, m_final_norm_w, m_gdn_norm_w, m_gdn_a_log, m_gdn_dt_bias),
                        packs(v_b_ada, v_norm_w, v_final_norm_w, v_gdn_norm_w, v_gdn_a_log, v_gdn_dt_bias),
                        "adamw_small")
    offs = {"norm_w": (0, D, (1, D)), "final_norm_w": (D, D, (D,)), "gdn_norm_w": (2 * D, HD, (1, HD)),
            "gdn_a_log": (2 * D + HD, H, (1, H)), "gdn_dt_bias": (2 * D + HD + H, H, (1, H))}
    out["b_ada"] = tuple(a[0:1] for a in (g_pack, d_, m_, v_))
    for name, (o, n_, shp) in offs.items():
        out[name] = tuple(a[1, o:o + n_].reshape(shp) for a in (g_pack, d_, m_, v_))

    names = ['w_ada', 'b_ada', 'norm_w', 'w_in', 'gdn_conv_w', 'gdn_a_log', 'gdn_dt_bias', 'gdn_norm_w',
             'w_proj_sb', 'w_proj_gdn', 'w_out', 'final_norm_w']
    return (loss, grad_x.reshape(x.shape), *[out[n][0] for n in names], *[out[n][1] for n in names],
            *[out[n][2] for n in names], *[out[n][3] for n in names])
```

```python
import functools

import jax
import jax.numpy as jnp
from jax import lax
from jax.experimental import pallas as pl
from jax.experimental.pallas import tpu as pltpu

F32 = jnp.float32
BF16 = jnp.bfloat16
HD = 128
CH = 64
CONV_K = 4
NORM_EPS = 1e-6
L2_EPS = 1e-6
ADAM_LR, ADAM_B1, ADAM_B2, ADAM_EPS, ADAM_WD, ADAM_STEP = 0.001, 0.9, 0.999, 1e-08, 0.01, 10

VMEM_LIMIT = 56 * 1024 * 1024
SB_BQ, SB_BK = 512, 256
GDN_TB = 256
GDN_G = 8
EW_TM = 256
HEAD_TM = 512
CONV_CW = 512
MM_TM, MM_TN, MM_TK = 1024, 1024, 512
SWAP_CHUNKS = 4
MESH = pl.DeviceIdType.MESH

NN = (((1,), (0,)), ((), ()))
NT = (((1,), (1,)), ((), ()))
TN = (((0,), (0,)), ((), ()))


def _pcall(body, **kw):
    return pl.pallas_call(body, **kw)


def _cparams(sem=None):
    return pltpu.CompilerParams(dimension_semantics=sem, vmem_limit_bytes=VMEM_LIMIT)


def _dot(a, b, dims=NN):
    return lax.dot_general(a, b, dims, preferred_element_type=F32)


def _dotf(a, b, dims=NN):
    return lax.dot_general(a, b, dims, precision=lax.Precision.HIGHEST, preferred_element_type=F32)


def _bdot_make(dims, da_rule, db_rule):
    @jax.custom_vjp
    def f(a, b):
        return _dot(a.astype(BF16), b.astype(BF16), dims)

    def fwd(a, b):
        return f(a, b), (a, b)

    def bwd(res, g):
        a, b = res
        return da_rule(g, a, b), db_rule(g, a, b)

    f.defvjp(fwd, bwd)
    return f


def _rdot(a, b, dims):
    return _dot(a.astype(BF16), b.astype(BF16), dims)


NNB =(((2,), (1,)), ((0,), (0,)))
NTB = (((2,), (2,)), ((0,), (0,)))
TNB = (((1,), (1,)), ((0,), (0,)))
_bdot_nn = _bdot_make(NNB, lambda g, a, b: _rdot(g, b, NTB), lambda g, a, b: _rdot(a, g, TNB))
_bdot_nt = _bdot_make(NTB, lambda g, a, b: _rdot(g, b, NNB), lambda g, a, b: _rdot(g, a, TNB))
_bdot_tn = _bdot_make(TNB, lambda g, a, b: _rdot(b, g, NTB), lambda g, a, b: _rdot(a, g, NNB))


def _iota(shape, axis):
    return lax.broadcasted_iota(jnp.int32, shape, axis)


def _sigmoid(x):
    e = jnp.exp(-jnp.abs(x))
    return jnp.where(x >= 0, 1.0 / (1.0 + e), e / (1.0 + e))


def _silu(x):
    return x * _sigmoid(x)


def _softplus(x):
    return jnp.maximum(x, 0.0) + jnp.log(1.0 + jnp.exp(-jnp.abs(x)))


def _rms(x, w):
    return x * lax.rsqrt(jnp.mean(x * x, axis=-1, keepdims=True) + NORM_EPS) * w


def _ew(fn, grid, ins, in_specs, outs, out_specs, acc=None, name=None):
    n_in = len(ins)
    acc = acc or [None] * len(outs)

    def body(*refs):
        vals = fn(*[r[...] for r in refs[:n_in]])
        if not isinstance(vals, (tuple, list)):
            vals = (vals,)
        for r, v, a in zip(refs[n_in:], vals, acc):
            if a is None:
                r[...] = v.astype(r.dtype)
                continue
            first = pl.program_id(len(grid) - 1) == 0
            if a == 'all':
                for ax in range(len(grid) - 1):
                    first = jnp.logical_and(first, pl.program_id(ax) == 0)

            @pl.when(first)
            def _():
                r[...] = v.astype(r.dtype)

            @pl.when(jnp.logical_not(first))
            def _():
                r[...] += v.astype(r.dtype)

    res = _pcall(body, grid=grid, in_specs=in_specs, out_specs=out_specs,
                 out_shape=[jax.ShapeDtypeStruct(s, d) for s, d in outs],
                 compiler_params=_cparams(("arbitrary",) * len(grid)), name=name)(*ins)
    return res


def _mm(pairs, mode, out_dtype, name, add=None, tm=None, tn=None, tk=None):
    a0, b0 = pairs[0][0], pairs[0][1]
    M = a0.shape[1] if mode == 'tn' else a0.shape[0]
    N = b0.shape[0] if mode == 'nt' else b0.shape[1]
    tm = min(tm or MM_TM, M)
    tn = min(tn or MM_TN, N)
    tk = tk or MM_TK
    tks = [min(tk, p[4]) for p in pairs]
    nks = [p[4] // t for p, t in zip(pairs, tks)]
    offs = [sum(nks[:i]) for i in range(len(pairs))]
    total = sum(nks)
    assert M % tm == 0 and N % tn == 0 and all(p[4] % t == 0 for p, t in zip(pairs, tks)), (name, M, N)
    dims = {'nn': NN, 'nt': NT, 'tn': TN}[mode]

    in_specs, ins = [], []
    for (a, b, ao, bo, kl), t, nk, off in zip(pairs, tks, nks, offs):
        def kidx(kk, nk=nk, off=off):
            return jnp.minimum(jnp.maximum(kk - off, 0), nk - 1)
        if mode == 'tn':
            in_specs.append(pl.BlockSpec((t, tm), lambda i, j, kk, f=kidx, o=ao // t: (o + f(kk), i)))
        else:
            in_specs.append(pl.BlockSpec((tm, t), lambda i, j, kk, f=kidx, o=ao // t: (i, o + f(kk))))
        if mode == 'nt':
            in_specs.append(pl.BlockSpec((tn, t), lambda i, j, kk, f=kidx, o=bo // t: (j, o + f(kk))))
        else:
            in_specs.append(pl.BlockSpec((t, tn), lambda i, j, kk, f=kidx, o=bo // t: (o + f(kk), j)))
        ins += [a, b]
    if add is not None:
        in_specs.append(pl.BlockSpec((tm, tn), lambda i, j, kk: (i, j)))
        ins.append(add)
    npairs = len(pairs)

    def body(*refs):
        out_ref, acc_ref = refs[-2], refs[-1]
        kk = pl.program_id(2)

        @pl.when(kk == 0)
        def _():
            acc_ref[...] = jnp.zeros_like(acc_ref)

        for p in range(npairs):
            def upd(p=p):
                acc_ref[...] += _dot(refs[2 * p][...].astype(BF16), refs[2 * p + 1][...].astype(BF16), dims)
            if npairs == 1:
                upd()
            else:
                pl.when(jnp.logical_and(kk >= offs[p], kk < offs[p] + nks[p]))(upd)

        @pl.when(kk == total - 1)
        def _():
            r = acc_ref[...]
            if add is not None:
                r = r + refs[2 * npairs][...]
            out_ref[...] = r.astype(out_ref.dtype)

    return _pcall(body, grid=(M // tm, N // tn, total), in_specs=in_specs,
                  out_specs=pl.BlockSpec((tm, tn), lambda i, j, kk: (i, j)),
                  out_shape=jax.ShapeDtypeStruct((M, N), out_dtype),
                  scratch_shapes=[pltpu.VMEM((tm, tn), F32)],
                  compiler_params=_cparams(("parallel", "parallel", "arbitrary")), name=name)(*ins)


def _mm1(a, b, mode, out_dtype, name, **kw):
    k = a.shape[0] if mode == 'tn' else a.shape[1]
    return _mm([(a, b, 0, 0, k)], mode, out_dtype, name, **kw)


def _sb_tiles(T):
    bq = min(SB_BQ, T)
    bk = min(SB_BK, bq)
    return bq, bk, bq // bk


def _sb_block(i, j, masked, q, k_ref, v_ref, accL, bq, bk, UU):
    r = pl.ds(pl.multiple_of(j * bk, bk), bk)
    kj = k_ref[r, :]
    vj = v_ref[r, :]
    z = _dot(q, kj, NT) * (HD ** -0.5)
    lb = jnp.minimum(z, 0.0) - jnp.log(1.0 + jnp.exp(-jnp.abs(z)))
    ln = lb - z
    mask = None
    if masked:
        mask = (j * bk + _iota((bq, bk), 1)) < (i * bq + _iota((bq, bk), 0))
        ln = jnp.where(mask, ln, 0.0)
    hi = ln.astype(BF16)
    lo = (ln - hi.astype(F32)).astype(BF16)
    later = _dot(jnp.concatenate([hi, lo], axis=1), UU) + accL
    a = jnp.exp(lb + later)
    if masked:
        a = jnp.where(mask, a, 0.0)
    return r, kj, vj, mask, ln, lb, a


def _sb_fwd(proj, T, H):
    bq, bk, nd = _sb_tiles(T)

    assert T // bk <= HD

    def body(q_ref, kf_ref, vf_ref, o_ref, l_ref, k_ref, v_ref):
        i = pl.program_id(1)

        @pl.when(i == 0)
        def _():
            k_ref[...] = kf_ref[...].astype(BF16)
            v_ref[...] = vf_ref[...].astype(BF16)

        q = q_ref[...].astype(BF16)
        U = (_iota((bk, bk), 0) > _iota((bk, bk), 1)).astype(BF16)
        UU = jnp.concatenate([U, U], axis=0)
        lane = _iota((bq, HD), 1)

        def blk(j, carry, masked):
            acc, accL, saved = carry
            _, _, vj, _, ln, _, a = _sb_block(i, j, masked, q, k_ref, v_ref, accL, bq, bk, UU)
            acc = acc + _dot(a.astype(BF16), vj)
            return acc, accL + jnp.sum(ln, axis=1, keepdims=True), jnp.where(lane == j, accL, saved)

        carry = (jnp.zeros((bq, HD), F32), jnp.zeros((bq, 1), F32), jnp.zeros((bq, HD), F32))
        for jj in range(nd):
            carry = blk(i * nd + (nd - 1 - jj), carry, True)
        def group(it, c):
            for jj in range(nd):
                c = blk((i - 1 - it) * nd + (nd - 1 - jj), c, False)
            return c

        carry = lax.fori_loop(0, i, group, carry)
        o_ref[...] = carry[0]
        l_ref[...] = carry[2]

    blk_q = pl.BlockSpec((bq, HD), lambda h, i: (i, h))
    return _pcall(
        body, grid=(H, T // bq),
        in_specs=[blk_q, pl.BlockSpec((T, HD), lambda h, i: (0, H + h)),
                  pl.BlockSpec((T, HD), lambda h, i: (0, 2 * H + h))],
        out_specs=[blk_q, blk_q],
        out_shape=[jax.ShapeDtypeStruct((T, H * HD), F32)] * 2,
        scratch_shapes=[pltpu.VMEM((T, HD), BF16), pltpu.VMEM((T, HD), BF16)],
        compiler_params=_cparams(("parallel", "arbitrary")), name="sb_fwd")(proj, proj, proj)


def _sb_bwd(proj, l_sb, do_sb, T, H):
    bq, bk, nd = _sb_tiles(T)
    nq = T // bq

    def body(q_ref, kf_ref, vf_ref, l_ref, do_ref, dq_ref, dk_ref, dv_ref, dk_acc, dv_acc, k_ref, v_ref):
        i = pl.program_id(1)

        @pl.when(i == 0)
        def _():
            dk_acc[...] = jnp.zeros_like(dk_acc)
            dv_acc[...] = jnp.zeros_like(dv_acc)
            k_ref[...] = kf_ref[...].astype(BF16)
            v_ref[...] = vf_ref[...].astype(BF16)

        q = q_ref[...].astype(BF16)
        dob = do_ref[...].astype(BF16)
        saved = l_ref[...]
        lane = _iota((bq, HD), 1)
        U = (_iota((bk, bk), 0) > _iota((bk, bk), 1)).astype(BF16)
        UU = jnp.concatenate([U, U], axis=0)
        Ue = (_iota((bk, bk), 0) < _iota((bk, bk), 1)).astype(BF16)

        def blk(j, carry, masked):
            dq, accP = carry
            accL = jnp.sum(jnp.where(lane == j, saved, 0.0), axis=1, keepdims=True)
            r, kj, vj, mask, _, lb, a = _sb_block(i, j, masked, q, k_ref, v_ref, accL, bq, bk, UU)
            p = a * _dot(dob, vj, NT)
            pre = _dot(p.astype(BF16), Ue) + accP
            sig = jnp.exp(lb)
            dz = p * (1.0 - sig) - pre * sig
            if masked:
                dz = jnp.where(mask, dz, 0.0)
            dzb = (dz * (HD ** -0.5)).astype(BF16)
            dq = dq + _dot(dzb, kj)
            dk_acc[r, :] += _dot(dzb, q, TN)
            dv_acc[r, :] += _dot(a.astype(BF16), dob, TN)
            return dq, accP + jnp.sum(p, axis=1, keepdims=True)

        carry = (jnp.zeros((bq, HD), F32), jnp.zeros((bq, 1), F32))
        def group(it, c):
            for jj in range(nd):
                c = blk(it * nd + jj, c, False)
            return c

        carry = lax.fori_loop(0, i, group, carry)
        for jj in range(nd):
            carry = blk(i * nd + jj, carry, True)
        dq_ref[...] = carry[0].astype(BF16)

        @pl.when(i == nq - 1)
        def _():
            dk_ref[...] = dk_acc[...].astype(BF16)
            dv_ref[...] = dv_acc[...].astype(BF16)

    W = H * HD
    blk_q = pl.BlockSpec((bq, HD), lambda h, i: (i, h))
    blk_t = pl.BlockSpec((T, HD), lambda h, i: (0, h))
    return _pcall(
        body, grid=(H, nq),
        in_specs=[blk_q, pl.BlockSpec((T, HD), lambda h, i: (0, H + h)),
                  pl.BlockSpec((T, HD), lambda h, i: (0, 2 * H + h)), blk_q, blk_q],
        out_specs=[blk_q, blk_t, blk_t],
        out_shape=[jax.ShapeDtypeStruct((T, W), BF16)] * 3,
        scratch_shapes=[pltpu.VMEM((T, HD), F32), pltpu.VMEM((T, HD), F32),
                        pltpu.VMEM((T, HD), BF16), pltpu.VMEM((T, HD), BF16)],
        compiler_params=_cparams(("parallel", "arbitrary")), name="sb_bwd")(proj, proj, proj, l_sb, do_sb)


def _gdn_chunk(q, k, v, bb, gb, S):
    G, C = q.shape[0], q.shape[1]
    ri, ci = _iota((G, C, C), 1), _iota((G, C, C), 2)
    tril, strict = ri >= ci, ri > ci
    e0 = (_iota((G, C, HD), 2) == 0).astype(F32)
    g_col = _dotf(gb, e0, NTB)
    g_row = _dotf(e0, gb, NTB)
    decay = jnp.where(tril, jnp.exp(jnp.where(tril, g_col - g_row, 0.0)), 0.0)
    eg = jnp.exp(gb)
    qs = q * (HD ** -0.5)
    kb = k * bb
    lw = jnp.where(strict, _bdot_nt(kb, k) * decay, 0.0)
    x = (ri == ci).astype(F32) - lw
    pw = lw
    for _ in range(C.bit_length() - 2):
        pw = _dotf(pw, pw, NNB)
        x = x + _dotf(x, pw, NNB)
    u = _bdot_nn(x, v * bb)
    w = _bdot_nn(x, kb * eg)
    aq = jnp.where(tril, _bdot_nt(qs, k) * decay, 0.0)
    vnew = u - _bdot_nn(w, S)
    o = _bdot_nn(qs * eg, S) + _bdot_nn(aq, vnew)
    g_last = _dotf((ci == C - 1).astype(F32), gb, NNB)
    g_last_s = _dotf((_iota((G, HD, C), 2) == C - 1).astype(F32), gb, NNB)
    s_new = S * jnp.exp(g_last_s) + _bdot_tn(k * jnp.exp(g_last - gb), vnew)
    return o, s_new


def _gdn_fwd(qn, kn, vc, beta_b, g_b, T, H):
    tb = min(GDN_TB, T)
    nc = tb // CH
    G = min(GDN_G, H)

    def body(q_ref, k_ref, v_ref, b_ref, g_ref, o_ref, s_ref, s_scr):
        @pl.when(pl.program_id(1) == 0)
        def _():
            s_scr[...] = jnp.zeros_like(s_scr)

        def step(c, carry):
            r = pl.ds(pl.multiple_of(c * CH, CH), CH)
            s = s_scr[...]
            s_ref[:, c] = s
            o, s2 = _gdn_chunk(q_ref[:, r, :], k_ref[:, r, :], v_ref[:, r, :], b_ref[:, r, :], g_ref[:, r, :], s)
            o_ref[:, r, :] = o
            s_scr[...] = s2
            return carry

        lax.fori_loop(0, nc, step, 0)

    blk3 = pl.BlockSpec((G, tb, HD), lambda h, t: (h, t, 0))
    return _pcall(
        body, grid=(H // G, T // tb), in_specs=[blk3] * 5,
        out_specs=[blk3, pl.BlockSpec((G, nc, HD, HD), lambda h, t: (h, t, 0, 0))],
        out_shape=[jax.ShapeDtypeStruct((H, T, HD), F32), jax.ShapeDtypeStruct((H, T // CH, HD, HD), F32)],
        scratch_shapes=[pltpu.VMEM((G, HD, HD), F32)],
        compiler_params=_cparams(("parallel", "arbitrary")), name="gdn_fwd")(qn, kn, vc, beta_b, g_b)


def _gdn_bwd(qn, kn, vc, beta_b, g_b, s_all, do, T, H):
    tb = min(GDN_TB, T)
    nc = tb // CH
    nt = T // tb
    G = min(GDN_G, H)

    def body(q_ref, k_ref, v_ref, b_ref, g_ref, s_ref, do_ref, dq_ref, dk_ref, dv_ref, db_ref, dg_ref, ds_scr):
        @pl.when(pl.program_id(1) == 0)
        def _():
            ds_scr[...] = jnp.zeros_like(ds_scr)

        def step(it, carry):
            c = nc - 1 - it
            r = pl.ds(pl.multiple_of(c * CH, CH), CH)
            _, vjp = jax.vjp(_gdn_chunk, q_ref[:, r, :], k_ref[:, r, :], v_ref[:, r, :], b_ref[:, r, :],
                             g_ref[:, r, :], s_ref[:, c])
            dq, dk, dv, db, dg, ds = vjp((do_ref[:, r, :], ds_scr[...]))
            dq_ref[:, r, :] = dq
            dk_ref[:, r, :] = dk
            dv_ref[:, r, :] = dv
            db_ref[:, r, :] = db
            dg_ref[:, r, :] = dg
            ds_scr[...] = ds
            return carry

        lax.fori_loop(0, nc, step, 0)

    blk3 = pl.BlockSpec((G, tb, HD), lambda h, t: (h, nt - 1 - t, 0))
    return _pcall(
        body, grid=(H // G, nt),
        in_specs=[blk3] * 5 + [pl.BlockSpec((G, nc, HD, HD), lambda h, t: (h, nt - 1 - t, 0, 0)), blk3],
        out_specs=[blk3] * 5,
        out_shape=[jax.ShapeDtypeStruct((H, T, HD), F32)] * 5,
        scratch_shapes=[pltpu.VMEM((G, HD, HD), F32)],
        compiler_params=_cparams(("parallel", "arbitrary")), name="gdn_bwd")(qn, kn, vc, beta_b, g_b, s_all, do)


def _conv_tiles(T, C):
    return min(HEAD_TM, T), min(CONV_CW, C)


def _shift_down(main, halo, first, d):
    halo = jnp.where(first, 0.0, halo)
    ext = jnp.concatenate([halo, main], axis=0)
    return pltpu.roll(ext, d, 0)[8:]


def _conv_fwd(proj, conv_w, T, H):
    W = H * HD
    C = 3 * W
    tm, cw = _conv_tiles(T, C)
    col0 = 4 * W // cw

    def body(x_ref, h_ref, w_ref, o_ref):
        first = pl.program_id(1) == 0
        main = x_ref[...]
        halo = h_ref[...]
        w = w_ref[...]
        out = w[CONV_K - 1:CONV_K, :] * main
        for d in range(1, CONV_K):
            out = out + w[CONV_K - 1 - d:CONV_K - d, :] * _shift_down(main, halo, first, d)
        o_ref[...] = out

    return _pcall(
        body, grid=(C // cw, T // tm),
        in_specs=[pl.BlockSpec((tm, cw), lambda cb, i: (i, col0 + cb)),
                  pl.BlockSpec((8, cw), lambda cb, i: (jnp.maximum(i * (tm // 8) - 1, 0), col0 + cb)),
                  pl.BlockSpec((CONV_K, cw), lambda cb, i: (0, cb))],
        out_specs=pl.BlockSpec((tm, cw), lambda cb, i: (i, cb)),
        out_shape=jax.ShapeDtypeStruct((T, C), F32),
        compiler_params=_cparams(("parallel", "arbitrary")), name="conv_fwd")(proj, proj, conv_w)


def _conv_bwd(proj, conv_w, dconv, T, H):
    W = H * HD
    C = 3 * W
    tm, cw = _conv_tiles(T, C)
    col0 = 4 * W // cw
    nt = T // tm

    def body(x_ref, h_ref, w_ref, d_ref, dn_ref, dx_ref, dw_ref):
        i = pl.program_id(1)
        first = i == 0
        main = x_ref[...]
        halo = h_ref[...]
        w = w_ref[...]
        dmain = d_ref[...]
        dnext = jnp.where(i == nt - 1, 0.0, dn_ref[...])
        dext = jnp.concatenate([dmain, dnext], axis=0)
        dx = w[CONV_K - 1:CONV_K, :] * dmain
        rows = [jnp.sum(dmain * main, axis=0, keepdims=True)]
        for d in range(1, CONV_K):
            dx = dx + w[CONV_K - 1 - d:CONV_K - d, :] * pltpu.roll(dext, tm + 8 - d, 0)[:tm]
            rows.append(jnp.sum(dmain * _shift_down(main, halo, first, d), axis=0, keepdims=True))
        dx_ref[...] = dx.astype(BF16)

        @pl.when(first)
        def _():
            dw_ref[...] = jnp.zeros_like(dw_ref)

        for d in range(CONV_K):
            dw_ref[CONV_K - 1 - d:CONV_K - d, :] += rows[d]

    return _pcall(
        body, grid=(C // cw, nt),
        in_specs=[pl.BlockSpec((tm, cw), lambda cb, i: (i, col0 + cb)),
                  pl.BlockSpec((8, cw), lambda cb, i: (jnp.maximum(i * (tm // 8) - 1, 0), col0 + cb)),
                  pl.BlockSpec((CONV_K, cw), lambda cb, i: (0, cb)),
                  pl.BlockSpec((tm, cw), lambda cb, i: (i, cb)),
                  pl.BlockSpec((8, cw), lambda cb, i: (jnp.minimum((i + 1) * (tm // 8), T // 8 - 1), cb))],
        out_specs=[pl.BlockSpec((tm, cw), lambda cb, i: (i, cb)), pl.BlockSpec((CONV_K, cw), lambda cb, i: (0, cb))],
        out_shape=[jax.ShapeDtypeStruct((T, C), BF16), jax.ShapeDtypeStruct((CONV_K, C), F32)],
        compiler_params=_cparams(("parallel", "arbitrary")), name="conv_bwd")(proj, proj, conv_w, dconv, dconv)


def _prep_fn(H, tm):
    def fn(cq, ck, cv, small, al, dtb):
        h = pl.program_id(1)
        lane = _iota((HD, HD), 0)
        oh_b = (lane == h).astype(F32)
        oh_a = (lane == H + h).astype(F32)
        q, k, v = _silu(cq), _silu(ck), _silu(cv)
        qn = q * lax.rsqrt(jnp.sum(q * q, axis=-1, keepdims=True) + L2_EPS)
        kn = k * lax.rsqrt(jnp.sum(k * k, axis=-1, keepdims=True) + L2_EPS)
        beta = _dotf(_sigmoid(small), oh_b)
        g = _dotf(-jnp.exp(al) * _softplus(small + dtb), oh_a)
        nch = tm // CH
        tri = (_iota((nch, CH, CH), 1) >= _iota((nch, CH, CH), 2)).astype(F32)
        gc = _dotf(tri, g.reshape(nch, CH, HD), NNB).reshape(tm, HD)
        return qn, kn, v, beta, gc
    return fn


def _prep_specs(T, H):
    tm = min(HEAD_TM, T)
    W = H * HD
    row = lambda s: pl.BlockSpec((tm, HD), lambda i, h, s=s: (i, s * H + h))
    small = pl.BlockSpec((tm, HD), lambda i, h: (i, 0))
    par = pl.BlockSpec((1, HD), lambda i, h: (0, 0))
    blk = pl.BlockSpec((tm, HD), lambda i, h: (i, h))
    blk3 = pl.BlockSpec((None, tm, HD), lambda i, h: (h, i, 0))
    return tm, W, row, small, par, blk, blk3


def _prep_fwd(conv, small, al, dtb, T, H):
    tm, W, row, sm, par, blk, blk3 = _prep_specs(T, H)
    return _ew(_prep_fn(H, tm), (T // tm, H), [conv, conv, conv, small, al, dtb],
               [row(0), row(1), row(2), sm, par, par],
               [((H, T, HD), F32)] * 5, [blk3] * 5, name="gdn_prep")


def _prep_bwd(conv, small, al, dtb, dqn, dkn, dvc, dbeta, dg, T, H):
    tm, W, row, sm, par, blk, blk3 = _prep_specs(T, H)
    f = _prep_fn(H, tm)

    def fn(cq, ck, cv, small_, al_, dtb_, a, b, c, d, e):
        _, vjp = jax.vjp(f, cq, ck, cv, small_, al_, dtb_)
        return vjp((a, b, c, d, e))

    return _ew(fn, (T // tm, H), [conv, conv, conv, small, al, dtb, dqn, dkn, dvc, dbeta, dg],
               [row(0), row(1), row(2), sm, par, par] + [blk3] * 5,
               [((T, W), F32)] * 3 + [((T, HD), F32), ((1, HD), F32), ((1, HD), F32)],
               [blk, blk, blk, sm, par, par], acc=[None, None, None, 'inner', 'all', 'all'], name="gdn_prep_bwd")


def _gate_fn(o_sb, z_sb, o_g, z_g, gnw):
    a_sb = o_sb * _silu(z_sb)
    a_g = _rms(o_g, gnw) * _silu(z_g)
    return a_sb, a_g


def _gate_specs(T, H):
    tm = min(HEAD_TM, T)
    blk = pl.BlockSpec((tm, HD), lambda i, h: (i, h))
    sec = lambda s: pl.BlockSpec((tm, HD), lambda i, h, s=s: (i, s * H + h))
    par = pl.BlockSpec((1, HD), lambda i, h: (0, 0))
    blk3 = pl.BlockSpec((None, tm, HD), lambda i, h: (h, i, 0))
    return tm, blk, blk3, sec, par


def _gate_fwd(o_sb, proj, o_g, gnw, T, H):
    tm, blk, blk3, sec, par = _gate_specs(T, H)
    W = H * HD
    return _ew(_gate_fn, (T // tm, H), [o_sb, proj, o_g, proj, gnw], [blk, sec(3), blk3, sec(7), par],
               [((T, W), BF16)] * 2, [blk, blk], name="gate_fwd")


def _gate_bwd(o_sb, proj, o_g, gnw, da_sb, da_g, T, H):
    tm, blk, blk3, sec, par = _gate_specs(T, H)
    W = H * HD

    def fn(o_sb_, z_sb, o_g_, z_g, gnw_, da, db):
        _, vjp = jax.vjp(_gate_fn, o_sb_, z_sb, o_g_, z_g, gnw_)
        return vjp((da, db))

    return _ew(fn, (T // tm, H), [o_sb, proj, o_g, proj, gnw, da_sb, da_g], [blk, sec(3), blk3, sec(7), par, blk, blk],
               [((T, W), F32), ((T, W), BF16), ((H, T, HD), F32), ((T, W), BF16), ((1, HD), F32)],
               [blk, blk, blk3, blk, par], acc=[None, None, None, None, 'all'], name="gate_bwd")


def _ada_fn(x, nw, scale, shift):
    return _rms(x, nw) * (1.0 + scale) + shift


def _row_specs(T, D):
    tm = min(EW_TM, T)
    return tm, pl.BlockSpec((tm, D), lambda i: (i, 0)), pl.BlockSpec((1, D), lambda i: (0, 0))


def _ada_fwd(x, nw, scale, shift):
    T, D = x.shape
    tm, row, par = _row_specs(T, D)
    return _ew(_ada_fn, (T // tm,), [x, nw, scale, shift], [row, par, par, par], [((T, D), BF16)], [row],
               name="ada_norm")[0]


def _ada_bwd(x, nw, scale, shift, dh, dx2):
    T, D = x.shape
    tm, row, par = _row_specs(T, D)

    def fn(x_, nw_, sc_, sh_, dh_, dx2_):
        _, vjp = jax.vjp(_ada_fn, x_, nw_, sc_, sh_)
        dx, dnw, dsc, dsh = vjp(dh_)
        return dx + dx2_, dnw, dsc, dsh

    return _ew(fn, (T // tm,), [x, nw, scale, shift, dh, dx2], [row, par, par, par, row, row],
               [((T, D), F32)] + [((1, D), F32)] * 3, [row, par, par, par], acc=[None, 'all', 'all', 'all'],
               name="ada_norm_bwd")


def _merge_fn(m_sb, m_g, p_sb, p_g):
    return _sigmoid(m_sb) * p_sb + _sigmoid(m_g) * p_g


def _merge_specs(T, D, H):
    tm, tc = min(HEAD_TM, T), min(512, D)
    nb = D // tc
    blk = pl.BlockSpec((tm, tc), lambda i, j: (i, j))
    sec = lambda s: pl.BlockSpec((tm, tc), lambda i, j, s=s: (i, s * nb + j))
    return tm, tc, blk, sec


def _merge_fwd(proj, p_sb, p_g, T, D, H):
    tm, tc, blk, sec = _merge_specs(T, D, H)
    return _ew(_merge_fn, (T // tm, D // tc), [proj, proj, p_sb, p_g], [sec(8), sec(9), blk, blk],
               [((T, D), BF16)], [blk], name="merge")[0]


def _merge_bwd(proj, p_sb, p_g, dy, T, D, H):
    tm, tc, blk, sec = _merge_specs(T, D, H)

    def fn(m_sb, m_g, p_sb_, p_g_, dy_):
        _, vjp = jax.vjp(_merge_fn, m_sb, m_g, p_sb_, p_g_)
        return vjp(dy_)

    return _ew(fn, (T // tm, D // tc), [proj, proj, p_sb, p_g, dy], [sec(8), sec(9), blk, blk, blk],
               [((T, D), BF16)] * 4, [blk] * 4, name="merge_bwd")


def _loss_head(x, u, gate, fnw, tgt):
    T, D = x.shape
    tm, row, par = _row_specs(T, D)

    def loss(x_, u_, gate_, fnw_, tgt_):
        y = _rms(x_ + gate_ * u_, fnw_)
        return 0.5 * jnp.sum(jnp.mean(jnp.square(y - tgt_), axis=-1))

    def fn(x_, u_, gate_, fnw_, tgt_):
        val, (dx, du, dgate, dfnw) = jax.value_and_grad(loss, argnums=(0, 1, 2, 3))(x_, u_, gate_, fnw_, tgt_)
        return jnp.full((1, HD), val, F32), dx, du, dgate, dfnw

    return _ew(fn, (T // tm,), [x, u, gate, fnw, tgt], [row, row, par, par, row],
               [((1, HD), F32), ((T, D), F32), ((T, D), BF16), ((1, D), F32), ((1, D), F32)],
               [pl.BlockSpec((1, HD), lambda i: (0, 0)), row, row, par, par],
               acc=['all', None, None, 'all', 'all'], name="loss_head")


def _mod_part(c_all, w_ada, b_ada):
    D, N = w_ada.shape
    tn = min(512, N)

    def fn(c, w, b):
        return _dot(_silu(c).astype(BF16), w.astype(BF16)) + b

    return _ew(fn, (N // tn,), [c_all, w_ada, b_ada],
               [pl.BlockSpec((8, D), lambda j: (0, 0)), pl.BlockSpec((D, tn), lambda j: (0, j)),
                pl.BlockSpec((1, tn), lambda j: (0, j))],
               [((8, N), F32)], [pl.BlockSpec((8, tn), lambda j: (0, j))], name="ada_mod")[0]


def _w_ada_grad(c_all, dmod):
    D, N = c_all.shape[1], dmod.shape[1]
    tn = min(512, N)

    def fn(c, dm):
        return _dot(_silu(c).astype(BF16), dm.astype(BF16), TN)

    return _ew(fn, (N // tn,), [c_all, dmod],
               [pl.BlockSpec((8, D), lambda j: (0, 0)), pl.BlockSpec((8, tn), lambda j: (0, j))],
               [((D, N), F32)], [pl.BlockSpec((D, tn), lambda j: (0, j))], name="w_ada_grad")[0]


def _sum8(packs):
    N = packs.shape[1]

    def fn(p):
        s = p[0:8]
        for d in range(1, 8):
            s = s + p[8 * d:8 * d + 8]
        return s

    return _ew(fn, (1,), [packs], [pl.BlockSpec((64, N), lambda i: (0, 0))], [((8, N), F32)],
               [pl.BlockSpec((8, N), lambda i: (0, 0))], name="sum_devices")[0]


def _sum_chips(q, c, name):
    _, R, C = q.shape
    tr = min(64, R)

    def body(c_ref, q_ref, o_ref):
        p = q_ref[...].astype(F32)
        o_ref[...] = (p[0] + p[1]) + (p[2] + p[3])

    grid_spec = pltpu.PrefetchScalarGridSpec(
        num_scalar_prefetch=1, grid=(R // tr,),
        in_specs=[pl.BlockSpec((4, tr, C), lambda i, c_ref: (0, i, 0))],
        out_specs=pl.BlockSpec((None, tr, C), lambda i, c_ref: (c_ref[0], i, 0)))
    return _pcall(body, grid_spec=grid_spec, out_shape=jax.ShapeDtypeStruct((2, R, C), F32),
                  compiler_params=_cparams(("arbitrary",)), name=name)(jnp.reshape(c, (1,)).astype(jnp.int32), q)


def _add_halves(g, rsib, c, name):
    _, R, C = g.shape
    rh = R // 2
    tr = min(64, rh)
    nb = rh // tr

    def body(c_ref, a_ref, b_ref, o_ref, o2_ref):
        v = (a_ref[...] + b_ref[...]).astype(o_ref.dtype)
        o_ref[...] = v
        o2_ref[...] = v

    spec = pl.BlockSpec((None, tr, C), lambda j, i, c_ref: (j, i, 0))
    grid_spec = pltpu.PrefetchScalarGridSpec(
        num_scalar_prefetch=1, grid=(4, nb),
        in_specs=[pl.BlockSpec((None, tr, C), lambda j, i, c_ref: (j, c_ref[0] * nb + i, 0)), spec],
        out_specs=[spec, spec])
    return _pcall(body, grid_spec=grid_spec, out_shape=[jax.ShapeDtypeStruct((4, rh, C), BF16)] * 2,
                  compiler_params=_cparams(("arbitrary", "arbitrary")), name=name)(
                      jnp.reshape(c, (1,)).astype(jnp.int32), g, rsib)


def _adamw(w, g, m, v, name):
    R, C = w.shape
    tr = R if R <= 64 else 64
    blk = pl.BlockSpec((tr, C), lambda i: (i, 0))

    def fn(w_, g_, m_, v_):
        m2 = ADAM_B1 * m_ + (1.0 - ADAM_B1) * g_
        v2 = ADAM_B2 * v_ + (1.0 - ADAM_B2) * jnp.square(g_)
        m_hat = m2 / (1.0 - ADAM_B1 ** ADAM_STEP)
        v_hat = v2 / (1.0 - ADAM_B2 ** ADAM_STEP)
        delta = -ADAM_LR * (m_hat / (jnp.sqrt(v_hat) + ADAM_EPS) + ADAM_WD * w_)
        return delta, m2, v2

    return _ew(fn, (R // tr,), [w, g, m, v], [blk] * 4, [((R, C), F32)] * 3, [blk] * 3, name=name)


def _place():
    x, y, c = lax.axis_index("x"), lax.axis_index("y"), lax.axis_index("c")
    return x, y, c, [(1 - x, y), (x, 1 - y), (1 - x, 1 - y)]


ANY = pl.BlockSpec(memory_space=pl.ANY)


def _allgather8(blk):
    m_per, n = blk.shape

    def body(x_ref, out_ref, send_sems, recv_sems, local_sem):
        x, y, c, chips = _place()
        me, sibling = (x, y, c), (x, y, 1 - c)

        def rows(px, py, pc):
            return out_ref.at[pl.ds((4 * px + 2 * py + pc) * m_per, m_per), :]

        def copy(k, block, to, src=None):
            return pltpu.make_async_remote_copy(
                src_ref=rows(*block) if src is None else src, dst_ref=rows(*block),
                send_sem=send_sems.at[k], recv_sem=recv_sems.at[k], device_id=to, device_id_type=MESH)

        mine = pltpu.make_async_copy(x_ref, rows(*me), local_sem)
        mine.start()
        first = [copy(0, me, sibling, src=x_ref)]
        first += [copy(1 + j, me, (*chip, c), src=x_ref) for j, chip in enumerate(chips)]
        for cp in first:
            cp.start()
        passed = [copy(4 + j, (*chip, c), sibling) for j, chip in enumerate(chips)]
        for j, chip in enumerate(chips):
            copy(1 + j, (*chip, c), me).wait_recv()
            passed[j].start()
        copy(0, sibling, me).wait_recv()
        for j, chip in enumerate(chips):
            copy(4 + j, (*chip, 1 - c), me).wait_recv()
        for cp in first + passed:
            cp.wait_send()
        mine.wait()

    vm = pl.BlockSpec(memory_space=pltpu.VMEM)
    return _pcall(body, out_shape=jax.ShapeDtypeStruct((8 * m_per, n), blk.dtype), in_specs=[vm], out_specs=vm,
                  scratch_shapes=[pltpu.SemaphoreType.DMA((7,)), pltpu.SemaphoreType.DMA((7,)),
                                  pltpu.SemaphoreType.DMA],
                  compiler_params=pltpu.CompilerParams(vmem_limit_bytes=VMEM_LIMIT), name="allgather8")(blk)


def _gather_weights(shards):
    n = len(shards)

    def body(*refs):
        ins, outs = refs[:n], refs[n:2 * n]
        send_sems, recv_sems, fsend_sems, frecv_sems = refs[2 * n:]
        x, y, c, chips = _place()
        mychip = 2 * x + y
        sends, fwds = [], []
        for a in range(n):
            rh = shards[a].shape[0] // 2
            for k, chip in enumerate(chips):
                cp = pltpu.make_async_remote_copy(
                    src_ref=ins[a].at[pl.ds(c * rh, rh), :], dst_ref=outs[a].at[mychip, pl.ds(c * rh, rh), :],
                    send_sem=send_sems.at[3 * a + k], recv_sem=recv_sems.at[3 * a + k],
                    device_id=(*chip, c), device_id_type=MESH)
                cp.start()
                sends.append(cp)
        for a in range(n):
            rh = shards[a].shape[0] // 2
            for k, (px, py) in enumerate(chips):
                land = outs[a].at[2 * px + py, pl.ds(c * rh, rh), :]
                pltpu.make_async_remote_copy(
                    src_ref=land, dst_ref=land, send_sem=send_sems.at[3 * a + k], recv_sem=recv_sems.at[3 * a + k],
                    device_id=(px, py, c), device_id_type=MESH).wait_recv()
                fw = pltpu.make_async_remote_copy(
                    src_ref=land, dst_ref=land, send_sem=fsend_sems.at[3 * a + k], recv_sem=frecv_sems.at[3 * a + k],
                    device_id=(x, y, 1 - c), device_id_type=MESH)
                fw.start()
                fwds.append(fw)
        for a in range(n):
            rh = shards[a].shape[0] // 2
            for k, (px, py) in enumerate(chips):
                land = outs[a].at[2 * px + py, pl.ds((1 - c) * rh, rh), :]
                pltpu.make_async_remote_copy(
                    src_ref=land, dst_ref=land, send_sem=fsend_sems.at[3 * a + k], recv_sem=frecv_sems.at[3 * a + k],
                    device_id=(x, y, 1 - c), device_id_type=MESH).wait_recv()
        for cp in sends + fwds:
            cp.wait_send()

    return _pcall(body, out_shape=[jax.ShapeDtypeStruct((4,) + s.shape, s.dtype) for s in shards],
                  in_specs=[ANY] * n, out_specs=[ANY] * n,
                  scratch_shapes=[pltpu.SemaphoreType.DMA((3 * n,))] * 4, name="gather_weights")(*shards)


def _row_chunks(rows, n):
    while n > 1 and rows % (8 * n):
        n //= 2
    return [(k * (rows // n), rows // n) for k in range(n)]


def _swap_partial_halves(gs):
    n = len(gs)
    plan = [(a, j, r0, nr) for a in range(n) for j in range(4)
            for r0, nr in _row_chunks(gs[a].shape[1] // 2, SWAP_CHUNKS)]

    def body(*refs):
        ins, outs = refs[:n], refs[n:2 * n]
        send_sems, recv_sems = refs[2 * n:]
        x, y, c, _ = _place()
        cps = []
        for k, (a, j, r0, nr) in enumerate(plan):
            rh = gs[a].shape[1] // 2
            cp = pltpu.make_async_remote_copy(
                src_ref=ins[a].at[j, pl.ds((1 - c) * rh + r0, nr), :], dst_ref=outs[a].at[j, pl.ds(r0, nr), :],
                send_sem=send_sems.at[k], recv_sem=recv_sems.at[k], device_id=(x, y, 1 - c), device_id_type=MESH)
            cp.start()
            cps.append(cp)
        for cp in cps:
            cp.wait()

    half = [jax.ShapeDtypeStruct((4, g.shape[1] // 2, g.shape[2]), g.dtype) for g in gs]
    return _pcall(body, out_shape=half, in_specs=[ANY] * n, out_specs=[ANY] * n,
                  scratch_shapes=[pltpu.SemaphoreType.DMA((len(plan),))] * 2, name="swap_partial_halves")(*gs)


def _chip_scatter(ps, lands):
    n = len(ps)

    def body(*refs):
        ins, outs = refs[:n], refs[2 * n:3 * n]
        send_sems, recv_sems = refs[3 * n:]
        x, y, c, chips = _place()
        mychip = 2 * x + y
        cps = []
        for a in range(n):
            for k, (px, py) in enumerate(chips):
                cp = pltpu.make_async_remote_copy(
                    src_ref=ins[a].at[2 * px + py], dst_ref=outs[a].at[mychip],
                    send_sem=send_sems.at[3 * a + k], recv_sem=recv_sems.at[3 * a + k],
                    device_id=(px, py, c), device_id_type=MESH)
                cp.start()
                cps.append(cp)
        for a in range(n):
            for k, (px, py) in enumerate(chips):
                land = outs[a].at[2 * px + py]
                pltpu.make_async_remote_copy(
                    src_ref=land, dst_ref=land, send_sem=send_sems.at[3 * a + k], recv_sem=recv_sems.at[3 * a + k],
                    device_id=(px, py, c), device_id_type=MESH).wait_recv()
        for cp in cps:
            cp.wait_send()

    return _pcall(body, out_shape=[jax.ShapeDtypeStruct(p.shape, p.dtype) for p in ps],
                  in_specs=[ANY] * (2 * n), out_specs=[ANY] * n,
                  input_output_aliases={n + a: a for a in range(n)},
                  scratch_shapes=[pltpu.SemaphoreType.DMA((3 * n,))] * 2, name="chip_scatter")(*ps, *lands)


def _swap_final_halves(hs):
    n = len(hs)
    plan = [(a, r0, nr) for a in range(n) for r0, nr in _row_chunks(hs[a].shape[1], 2 * SWAP_CHUNKS)]

    def body(*refs):
        outs = refs[n:2 * n]
        send_sems, recv_sems = refs[2 * n:]
        x, y, c, _ = _place()
        cps = []
        for k, (a, r0, nr) in enumerate(plan):
            mine = outs[a].at[c, pl.ds(r0, nr), :]
            cp = pltpu.make_async_remote_copy(
                src_ref=mine, dst_ref=mine, send_sem=send_sems.at[k], recv_sem=recv_sems.at[k],
                device_id=(x, y, 1 - c), device_id_type=MESH)
            cp.start()
            cps.append(cp)
        for k, (a, r0, nr) in enumerate(plan):
            land = outs[a].at[1 - c, pl.ds(r0, nr), :]
            pltpu.make_async_remote_copy(
                src_ref=land, dst_ref=land, send_sem=send_sems.at[k], recv_sem=recv_sems.at[k],
                device_id=(x, y, 1 - c), device_id_type=MESH).wait_recv()
        for cp in cps:
            cp.wait_send()

    return _pcall(body, out_shape=[jax.ShapeDtypeStruct(h.shape, h.dtype) for h in hs],
                  in_specs=[ANY] * n, out_specs=[ANY] * n, input_output_aliases={a: a for a in range(n)},
                  scratch_shapes=[pltpu.SemaphoreType.DMA((len(plan),))] * 2, name="swap_final_halves")(*hs)


def _pad_cols(a, n):
    return jnp.pad(a, ((0, 0), (0, n - a.shape[1])))


def kernel(x, c, w_ada, b_ada, norm_w, w_in, gdn_conv_w, gdn_a_log, gdn_dt_bias, gdn_norm_w, w_proj_sb, w_proj_gdn, w_out, final_norm_w, loss_target, m_w_ada, m_b_ada, m_norm_w, m_w_in, m_gdn_conv_w, m_gdn_a_log, m_gdn_dt_bias, m_gdn_norm_w, m_w_proj_sb, m_w_proj_gdn, m_w_out, m_final_norm_w, v_w_ada, v_b_ada, v_norm_w, v_w_in, v_gdn_conv_w, v_gdn_a_log, v_gdn_dt_bias, v_gdn_norm_w, v_w_proj_sb, v_w_proj_gdn, v_w_out, v_final_norm_w):
    T, D = x.shape[1], x.shape[2]
    H = gdn_a_log.shape[1]
    W = H * HD
    assert W == D and T % CH == 0
    NA = w_ada.shape[2]
    NI = w_in.shape[2]
    CW = gdn_conv_w.shape[2]
    px, py, pc = lax.axis_index("x"), lax.axis_index("y"), lax.axis_index("c")
    chip = 2 * px + py
    me = 2 * chip + pc
    x2d, tgt = x[0], loss_target[0]
    PW = 3 * D

    pack1 = jnp.concatenate([c, _pad_cols(gdn_conv_w[0], D), jnp.zeros((3, D), F32)], axis=0)
    got1 = _allgather8(pack1).reshape(8, 8, D)
    c_all = got1[:, 0, :]
    conv_w = jnp.concatenate([got1[2 * j, 1:1 + CONV_K, :CW] for j in range(4)], axis=1)

    b_shard = lax.dynamic_slice_in_dim(b_ada, chip * NA, NA, axis=1)
    mod_part = _mod_part(c_all, w_ada[0], b_shard)
    got2 = _allgather8(mod_part).reshape(8, 8, NA)
    mod = jnp.concatenate([lax.dynamic_index_in_dim(got2[2 * j], me, 0) for j in range(4)], axis=1)
    shift, scale, gate = mod[:, :D], mod[:, D:2 * D], mod[:, 2 * D:]

    own = [w_in[0].astype(BF16), w_proj_sb[0].astype(BF16), w_proj_gdn[0].astype(BF16), w_out[0].astype(BF16)]
    is_own = (jnp.arange(4) == chip)[:, None, None]
    wg = [jnp.where(is_own, o[None], g) for o, g in zip(own, _gather_weights(own))]
    w_in_full = jnp.concatenate([wg[0][j] for j in range(4)], axis=1)
    w_big = jnp.concatenate([w_in_full[:, :8 * W], w_in_full[:, 8 * W + 2 * H:]], axis=1)
    w_small = _pad_cols(w_in_full[:, 8 * W:8 * W + 2 * H], HD)
    w_psb, w_pg, w_o = (wg[i].reshape(D, D) for i in (1, 2, 3))

    h = _ada_fwd(x2d, norm_w, scale, shift)
    proj = _mm1(h, w_big, 'nn', F32, "proj_big")
    small = _mm1(h, w_small, 'nn', F32, "proj_small")
    o_sb, l_sb = _sb_fwd(proj, T, H)
    conv = _conv_fwd(proj, conv_w, T, H)
    al = jnp.pad(gdn_a_log, ((0, 0), (H, HD - 2 * H)))
    dtb = jnp.pad(gdn_dt_bias, ((0, 0), (H, HD - 2 * H)))
    qn, kn, vc, beta_b, g_b = _prep_fwd(conv, small, al, dtb, T, H)
    o_g, s_all = _gdn_fwd(qn, kn, vc, beta_b, g_b, T, H)
    a_sb, a_g = _gate_fwd(o_sb, proj, o_g, gdn_norm_w, T, H)
    p_sb = _mm1(a_sb, w_psb, 'nn', F32, "proj_sb")
    p_g = _mm1(a_g, w_pg, 'nn', F32, "proj_gdn")
    y = _merge_fwd(proj, p_sb, p_g, T, D, H)
    u = _mm1(y, w_o, 'nn', F32, "proj_out")
    loss_p, dx2, du, dgate, g_fnw = _loss_head(x2d, u, gate, final_norm_w.reshape(1, D), tgt)

    dy = _mm1(du, w_o, 'nt', F32, "d_merge")
    g_w_out = _mm1(y, du, 'tn', F32, "g_w_out")
    dm_sb, dm_g, dp_sb, dp_g = _merge_bwd(proj, p_sb, p_g, dy, T, D, H)
    da_sb = _mm1(dp_sb, w_psb, 'nt', F32, "d_a_sb")
    g_w_psb = _mm1(a_sb, dp_sb, 'tn', F32, "g_w_proj_sb")
    da_g = _mm1(dp_g, w_pg, 'nt', F32, "d_a_gdn")
    g_w_pg = _mm1(a_g, dp_g, 'tn', F32, "g_w_proj_gdn")
    do_sb, dz_sb, do_g, dz_g, g_gnw = _gate_bwd(o_sb, proj, o_g, gdn_norm_w, da_sb, da_g, T, H)
    dq_sb, dk_sb, dv_sb = _sb_bwd(proj, l_sb, do_sb, T, H)
    dqn, dkn, dvc, dbeta_b, dg_b = _gdn_bwd(qn, kn, vc, beta_b, g_b, s_all, do_g, T, H)
    dcq, dck, dcv, dsmall, dal, ddtb = _prep_bwd(conv, small, al, dtb, dqn, dkn, dvc, dbeta_b, dg_b, T, H)
    dpre, g_conv = _conv_bwd(proj, conv_w, jnp.concatenate([dcq, dck, dcv], axis=1), T, H)

    secs = [(dq_sb, 0), (dk_sb, W), (dv_sb, 2 * W), (dz_sb, 3 * W), (dpre, 4 * W), (dz_g, 7 * W),
            (dm_sb, 8 * W), (dm_g, 9 * W)]
    dh = _mm([(a, w_big, 0, off, a.shape[1]) for a, off in secs] + [(dsmall, w_small, 0, 0, HD)],
             'nt', F32, "d_h", tk=512)
    g_secs = [_mm1(h, a, 'tn', F32, "g_w_in_%d" % i) for i, (a, _) in enumerate(secs)]
    g_small = _mm1(h, dsmall, 'tn', F32, "g_w_in_small")
    grad_x, g_nw, dscale, dshift = _ada_bwd(x2d, norm_w, scale, shift, dh, dx2)

    dmod = jnp.concatenate([dshift, dscale, dgate], axis=1)
    misc = jnp.concatenate([g_nw, g_fnw, g_gnw, dal[:, H:2 * H], ddtb[:, H:2 * H], loss_p[:, :1]], axis=1)
    pack3 = jnp.concatenate([dmod, g_conv, _pad_cols(misc, PW), jnp.zeros((2, PW), F32)], axis=0)
    got3 = _allgather8(pack3)
    tot = _sum8(got3)
    dmod_all = got3.reshape(8, 8, PW)[:, 0, :]
    g_w_ada = _w_ada_grad(c_all, lax.dynamic_slice_in_dim(dmod_all, chip * NA, NA, axis=1))
    g_conv_sh = lax.dynamic_slice_in_dim(tot[1:1 + CONV_K], chip * CW, CW, axis=1)
    loss = tot[5, 2 * D + HD + 2 * H]

    g_in = jnp.concatenate(g_secs[:6] + [g_small[:, :2 * H]] + g_secs[6:], axis=1)
    g_full = [jnp.stack([g_in[:, j * NI:(j + 1) * NI] for j in range(4)]),
              g_w_psb.reshape(4, D // 4, D), g_w_pg.reshape(4, D // 4, D), g_w_out.reshape(4, D // 4, D)]
    g_sib = _swap_partial_halves(g_full)
    parts = [_add_halves(g, r, pc, "add_halves_%d" % i) for i, (g, r) in enumerate(zip(g_full, g_sib))]
    got = _chip_scatter([p[0] for p in parts], [p[1] for p in parts])
    halves = [_sum_chips(q, pc, "sum_chips_%d" % i) for i, q in enumerate(got)]
    g_red = [f.reshape(f.shape[1] * 2, f.shape[2]) for f in _swap_final_halves(halves)]

    out = {}

    def upd(name, w, g, m, v, shape):
        d_, m_, v_ = _adamw(w, g, m, v, "adamw_" + name)
        out[name] = (g.reshape(shape), d_.reshape(shape), m_.reshape(shape), v_.reshape(shape))

    upd("w_ada", w_ada[0], g_w_ada, m_w_ada[0], v_w_ada[0], w_ada.shape)
    upd("w_in", w_in[0], g_red[0], m_w_in[0], v_w_in[0], w_in.shape)
    upd("gdn_conv_w", gdn_conv_w[0], g_conv_sh, m_gdn_conv_w[0], v_gdn_conv_w[0], gdn_conv_w.shape)
    upd("w_proj_sb", w_proj_sb[0], g_red[1], m_w_proj_sb[0], v_w_proj_sb[0], w_proj_sb.shape)
    upd("w_proj_gdn", w_proj_gdn[0], g_red[2], m_w_proj_gdn[0], v_w_proj_gdn[0], w_proj_gdn.shape)
    upd("w_out", w_out[0], g_red[3], m_w_out[0], v_w_out[0], w_out.shape)

    def packs(b, nw, fnw, gnw, a, dt):
        row = jnp.concatenate([nw, fnw.reshape(1, D), gnw, a, dt], axis=1)
        return jnp.concatenate([b, _pad_cols(row, PW), jnp.zeros((6, PW), F32)], axis=0)

    g_pack = jnp.concatenate([tot[0:1], tot[5:6], jnp.zeros((6, PW), F32)], axis=0)
    d_, m_, v_ = _adamw(packs(b_ada, norm_w, final_norm_w, gdn_norm_w, gdn_a_log, gdn_dt_bias), g_pack,
                        packs(m_b_ada, m_norm_w, m_final_norm_w, m_gdn_norm_w, m_gdn_a_log, m_gdn_dt_bias),
                        packs(v_b_ada, v_norm_w, v_final_norm_w, v_gdn_norm_w, v_gdn_a_log, v_gdn_dt_bias),
                        "adamw_small")
    offs = {"norm_w": (0, D, (1, D)), "final_norm_w": (D, D, (D,)), "gdn_norm_w": (2 * D, HD, (1, HD)),
            "gdn_a_log": (2 * D + HD, H, (1, H)), "gdn_dt_bias": (2 * D + HD + H, H, (1, H))}
    out["b_ada"] = tuple(a[0:1] for a in (g_pack, d_, m_, v_))
    for name, (o, n_, shp) in offs.items():
        out[name] = tuple(a[1, o:o + n_].reshape(shp) for a in (g_pack, d_, m_, v_))

    names = ['w_ada', 'b_ada', 'norm_w', 'w_in', 'gdn_conv_w', 'gdn_a_log', 'gdn_dt_bias', 'gdn_norm_w',
             'w_proj_sb', 'w_proj_gdn', 'w_out', 'final_norm_w']
    return (loss, grad_x.reshape(x.shape), *[out[n][0] for n in names], *[out[n][1] for n in names],
            *[out[n][2] for n in names], *[out[n][3] for n in names])
```

```python
import functools

import jax
import jax.numpy as jnp
from jax import lax
from jax.experimental import pallas as pl
from jax.experimental.pallas import tpu as pltpu

F32 = jnp.float32
BF16 = jnp.bfloat16
HD = 128
CH = 64
CONV_K = 4
NORM_EPS = 1e-6
L2_EPS = 1e-6
ADAM_LR, ADAM_B1, ADAM_B2, ADAM_EPS, ADAM_WD, ADAM_STEP = 0.001, 0.9, 0.999, 1e-08, 0.01, 10

VMEM_LIMIT = 56 * 1024 * 1024
SB_BQ, SB_BK = 512, 256
GDN_TB = 256
GDN_G = 8
EW_TM = 256
HEAD_TM = 512
CONV_CW = 512
MM_TM, MM_TN, MM_TK = 1024, 1024, 1024
MM_TK_MANY = 512
SWAP_CHUNKS = 4
MESH = pl.DeviceIdType.MESH

NN = (((1,), (0,)), ((), ()))
NT = (((1,), (1,)), ((), ()))
TN = (((0,), (0,)), ((), ()))


def _pcall(body, **kw):
    return pl.pallas_call(body, **kw)


def _cparams(sem=None):
    return pltpu.CompilerParams(dimension_semantics=sem, vmem_limit_bytes=VMEM_LIMIT)


def _dot(a, b, dims=NN):
    return lax.dot_general(a, b, dims, preferred_element_type=F32)


def _dotf(a, b, dims=NN):
    return lax.dot_general(a, b, dims, precision=lax.Precision.HIGHEST, preferred_element_type=F32)


def _bdot_make(dims, da_rule, db_rule):
    @jax.custom_vjp
    def f(a, b):
        return _dot(a.astype(BF16), b.astype(BF16), dims)

    def fwd(a, b):
        return f(a, b), (a, b)

    def bwd(res, g):
        a, b = res
        return da_rule(g, a, b), db_rule(g, a, b)

    f.defvjp(fwd, bwd)
    return f


def _rdot(a, b, dims):
    return _dot(a.astype(BF16), b.astype(BF16), dims)


NNB =(((2,), (1,)), ((0,), (0,)))
NTB = (((2,), (2,)), ((0,), (0,)))
TNB = (((1,), (1,)), ((0,), (0,)))
_bdot_nn = _bdot_make(NNB, lambda g, a, b: _rdot(g, b, NTB), lambda g, a, b: _rdot(a, g, TNB))
_bdot_nt = _bdot_make(NTB, lambda g, a, b: _rdot(g, b, NNB), lambda g, a, b: _rdot(g, a, TNB))
_bdot_tn = _bdot_make(TNB, lambda g, a, b: _rdot(b, g, NTB), lambda g, a, b: _rdot(a, g, NNB))


def _iota(shape, axis):
    return lax.broadcasted_iota(jnp.int32, shape, axis)


def _sigmoid(x):
    e = jnp.exp(-jnp.abs(x))
    return jnp.where(x >= 0, 1.0 / (1.0 + e), e / (1.0 + e))


def _silu(x):
    return x * _sigmoid(x)


def _softplus(x):
    return jnp.maximum(x, 0.0) + jnp.log(1.0 + jnp.exp(-jnp.abs(x)))


def _rms(x, w):
    return x * lax.rsqrt(jnp.mean(x * x, axis=-1, keepdims=True) + NORM_EPS) * w


def _ew(fn, grid, ins, in_specs, outs, out_specs, acc=None, name=None):
    n_in = len(ins)
    acc = acc or [None] * len(outs)

    def body(*refs):
        vals = fn(*[r[...] for r in refs[:n_in]])
        if not isinstance(vals, (tuple, list)):
            vals = (vals,)
        for r, v, a in zip(refs[n_in:], vals, acc):
            if a is None:
                r[...] = v.astype(r.dtype)
                continue
            first = pl.program_id(len(grid) - 1) == 0
            if a == 'all':
                for ax in range(len(grid) - 1):
                    first = jnp.logical_and(first, pl.program_id(ax) == 0)

            @pl.when(first)
            def _():
                r[...] = v.astype(r.dtype)

            @pl.when(jnp.logical_not(first))
            def _():
                r[...] += v.astype(r.dtype)

    res = _pcall(body, grid=grid, in_specs=in_specs, out_specs=out_specs,
                 out_shape=[jax.ShapeDtypeStruct(s, d) for s, d in outs],
                 compiler_params=_cparams(("arbitrary",) * len(grid)), name=name)(*ins)
    return res


def _mm(pairs, mode, out_dtype, name, add=None, tm=None, tn=None, tk=None):
    a0, b0 = pairs[0][0], pairs[0][1]
    M = a0.shape[1] if mode == 'tn' else a0.shape[0]
    N = b0.shape[0] if mode == 'nt' else b0.shape[1]
    tm = min(tm or MM_TM, M)
    tn = min(tn or MM_TN, N)
    tk = tk or MM_TK
    tks = [min(tk, p[4]) for p in pairs]
    nks = [p[4] // t for p, t in zip(pairs, tks)]
    offs = [sum(nks[:i]) for i in range(len(pairs))]
    total = sum(nks)
    assert M % tm == 0 and N % tn == 0 and all(p[4] % t == 0 for p, t in zip(pairs, tks)), (name, M, N)
    dims = {'nn': NN, 'nt': NT, 'tn': TN}[mode]

    in_specs, ins = [], []
    for (a, b, ao, bo, kl), t, nk, off in zip(pairs, tks, nks, offs):
        def kidx(kk, nk=nk, off=off):
            return jnp.minimum(jnp.maximum(kk - off, 0), nk - 1)
        if mode == 'tn':
            in_specs.append(pl.BlockSpec((t, tm), lambda i, j, kk, f=kidx, o=ao // t: (o + f(kk), i)))
        else:
            in_specs.append(pl.BlockSpec((tm, t), lambda i, j, kk, f=kidx, o=ao // t: (i, o + f(kk))))
        if mode == 'nt':
            in_specs.append(pl.BlockSpec((tn, t), lambda i, j, kk, f=kidx, o=bo // t: (j, o + f(kk))))
        else:
            in_specs.append(pl.BlockSpec((t, tn), lambda i, j, kk, f=kidx, o=bo // t: (o + f(kk), j)))
        ins += [a, b]
    if add is not None:
        in_specs.append(pl.BlockSpec((tm, tn), lambda i, j, kk: (i, j)))
        ins.append(add)
    npairs = len(pairs)

    def body(*refs):
        out_ref, acc_ref = refs[-2], refs[-1]
        kk = pl.program_id(2)

        @pl.when(kk == 0)
        def _():
            acc_ref[...] = jnp.zeros_like(acc_ref)

        for p in range(npairs):
            def upd(p=p):
                acc_ref[...] += _dot(refs[2 * p][...].astype(BF16), refs[2 * p + 1][...].astype(BF16), dims)
            if npairs == 1:
                upd()
            else:
                pl.when(jnp.logical_and(kk >= offs[p], kk < offs[p] + nks[p]))(upd)

        @pl.when(kk == total - 1)
        def _():
            r = acc_ref[...]
            if add is not None:
                r = r + refs[2 * npairs][...]
            out_ref[...] = r.astype(out_ref.dtype)

    return _pcall(body, grid=(M // tm, N // tn, total), in_specs=in_specs,
                  out_specs=pl.BlockSpec((tm, tn), lambda i, j, kk: (i, j)),
                  out_shape=jax.ShapeDtypeStruct((M, N), out_dtype),
                  scratch_shapes=[pltpu.VMEM((tm, tn), F32)],
                  compiler_params=_cparams(("parallel", "parallel", "arbitrary")), name=name)(*ins)


def _mm1(a, b, mode, out_dtype, name, **kw):
    k = a.shape[0] if mode == 'tn' else a.shape[1]
    return _mm([(a, b, 0, 0, k)], mode, out_dtype, name, **kw)


def _sb_tiles(T):
    bq = min(SB_BQ, T)
    bk = min(SB_BK, bq)
    return bq, bk, bq // bk


def _sb_block(i, j, masked, q, k_ref, v_ref, accL, bq, bk, UU):
    r = pl.ds(pl.multiple_of(j * bk, bk), bk)
    kj = k_ref[r, :]
    vj = v_ref[r, :]
    z = _dot(q, kj, NT) * (HD ** -0.5)
    lb = jnp.minimum(z, 0.0) - jnp.log(1.0 + jnp.exp(-jnp.abs(z)))
    ln = lb - z
    mask = None
    if masked:
        mask = (j * bk + _iota((bq, bk), 1)) < (i * bq + _iota((bq, bk), 0))
        ln = jnp.where(mask, ln, 0.0)
    hi = ln.astype(BF16)
    lo = (ln - hi.astype(F32)).astype(BF16)
    later = _dot(jnp.concatenate([hi, lo], axis=1), UU) + accL
    a = jnp.exp(lb + later)
    if masked:
        a = jnp.where(mask, a, 0.0)
    return r, kj, vj, mask, ln, lb, a


def _sb_fwd(proj, T, H):
    bq, bk, nd = _sb_tiles(T)

    assert T // bk <= HD

    def body(q_ref, kf_ref, vf_ref, o_ref, l_ref, k_ref, v_ref):
        i = pl.program_id(1)

        @pl.when(i == 0)
        def _():
            k_ref[...] = kf_ref[...].astype(BF16)
            v_ref[...] = vf_ref[...].astype(BF16)

        q = q_ref[...].astype(BF16)
        U = (_iota((bk, bk), 0) > _iota((bk, bk), 1)).astype(BF16)
        UU = jnp.concatenate([U, U], axis=0)
        lane = _iota((bq, HD), 1)

        def blk(j, carry, masked):
            acc, accL, saved = carry
            _, _, vj, _, ln, _, a = _sb_block(i, j, masked, q, k_ref, v_ref, accL, bq, bk, UU)
            acc = acc + _dot(a.astype(BF16), vj)
            return acc, accL + jnp.sum(ln, axis=1, keepdims=True), jnp.where(lane == j, accL, saved)

        carry = (jnp.zeros((bq, HD), F32), jnp.zeros((bq, 1), F32), jnp.zeros((bq, HD), F32))
        for jj in range(nd):
            carry = blk(i * nd + (nd - 1 - jj), carry, True)
        def group(it, c):
            for jj in range(nd):
                c = blk((i - 1 - it) * nd + (nd - 1 - jj), c, False)
            return c

        carry = lax.fori_loop(0, i, group, carry)
        o_ref[...] = carry[0]
        l_ref[...] = carry[2]

    blk_q = pl.BlockSpec((bq, HD), lambda h, i: (i, h))
    return _pcall(
        body, grid=(H, T // bq),
        in_specs=[blk_q, pl.BlockSpec((T, HD), lambda h, i: (0, H + h)),
                  pl.BlockSpec((T, HD), lambda h, i: (0, 2 * H + h))],
        out_specs=[blk_q, blk_q],
        out_shape=[jax.ShapeDtypeStruct((T, H * HD), F32)] * 2,
        scratch_shapes=[pltpu.VMEM((T, HD), BF16), pltpu.VMEM((T, HD), BF16)],
        compiler_params=_cparams(("parallel", "arbitrary")), name="sb_fwd")(proj, proj, proj)


def _sb_bwd(proj, l_sb, do_sb, T, H):
    bq, bk, nd = _sb_tiles(T)
    nq = T // bq

    def body(q_ref, kf_ref, vf_ref, l_ref, do_ref, dq_ref, dk_ref, dv_ref, dk_acc, dv_acc, k_ref, v_ref):
        i = pl.program_id(1)

        @pl.when(i == 0)
        def _():
            dk_acc[...] = jnp.zeros_like(dk_acc)
            dv_acc[...] = jnp.zeros_like(dv_acc)
            k_ref[...] = kf_ref[...].astype(BF16)
            v_ref[...] = vf_ref[...].astype(BF16)

        q = q_ref[...].astype(BF16)
        dob = do_ref[...].astype(BF16)
        saved = l_ref[...]
        lane = _iota((bq, HD), 1)
        U = (_iota((bk, bk), 0) > _iota((bk, bk), 1)).astype(BF16)
        UU = jnp.concatenate([U, U], axis=0)
        Ue = (_iota((bk, bk), 0) < _iota((bk, bk), 1)).astype(BF16)

        def blk(j, carry, masked):
            dq, accP = carry
            accL = jnp.sum(jnp.where(lane == j, saved, 0.0), axis=1, keepdims=True)
            r, kj, vj, mask, _, lb, a = _sb_block(i, j, masked, q, k_ref, v_ref, accL, bq, bk, UU)
            p = a * _dot(dob, vj, NT)
            pre = _dot(p.astype(BF16), Ue) + accP
            sig = jnp.exp(lb)
            dz = p * (1.0 - sig) - pre * sig
            if masked:
                dz = jnp.where(mask, dz, 0.0)
            dzb = (dz * (HD ** -0.5)).astype(BF16)
            dq = dq + _dot(dzb, kj)
            dk_acc[r, :] += _dot(dzb, q, TN)
            dv_acc[r, :] += _dot(a.astype(BF16), dob, TN)
            return dq, accP + jnp.sum(p, axis=1, keepdims=True)

        carry = (jnp.zeros((bq, HD), F32), jnp.zeros((bq, 1), F32))
        def group(it, c):
            for jj in range(nd):
                c = blk(it * nd + jj, c, False)
            return c

        carry = lax.fori_loop(0, i, group, carry)
        for jj in range(nd):
            carry = blk(i * nd + jj, carry, True)
        dq_ref[...] = carry[0].astype(BF16)

        @pl.when(i == nq - 1)
        def _():
            dk_ref[...] = dk_acc[...].astype(BF16)
            dv_ref[...] = dv_acc[...].astype(BF16)

    W = H * HD
    blk_q = pl.BlockSpec((bq, HD), lambda h, i: (i, h))
    blk_t = pl.BlockSpec((T, HD), lambda h, i: (0, h))
    return _pcall(
        body, grid=(H, nq),
        in_specs=[blk_q, pl.BlockSpec((T, HD), lambda h, i: (0, H + h)),
                  pl.BlockSpec((T, HD), lambda h, i: (0, 2 * H + h)), blk_q, blk_q],
        out_specs=[blk_q, blk_t, blk_t],
        out_shape=[jax.ShapeDtypeStruct((T, W), BF16)] * 3,
        scratch_shapes=[pltpu.VMEM((T, HD), F32), pltpu.VMEM((T, HD), F32),
                        pltpu.VMEM((T, HD), BF16), pltpu.VMEM((T, HD), BF16)],
        compiler_params=_cparams(("parallel", "arbitrary")), name="sb_bwd")(proj, proj, proj, l_sb, do_sb)


@jax.custom_vjp
def _unit_lower_inv(lw):
    C = lw.shape[1]
    x = (_iota(lw.shape, 1) == _iota(lw.shape, 2)).astype(F32) - lw
    pw = lw
    for _ in range(C.bit_length() - 2):
        pw = _dotf(pw, pw, NNB)
        x = x + _dotf(x, pw, NNB)
    return x


_unit_lower_inv.defvjp(lambda lw: (_unit_lower_inv(lw),) * 2,
                       lambda x, g: (-_dotf(_dotf(x, g, TNB), x, NTB),))


def _gdn_chunk(q, k, v, bb, gb, S):
    G, C = q.shape[0], q.shape[1]
    ri, ci = _iota((G, C, C), 1), _iota((G, C, C), 2)
    tril, strict = ri >= ci, ri > ci
    g_col = gb[:, :, :C]
    g_row = jnp.swapaxes(g_col, 1, 2)
    decay = jnp.where(tril, jnp.exp(jnp.where(tril, g_col - g_row, 0.0)), 0.0)
    eg = jnp.exp(gb)
    qs = q * (HD ** -0.5)
    kb = k * bb
    lw = jnp.where(strict, _bdot_nt(kb, k) * decay, 0.0)
    x = _unit_lower_inv(lw)
    u = _bdot_nn(x, v * bb)
    w = _bdot_nn(x, kb * eg)
    aq = jnp.where(tril, _bdot_nt(qs, k) * decay, 0.0)
    vnew = u - _bdot_nn(w, S)
    o = _bdot_nn(qs * eg, S) + _bdot_nn(aq, vnew)
    g_last = gb[:, C - 1:C, :]
    s_new = S * jnp.exp(g_last) + _bdot_tn(k * jnp.exp(g_last - gb), vnew)
    return o, s_new


def _gdn_fwd(qn, kn, vc, beta_b, g_b, T, H):
    tb = min(GDN_TB, T)
    nc = tb // CH
    G = min(GDN_G, H)

    def body(q_ref, k_ref, v_ref, b_ref, g_ref, o_ref, s_ref, s_scr):
        @pl.when(pl.program_id(1) == 0)
        def _():
            s_scr[...] = jnp.zeros_like(s_scr)

        def step(c, carry):
            r = pl.ds(pl.multiple_of(c * CH, CH), CH)
            s = s_scr[...]
            s_ref[:, c] = s
            o, s2 = _gdn_chunk(q_ref[:, r, :], k_ref[:, r, :], v_ref[:, r, :], b_ref[:, r, :], g_ref[:, r, :], s)
            o_ref[:, r, :] = o
            s_scr[...] = s2
            return carry

        lax.fori_loop(0, nc, step, 0)

    blk3 = pl.BlockSpec((G, tb, HD), lambda h, t: (h, t, 0))
    return _pcall(
        body, grid=(H // G, T // tb), in_specs=[blk3] * 5,
        out_specs=[blk3, pl.BlockSpec((G, nc, HD, HD), lambda h, t: (h, t, 0, 0))],
        out_shape=[jax.ShapeDtypeStruct((H, T, HD), F32), jax.ShapeDtypeStruct((H, T // CH, HD, HD), F32)],
        scratch_shapes=[pltpu.VMEM((G, HD, HD), F32)],
        compiler_params=_cparams(("parallel", "arbitrary")), name="gdn_fwd")(qn, kn, vc, beta_b, g_b)


def _gdn_bwd(qn, kn, vc, beta_b, g_b, s_all, do, T, H):
    tb = min(GDN_TB, T)
    nc = tb // CH
    nt = T // tb
    G = min(GDN_G, H)

    def body(q_ref, k_ref, v_ref, b_ref, g_ref, s_ref, do_ref, dq_ref, dk_ref, dv_ref, db_ref, dg_ref, ds_scr):
        @pl.when(pl.program_id(1) == 0)
        def _():
            ds_scr[...] = jnp.zeros_like(ds_scr)

        def step(it, carry):
            c = nc - 1 - it
            r = pl.ds(pl.multiple_of(c * CH, CH), CH)
            _, vjp = jax.vjp(_gdn_chunk, q_ref[:, r, :], k_ref[:, r, :], v_ref[:, r, :], b_ref[:, r, :],
                             g_ref[:, r, :], s_ref[:, c])
            dq, dk, dv, db, dg, ds = vjp((do_ref[:, r, :], ds_scr[...]))
            dq_ref[:, r, :] = dq
            dk_ref[:, r, :] = dk
            dv_ref[:, r, :] = dv
            db_ref[:, r, :] = db
            dg_ref[:, r, :] = dg
            ds_scr[...] = ds
            return carry

        lax.fori_loop(0, nc, step, 0)

    blk3 = pl.BlockSpec((G, tb, HD), lambda h, t: (h, nt - 1 - t, 0))
    return _pcall(
        body, grid=(H // G, nt),
        in_specs=[blk3] * 5 + [pl.BlockSpec((G, nc, HD, HD), lambda h, t: (h, nt - 1 - t, 0, 0)), blk3],
        out_specs=[blk3] * 5,
        out_shape=[jax.ShapeDtypeStruct((H, T, HD), F32)] * 5,
        scratch_shapes=[pltpu.VMEM((G, HD, HD), F32)],
        compiler_params=_cparams(("parallel", "arbitrary")), name="gdn_bwd")(qn, kn, vc, beta_b, g_b, s_all, do)


def _conv_tiles(T, C):
    return min(HEAD_TM, T), min(CONV_CW, C)


def _shift_down(main, halo, first, d):
    halo = jnp.where(first, 0.0, halo)
    ext = jnp.concatenate([halo, main], axis=0)
    return pltpu.roll(ext, d, 0)[8:]


def _conv_fwd(proj, conv_w, T, H):
    W = H * HD
    C = 3 * W
    tm, cw = _conv_tiles(T, C)
    col0 = 4 * W // cw

    def body(x_ref, h_ref, w_ref, o_ref):
        first = pl.program_id(1) == 0
        main = x_ref[...]
        halo = h_ref[...]
        w = w_ref[...]
        out = w[CONV_K - 1:CONV_K, :] * main
        for d in range(1, CONV_K):
            out = out + w[CONV_K - 1 - d:CONV_K - d, :] * _shift_down(main, halo, first, d)
        o_ref[...] = out

    return _pcall(
        body, grid=(C // cw, T // tm),
        in_specs=[pl.BlockSpec((tm, cw), lambda cb, i: (i, col0 + cb)),
                  pl.BlockSpec((8, cw), lambda cb, i: (jnp.maximum(i * (tm // 8) - 1, 0), col0 + cb)),
                  pl.BlockSpec((CONV_K, cw), lambda cb, i: (0, cb))],
        out_specs=pl.BlockSpec((tm, cw), lambda cb, i: (i, cb)),
        out_shape=jax.ShapeDtypeStruct((T, C), F32),
        compiler_params=_cparams(("parallel", "arbitrary")), name="conv_fwd")(proj, proj, conv_w)


def _conv_bwd(proj, conv_w, dconv, T, H):
    W = H * HD
    C = 3 * W
    tm, cw = _conv_tiles(T, C)
    col0 = 4 * W // cw
    nt = T // tm

    def body(x_ref, h_ref, w_ref, d_ref, dn_ref, dx_ref, dw_ref):
        i = pl.program_id(1)
        first = i == 0
        main = x_ref[...]
        halo = h_ref[...]
        w = w_ref[...]
        dmain = d_ref[...]
        dnext = jnp.where(i == nt - 1, 0.0, dn_ref[...])
        dext = jnp.concatenate([dmain, dnext], axis=0)
        dx = w[CONV_K - 1:CONV_K, :] * dmain
        rows = [jnp.sum(dmain * main, axis=0, keepdims=True)]
        for d in range(1, CONV_K):
            dx = dx + w[CONV_K - 1 - d:CONV_K - d, :] * pltpu.roll(dext, tm + 8 - d, 0)[:tm]
            rows.append(jnp.sum(dmain * _shift_down(main, halo, first, d), axis=0, keepdims=True))
        dx_ref[...] = dx.astype(BF16)

        @pl.when(first)
        def _():
            dw_ref[...] = jnp.zeros_like(dw_ref)

        for d in range(CONV_K):
            dw_ref[CONV_K - 1 - d:CONV_K - d, :] += rows[d]

    return _pcall(
        body, grid=(C // cw, nt),
        in_specs=[pl.BlockSpec((tm, cw), lambda cb, i: (i, col0 + cb)),
                  pl.BlockSpec((8, cw), lambda cb, i: (jnp.maximum(i * (tm // 8) - 1, 0), col0 + cb)),
                  pl.BlockSpec((CONV_K, cw), lambda cb, i: (0, cb)),
                  pl.BlockSpec((tm, cw), lambda cb, i: (i, cb)),
                  pl.BlockSpec((8, cw), lambda cb, i: (jnp.minimum((i + 1) * (tm // 8), T // 8 - 1), cb))],
        out_specs=[pl.BlockSpec((tm, cw), lambda cb, i: (i, cb)), pl.BlockSpec((CONV_K, cw), lambda cb, i: (0, cb))],
        out_shape=[jax.ShapeDtypeStruct((T, C), BF16), jax.ShapeDtypeStruct((CONV_K, C), F32)],
        compiler_params=_cparams(("parallel", "arbitrary")), name="conv_bwd")(proj, proj, conv_w, dconv, dconv)


def _prep_fn(H, tm):
    def fn(cq, ck, cv, small, al, dtb):
        h = pl.program_id(1)
        lane = _iota((HD, HD), 0)
        oh_b = (lane == h).astype(F32)
        oh_a = (lane == H + h).astype(F32)
        q, k, v = _silu(cq), _silu(ck), _silu(cv)
        qn = q * lax.rsqrt(jnp.sum(q * q, axis=-1, keepdims=True) + L2_EPS)
        kn = k * lax.rsqrt(jnp.sum(k * k, axis=-1, keepdims=True) + L2_EPS)
        beta = _dotf(_sigmoid(small), oh_b)
        g = _dotf(-jnp.exp(al) * _softplus(small + dtb), oh_a)
        nch = tm // CH
        tri = (_iota((nch, CH, CH), 1) >= _iota((nch, CH, CH), 2)).astype(F32)
        gc = _dotf(tri, g.reshape(nch, CH, HD), NNB).reshape(tm, HD)
        return qn, kn, v, beta, gc
    return fn


def _prep_specs(T, H):
    tm = min(HEAD_TM, T)
    W = H * HD
    row = lambda s: pl.BlockSpec((tm, HD), lambda i, h, s=s: (i, s * H + h))
    small = pl.BlockSpec((tm, HD), lambda i, h: (i, 0))
    par = pl.BlockSpec((1, HD), lambda i, h: (0, 0))
    blk = pl.BlockSpec((tm, HD), lambda i, h: (i, h))
    blk3 = pl.BlockSpec((None, tm, HD), lambda i, h: (h, i, 0))
    return tm, W, row, small, par, blk, blk3


def _prep_fwd(conv, small, al, dtb, T, H):
    tm, W, row, sm, par, blk, blk3 = _prep_specs(T, H)
    return _ew(_prep_fn(H, tm), (T // tm, H), [conv, conv, conv, small, al, dtb],
               [row(0), row(1), row(2), sm, par, par],
               [((H, T, HD), F32)] * 5, [blk3] * 5, name="gdn_prep")


def _prep_bwd(conv, small, al, dtb, dqn, dkn, dvc, dbeta, dg, T, H):
    tm, W, row, sm, par, blk, blk3 = _prep_specs(T, H)
    f = _prep_fn(H, tm)

    def fn(cq, ck, cv, small_, al_, dtb_, a, b, c, d, e):
        _, vjp = jax.vjp(f, cq, ck, cv, small_, al_, dtb_)
        return vjp((a, b, c, d, e))

    return _ew(fn, (T // tm, H), [conv, conv, conv, small, al, dtb, dqn, dkn, dvc, dbeta, dg],
               [row(0), row(1), row(2), sm, par, par] + [blk3] * 5,
               [((T, W), F32)] * 3 + [((T, HD), F32), ((1, HD), F32), ((1, HD), F32)],
               [blk, blk, blk, sm, par, par], acc=[None, None, None, 'inner', 'all', 'all'], name="gdn_prep_bwd")


def _gate_fn(o_sb, z_sb, o_g, z_g, gnw):
    a_sb = o_sb * _silu(z_sb)
    a_g = _rms(o_g, gnw) * _silu(z_g)
    return a_sb, a_g


def _gate_specs(T, H):
    tm = min(HEAD_TM, T)
    blk = pl.BlockSpec((tm, HD), lambda i, h: (i, h))
    sec = lambda s: pl.BlockSpec((tm, HD), lambda i, h, s=s: (i, s * H + h))
    par = pl.BlockSpec((1, HD), lambda i, h: (0, 0))
    blk3 = pl.BlockSpec((None, tm, HD), lambda i, h: (h, i, 0))
    return tm, blk, blk3, sec, par


def _gate_fwd(o_sb, proj, o_g, gnw, T, H):
    tm, blk, blk3, sec, par = _gate_specs(T, H)
    W = H * HD
    return _ew(_gate_fn, (T // tm, H), [o_sb, proj, o_g, proj, gnw], [blk, sec(3), blk3, sec(7), par],
               [((T, W), BF16)] * 2, [blk, blk], name="gate_fwd")


def _gate_bwd(o_sb, proj, o_g, gnw, da_sb, da_g, T, H):
    tm, blk, blk3, sec, par = _gate_specs(T, H)
    W = H * HD

    def fn(o_sb_, z_sb, o_g_, z_g, gnw_, da, db):
        _, vjp = jax.vjp(_gate_fn, o_sb_, z_sb, o_g_, z_g, gnw_)
        return vjp((da, db))

    return _ew(fn, (T // tm, H), [o_sb, proj, o_g, proj, gnw, da_sb, da_g], [blk, sec(3), blk3, sec(7), par, blk, blk],
               [((T, W), F32), ((T, W), BF16), ((H, T, HD), F32), ((T, W), BF16), ((1, HD), F32)],
               [blk, blk, blk3, blk, par], acc=[None, None, None, None, 'all'], name="gate_bwd")


def _ada_fn(x, nw, scale, shift):
    return _rms(x, nw) * (1.0 + scale) + shift


def _row_specs(T, D):
    tm = min(EW_TM, T)
    return tm, pl.BlockSpec((tm, D), lambda i: (i, 0)), pl.BlockSpec((1, D), lambda i: (0, 0))


def _ada_fwd(x, nw, scale, shift):
    T, D = x.shape
    tm, row, par = _row_specs(T, D)
    return _ew(_ada_fn, (T // tm,), [x, nw, scale, shift], [row, par, par, par], [((T, D), BF16)], [row],
               name="ada_norm")[0]


def _ada_bwd(x, nw, scale, shift, dh, dx2):
    T, D = x.shape
    tm, row, par = _row_specs(T, D)

    def fn(x_, nw_, sc_, sh_, dh_, dx2_):
        _, vjp = jax.vjp(_ada_fn, x_, nw_, sc_, sh_)
        dx, dnw, dsc, dsh = vjp(dh_)
        return dx + dx2_, dnw, dsc, dsh

    return _ew(fn, (T // tm,), [x, nw, scale, shift, dh, dx2], [row, par, par, par, row, row],
               [((T, D), F32)] + [((1, D), F32)] * 3, [row, par, par, par], acc=[None, 'all', 'all', 'all'],
               name="ada_norm_bwd")


def _merge_fn(m_sb, m_g, p_sb, p_g):
    return _sigmoid(m_sb) * p_sb + _sigmoid(m_g) * p_g


def _merge_specs(T, D, H):
    tm, tc = min(HEAD_TM, T), min(512, D)
    nb = D // tc
    blk = pl.BlockSpec((tm, tc), lambda i, j: (i, j))
    sec = lambda s: pl.BlockSpec((tm, tc), lambda i, j, s=s: (i, s * nb + j))
    return tm, tc, blk, sec


def _merge_fwd(proj, p_sb, p_g, T, D, H):
    tm, tc, blk, sec = _merge_specs(T, D, H)
    return _ew(_merge_fn, (T // tm, D // tc), [proj, proj, p_sb, p_g], [sec(8), sec(9), blk, blk],
               [((T, D), BF16)], [blk], name="merge")[0]


def _merge_bwd(proj, p_sb, p_g, dy, T, D, H):
    tm, tc, blk, sec = _merge_specs(T, D, H)

    def fn(m_sb, m_g, p_sb_, p_g_, dy_):
        _, vjp = jax.vjp(_merge_fn, m_sb, m_g, p_sb_, p_g_)
        return vjp(dy_)

    return _ew(fn, (T // tm, D // tc), [proj, proj, p_sb, p_g, dy], [sec(8), sec(9), blk, blk, blk],
               [((T, D), BF16)] * 4, [blk] * 4, name="merge_bwd")


def _loss_head(x, u, gate, fnw, tgt):
    T, D = x.shape
    tm, row, par = _row_specs(T, D)

    def loss(x_, u_, gate_, fnw_, tgt_):
        y = _rms(x_ + gate_ * u_, fnw_)
        return 0.5 * jnp.sum(jnp.mean(jnp.square(y - tgt_), axis=-1))

    def fn(x_, u_, gate_, fnw_, tgt_):
        val, (dx, du, dgate, dfnw) = jax.value_and_grad(loss, argnums=(0, 1, 2, 3))(x_, u_, gate_, fnw_, tgt_)
        return jnp.full((1, HD), val, F32), dx, du, dgate, dfnw

    return _ew(fn, (T // tm,), [x, u, gate, fnw, tgt], [row, row, par, par, row],
               [((1, HD), F32), ((T, D), F32), ((T, D), BF16), ((1, D), F32), ((1, D), F32)],
               [pl.BlockSpec((1, HD), lambda i: (0, 0)), row, row, par, par],
               acc=['all', None, None, 'all', 'all'], name="loss_head")


def _mod_part(c_all, w_ada, b_ada):
    D, N = w_ada.shape
    tn = min(512, N)

    def fn(c, w, b):
        return _dot(_silu(c).astype(BF16), w.astype(BF16)) + b

    return _ew(fn, (N // tn,), [c_all, w_ada, b_ada],
               [pl.BlockSpec((8, D), lambda j: (0, 0)), pl.BlockSpec((D, tn), lambda j: (0, j)),
                pl.BlockSpec((1, tn), lambda j: (0, j))],
               [((8, N), F32)], [pl.BlockSpec((8, tn), lambda j: (0, j))], name="ada_mod")[0]


def _w_ada_grad(c_all, dmod):
    D, N = c_all.shape[1], dmod.shape[1]
    tn = min(512, N)

    def fn(c, dm):
        return _dot(_silu(c).astype(BF16), dm.astype(BF16), TN)

    return _ew(fn, (N // tn,), [c_all, dmod],
               [pl.BlockSpec((8, D), lambda j: (0, 0)), pl.BlockSpec((8, tn), lambda j: (0, j))],
               [((D, N), F32)], [pl.BlockSpec((D, tn), lambda j: (0, j))], name="w_ada_grad")[0]


def _sum8(packs):
    N = packs.shape[1]

    def fn(p):
        s = p[0:8]
        for d in range(1, 8):
            s = s + p[8 * d:8 * d + 8]
        return s

    return _ew(fn, (1,), [packs], [pl.BlockSpec((64, N), lambda i: (0, 0))], [((8, N), F32)],
               [pl.BlockSpec((8, N), lambda i: (0, 0))], name="sum_devices")[0]


def _sum_chips(q, c, name):
    _, R, C = q.shape
    tr = min(64, R)

    def body(c_ref, q_ref, o_ref):
        p = q_ref[...].astype(F32)
        o_ref[...] = (p[0] + p[1]) + (p[2] + p[3])

    grid_spec = pltpu.PrefetchScalarGridSpec(
        num_scalar_prefetch=1, grid=(R // tr,),
        in_specs=[pl.BlockSpec((4, tr, C), lambda i, c_ref: (0, i, 0))],
        out_specs=pl.BlockSpec((None, tr, C), lambda i, c_ref: (c_ref[0], i, 0)))
    return _pcall(body, grid_spec=grid_spec, out_shape=jax.ShapeDtypeStruct((2, R, C), F32),
                  compiler_params=_cparams(("arbitrary",)), name=name)(jnp.reshape(c, (1,)).astype(jnp.int32), q)


def _add_halves(g, rsib, c, name):
    _, R, C = g.shape
    rh = R // 2
    tr = min(64, rh)
    nb = rh // tr

    def body(c_ref, a_ref, b_ref, o_ref, o2_ref):
        v = (a_ref[...] + b_ref[...]).astype(o_ref.dtype)
        o_ref[...] = v
        o2_ref[...] = v

    spec = pl.BlockSpec((None, tr, C), lambda j, i, c_ref: (j, i, 0))
    grid_spec = pltpu.PrefetchScalarGridSpec(
        num_scalar_prefetch=1, grid=(4, nb),
        in_specs=[pl.BlockSpec((None, tr, C), lambda j, i, c_ref: (j, c_ref[0] * nb + i, 0)), spec],
        out_specs=[spec, spec])
    return _pcall(body, grid_spec=grid_spec, out_shape=[jax.ShapeDtypeStruct((4, rh, C), BF16)] * 2,
                  compiler_params=_cparams(("arbitrary", "arbitrary")), name=name)(
                      jnp.reshape(c, (1,)).astype(jnp.int32), g, rsib)


def _adamw(w, g, m, v, name):
    R, C = w.shape
    tr = R if R <= 64 else 64
    blk = pl.BlockSpec((tr, C), lambda i: (i, 0))

    def fn(w_, g_, m_, v_):
        m2 = ADAM_B1 * m_ + (1.0 - ADAM_B1) * g_
        v2 = ADAM_B2 * v_ + (1.0 - ADAM_B2) * jnp.square(g_)
        m_hat = m2 / (1.0 - ADAM_B1 ** ADAM_STEP)
        v_hat = v2 / (1.0 - ADAM_B2 ** ADAM_STEP)
        delta = -ADAM_LR * (m_hat / (jnp.sqrt(v_hat) + ADAM_EPS) + ADAM_WD * w_)
        return delta, m2, v2

    return _ew(fn, (R // tr,), [w, g, m, v], [blk] * 4, [((R, C), F32)] * 3, [blk] * 3, name=name)


def _place():
    x, y, c = lax.axis_index("x"), lax.axis_index("y"), lax.axis_index("c")
    return x, y, c, [(1 - x, y), (x, 1 - y), (1 - x, 1 - y)]


ANY = pl.BlockSpec(memory_space=pl.ANY)


def _allgather8(blk):
    m_per, n = blk.shape

    def body(x_ref, out_ref, send_sems, recv_sems, local_sem):
        x, y, c, chips = _place()
        me, sibling = (x, y, c), (x, y, 1 - c)

        def rows(px, py, pc):
            return out_ref.at[pl.ds((4 * px + 2 * py + pc) * m_per, m_per), :]

        def copy(k, block, to, src=None):
            return pltpu.make_async_remote_copy(
                src_ref=rows(*block) if src is None else src, dst_ref=rows(*block),
                send_sem=send_sems.at[k], recv_sem=recv_sems.at[k], device_id=to, device_id_type=MESH)

        mine = pltpu.make_async_copy(x_ref, rows(*me), local_sem)
        mine.start()
        first = [copy(0, me, sibling, src=x_ref)]
        first += [copy(1 + j, me, (*chip, c), src=x_ref) for j, chip in enumerate(chips)]
        for cp in first:
            cp.start()
        passed = [copy(4 + j, (*chip, c), sibling) for j, chip in enumerate(chips)]
        for j, chip in enumerate(chips):
            copy(1 + j, (*chip, c), me).wait_recv()
            passed[j].start()
        copy(0, sibling, me).wait_recv()
        for j, chip in enumerate(chips):
            copy(4 + j, (*chip, 1 - c), me).wait_recv()
        for cp in first + passed:
            cp.wait_send()
        mine.wait()

    vm = pl.BlockSpec(memory_space=pltpu.VMEM)
    return _pcall(body, out_shape=jax.ShapeDtypeStruct((8 * m_per, n), blk.dtype), in_specs=[vm], out_specs=vm,
                  scratch_shapes=[pltpu.SemaphoreType.DMA((7,)), pltpu.SemaphoreType.DMA((7,)),
                                  pltpu.SemaphoreType.DMA],
                  compiler_params=pltpu.CompilerParams(vmem_limit_bytes=VMEM_LIMIT), name="allgather8")(blk)


def _gather_weights(shards):
    n = len(shards)

    def body(*refs):
        ins, outs = refs[:n], refs[n:2 * n]
        send_sems, recv_sems, fsend_sems, frecv_sems = refs[2 * n:]
        x, y, c, chips = _place()
        mychip = 2 * x + y
        sends, fwds = [], []
        for a in range(n):
            rh = shards[a].shape[0] // 2
            for k, chip in enumerate(chips):
                cp = pltpu.make_async_remote_copy(
                    src_ref=ins[a].at[pl.ds(c * rh, rh), :], dst_ref=outs[a].at[mychip, pl.ds(c * rh, rh), :],
                    send_sem=send_sems.at[3 * a + k], recv_sem=recv_sems.at[3 * a + k],
                    device_id=(*chip, c), device_id_type=MESH)
                cp.start()
                sends.append(cp)
        for a in range(n):
            rh = shards[a].shape[0] // 2
            for k, (px, py) in enumerate(chips):
                land = outs[a].at[2 * px + py, pl.ds(c * rh, rh), :]
                pltpu.make_async_remote_copy(
                    src_ref=land, dst_ref=land, send_sem=send_sems.at[3 * a + k], recv_sem=recv_sems.at[3 * a + k],
                    device_id=(px, py, c), device_id_type=MESH).wait_recv()
                fw = pltpu.make_async_remote_copy(
                    src_ref=land, dst_ref=land, send_sem=fsend_sems.at[3 * a + k], recv_sem=frecv_sems.at[3 * a + k],
                    device_id=(x, y, 1 - c), device_id_type=MESH)
                fw.start()
                fwds.append(fw)
        for a in range(n):
            rh = shards[a].shape[0] // 2
            for k, (px, py) in enumerate(chips):
                land = outs[a].at[2 * px + py, pl.ds((1 - c) * rh, rh), :]
                pltpu.make_async_remote_copy(
                    src_ref=land, dst_ref=land, send_sem=fsend_sems.at[3 * a + k], recv_sem=frecv_sems.at[3 * a + k],
                    device_id=(x, y, 1 - c), device_id_type=MESH).wait_recv()
        for cp in sends + fwds:
            cp.wait_send()

    return _pcall(body, out_shape=[jax.ShapeDtypeStruct((4,) + s.shape, s.dtype) for s in shards],
                  in_specs=[ANY] * n, out_specs=[ANY] * n,
                  scratch_shapes=[pltpu.SemaphoreType.DMA((3 * n,))] * 4, name="gather_weights")(*shards)


def _row_chunks(rows, n):
    while n > 1 and rows % (8 * n):
        n //= 2
    return [(k * (rows // n), rows // n) for k in range(n)]


def _swap_partial_halves(gs):
    n = len(gs)
    plan = [(a, j, r0, nr) for a in range(n) for j in range(4)
            for r0, nr in _row_chunks(gs[a].shape[1] // 2, SWAP_CHUNKS)]

    def body(*refs):
        ins, outs = refs[:n], refs[n:2 * n]
        send_sems, recv_sems = refs[2 * n:]
        x, y, c, _ = _place()
        cps = []
        for k, (a, j, r0, nr) in enumerate(plan):
            rh = gs[a].shape[1] // 2
            cp = pltpu.make_async_remote_copy(
                src_ref=ins[a].at[j, pl.ds((1 - c) * rh + r0, nr), :], dst_ref=outs[a].at[j, pl.ds(r0, nr), :],
                send_sem=send_sems.at[k], recv_sem=recv_sems.at[k], device_id=(x, y, 1 - c), device_id_type=MESH)
            cp.start()
            cps.append(cp)
        for cp in cps:
            cp.wait()

    half = [jax.ShapeDtypeStruct((4, g.shape[1] // 2, g.shape[2]), g.dtype) for g in gs]
    return _pcall(body, out_shape=half, in_specs=[ANY] * n, out_specs=[ANY] * n,
                  scratch_shapes=[pltpu.SemaphoreType.DMA((len(plan),))] * 2, name="swap_partial_halves")(*gs)


def _chip_scatter(ps, lands):
    n = len(ps)

    def body(*refs):
        ins, outs = refs[:n], refs[2 * n:3 * n]
        send_sems, recv_sems = refs[3 * n:]
        x, y, c, chips = _place()
        mychip = 2 * x + y
        cps = []
        for a in range(n):
            for k, (px, py) in enumerate(chips):
                cp = pltpu.make_async_remote_copy(
                    src_ref=ins[a].at[2 * px + py], dst_ref=outs[a].at[mychip],
                    send_sem=send_sems.at[3 * a + k], recv_sem=recv_sems.at[3 * a + k],
                    device_id=(px, py, c), device_id_type=MESH)
                cp.start()
                cps.append(cp)
        for a in range(n):
            for k, (px, py) in enumerate(chips):
                land = outs[a].at[2 * px + py]
                pltpu.make_async_remote_copy(
                    src_ref=land, dst_ref=land, send_sem=send_sems.at[3 * a + k], recv_sem=recv_sems.at[3 * a + k],
                    device_id=(px, py, c), device_id_type=MESH).wait_recv()
        for cp in cps:
            cp.wait_send()

    return _pcall(body, out_shape=[jax.ShapeDtypeStruct(p.shape, p.dtype) for p in ps],
                  in_specs=[ANY] * (2 * n), out_specs=[ANY] * n,
                  input_output_aliases={n + a: a for a in range(n)},
                  scratch_shapes=[pltpu.SemaphoreType.DMA((3 * n,))] * 2, name="chip_scatter")(*ps, *lands)


def _swap_final_halves(hs):
    n = len(hs)
    plan = [(a, r0, nr) for a in range(n) for r0, nr in _row_chunks(hs[a].shape[1], 2 * SWAP_CHUNKS)]

    def body(*refs):
        outs = refs[n:2 * n]
        send_sems, recv_sems = refs[2 * n:]
        x, y, c, _ = _place()
        cps = []
        for k, (a, r0, nr) in enumerate(plan):
            mine = outs[a].at[c, pl.ds(r0, nr), :]
            cp = pltpu.make_async_remote_copy(
                src_ref=mine, dst_ref=mine, send_sem=send_sems.at[k], recv_sem=recv_sems.at[k],
                device_id=(x, y, 1 - c), device_id_type=MESH)
            cp.start()
            cps.append(cp)
        for k, (a, r0, nr) in enumerate(plan):
            land = outs[a].at[1 - c, pl.ds(r0, nr), :]
            pltpu.make_async_remote_copy(
                src_ref=land, dst_ref=land, send_sem=send_sems.at[k], recv_sem=recv_sems.at[k],
                device_id=(x, y, 1 - c), device_id_type=MESH).wait_recv()
        for cp in cps:
            cp.wait_send()

    return _pcall(body, out_shape=[jax.ShapeDtypeStruct(h.shape, h.dtype) for h in hs],
                  in_specs=[ANY] * n, out_specs=[ANY] * n, input_output_aliases={a: a for a in range(n)},
                  scratch_shapes=[pltpu.SemaphoreType.DMA((len(plan),))] * 2, name="swap_final_halves")(*hs)


def _pad_cols(a, n):
    return jnp.pad(a, ((0, 0), (0, n - a.shape[1])))


def kernel(x, c, w_ada, b_ada, norm_w, w_in, gdn_conv_w, gdn_a_log, gdn_dt_bias, gdn_norm_w, w_proj_sb, w_proj_gdn, w_out, final_norm_w, loss_target, m_w_ada, m_b_ada, m_norm_w, m_w_in, m_gdn_conv_w, m_gdn_a_log, m_gdn_dt_bias, m_gdn_norm_w, m_w_proj_sb, m_w_proj_gdn, m_w_out, m_final_norm_w, v_w_ada, v_b_ada, v_norm_w, v_w_in, v_gdn_conv_w, v_gdn_a_log, v_gdn_dt_bias, v_gdn_norm_w, v_w_proj_sb, v_w_proj_gdn, v_w_out, v_final_norm_w):
    T, D = x.shape[1], x.shape[2]
    H = gdn_a_log.shape[1]
    W = H * HD
    assert W == D and T % CH == 0
    NA = w_ada.shape[2]
    NI = w_in.shape[2]
    CW = gdn_conv_w.shape[2]
    px, py, pc = lax.axis_index("x"), lax.axis_index("y"), lax.axis_index("c")
    chip = 2 * px + py
    me = 2 * chip + pc
    x2d, tgt = x[0], loss_target[0]
    PW = 3 * D

    pack1 = jnp.concatenate([c, _pad_cols(gdn_conv_w[0], D), jnp.zeros((3, D), F32)], axis=0)
    got1 = _allgather8(pack1).reshape(8, 8, D)
    c_all = got1[:, 0, :]
    conv_w = jnp.concatenate([got1[2 * j, 1:1 + CONV_K, :CW] for j in range(4)], axis=1)

    b_shard = lax.dynamic_slice_in_dim(b_ada, chip * NA, NA, axis=1)
    mod_part = _mod_part(c_all, w_ada[0], b_shard)
    got2 = _allgather8(mod_part).reshape(8, 8, NA)
    mod = jnp.concatenate([lax.dynamic_index_in_dim(got2[2 * j], me, 0) for j in range(4)], axis=1)
    shift, scale, gate = mod[:, :D], mod[:, D:2 * D], mod[:, 2 * D:]

    own = [w_in[0].astype(BF16), w_proj_sb[0].astype(BF16), w_proj_gdn[0].astype(BF16), w_out[0].astype(BF16)]
    is_own = (jnp.arange(4) == chip)[:, None, None]
    wg = [jnp.where(is_own, o[None], g) for o, g in zip(own, _gather_weights(own))]
    w_in_full = jnp.concatenate([wg[0][j] for j in range(4)], axis=1)
    w_big = jnp.concatenate([w_in_full[:, :8 * W], w_in_full[:, 8 * W + 2 * H:]], axis=1)
    w_small = _pad_cols(w_in_full[:, 8 * W:8 * W + 2 * H], HD)
    w_psb, w_pg, w_o = (wg[i].reshape(D, D) for i in (1, 2, 3))

    h = _ada_fwd(x2d, norm_w, scale, shift)
    proj = _mm1(h, w_big, 'nn', F32, "proj_big")
    small = _mm1(h, w_small, 'nn', F32, "proj_small")
    o_sb, l_sb = _sb_fwd(proj, T, H)
    conv = _conv_fwd(proj, conv_w, T, H)
    al = jnp.pad(gdn_a_log, ((0, 0), (H, HD - 2 * H)))
    dtb = jnp.pad(gdn_dt_bias, ((0, 0), (H, HD - 2 * H)))
    qn, kn, vc, beta_b, g_b = _prep_fwd(conv, small, al, dtb, T, H)
    o_g, s_all = _gdn_fwd(qn, kn, vc, beta_b, g_b, T, H)
    a_sb, a_g = _gate_fwd(o_sb, proj, o_g, gdn_norm_w, T, H)
    p_sb = _mm1(a_sb, w_psb, 'nn', F32, "proj_sb")
    p_g = _mm1(a_g, w_pg, 'nn', F32, "proj_gdn")
    y = _merge_fwd(proj, p_sb, p_g, T, D, H)
    u = _mm1(y, w_o, 'nn', F32, "proj_out")
    loss_p, dx2, du, dgate, g_fnw = _loss_head(x2d, u, gate, final_norm_w.reshape(1, D), tgt)

    dy = _mm1(du, w_o, 'nt', F32, "d_merge")
    g_w_out = _mm1(y, du, 'tn', F32, "g_w_out")
    dm_sb, dm_g, dp_sb, dp_g = _merge_bwd(proj, p_sb, p_g, dy, T, D, H)
    da_sb = _mm1(dp_sb, w_psb, 'nt', F32, "d_a_sb")
    g_w_psb = _mm1(a_sb, dp_sb, 'tn', F32, "g_w_proj_sb")
    da_g = _mm1(dp_g, w_pg, 'nt', F32, "d_a_gdn")
    g_w_pg = _mm1(a_g, dp_g, 'tn', F32, "g_w_proj_gdn")
    do_sb, dz_sb, do_g, dz_g, g_gnw = _gate_bwd(o_sb, proj, o_g, gdn_norm_w, da_sb, da_g, T, H)
    dq_sb, dk_sb, dv_sb = _sb_bwd(proj, l_sb, do_sb, T, H)
    dqn, dkn, dvc, dbeta_b, dg_b = _gdn_bwd(qn, kn, vc, beta_b, g_b, s_all, do_g, T, H)
    dcq, dck, dcv, dsmall, dal, ddtb = _prep_bwd(conv, small, al, dtb, dqn, dkn, dvc, dbeta_b, dg_b, T, H)
    dpre, g_conv = _conv_bwd(proj, conv_w, jnp.concatenate([dcq, dck, dcv], axis=1), T, H)

    secs = [(dq_sb, 0), (dk_sb, W), (dv_sb, 2 * W), (dz_sb, 3 * W), (dpre, 4 * W), (dz_g, 7 * W),
            (dm_sb, 8 * W), (dm_g, 9 * W)]
    dh = _mm([(a, w_big, 0, off, a.shape[1]) for a, off in secs] + [(dsmall, w_small, 0, 0, HD)],
             'nt', F32, "d_h", tk=min(MM_TK_MANY, W))
    g_secs = [_mm1(h, a, 'tn', F32, "g_w_in_%d" % i) for i, (a, _) in enumerate(secs)]
    g_small = _mm1(h, dsmall, 'tn', F32, "g_w_in_small")
    grad_x, g_nw, dscale, dshift = _ada_bwd(x2d, norm_w, scale, shift, dh, dx2)

    dmod = jnp.concatenate([dshift, dscale, dgate], axis=1)
    misc = jnp.concatenate([g_nw, g_fnw, g_gnw, dal[:, H:2 * H], ddtb[:, H:2 * H], loss_p[:, :1]], axis=1)
    pack3 = jnp.concatenate([dmod, g_conv, _pad_cols(misc, PW), jnp.zeros((2, PW), F32)], axis=0)
    got3 = _allgather8(pack3)
    tot = _sum8(got3)
    dmod_all = got3.reshape(8, 8, PW)[:, 0, :]
    g_w_ada = _w_ada_grad(c_all, lax.dynamic_slice_in_dim(dmod_all, chip * NA, NA, axis=1))
    g_conv_sh = lax.dynamic_slice_in_dim(tot[1:1 + CONV_K], chip * CW, CW, axis=1)
    loss = tot[5, 2 * D + HD + 2 * H]

    g_in = jnp.concatenate(g_secs[:6] + [g_small[:, :2 * H]] + g_secs[6:], axis=1)
    g_full = [jnp.stack([g_in[:, j * NI:(j + 1) * NI] for j in range(4)]),
              g_w_psb.reshape(4, D // 4, D), g_w_pg.reshape(4, D // 4, D), g_w_out.reshape(4, D // 4, D)]
    g_sib = _swap_partial_halves(g_full)
    parts = [_add_halves(g, r, pc, "add_halves_%d" % i) for i, (g, r) in enumerate(zip(g_full, g_sib))]
    got = _chip_scatter([p[0] for p in parts], [p[1] for p in parts])
    halves = [_sum_chips(q, pc, "sum_chips_%d" % i) for i, q in enumerate(got)]
    g_red = [f.reshape(f.shape[1] * 2, f.shape[2]) for f in _swap_final_halves(halves)]

    out = {}

    def upd(name, w, g, m, v, shape):
        d_, m_, v_ = _adamw(w, g, m, v, "adamw_" + name)
        out[name] = (g.reshape(shape), d_.reshape(shape), m_.reshape(shape), v_.reshape(shape))

    upd("w_ada", w_ada[0], g_w_ada, m_w_ada[0], v_w_ada[0], w_ada.shape)
    upd("w_in", w_in[0], g_red[0], m_w_in[0], v_w_in[0], w_in.shape)
    upd("gdn_conv_w", gdn_conv_w[0], g_conv_sh, m_gdn_conv_w[0], v_gdn_conv_w[0], gdn_conv_w.shape)
    upd("w_proj_sb", w_proj_sb[0], g_red[1], m_w_proj_sb[0], v_w_proj_sb[0], w_proj_sb.shape)
    upd("w_proj_gdn", w_proj_gdn[0], g_red[2], m_w_proj_gdn[0], v_w_proj_gdn[0], w_proj_gdn.shape)
    upd("w_out", w_out[0], g_red[3], m_w_out[0], v_w_out[0], w_out.shape)

    def packs(b, nw, fnw, gnw, a, dt):
        row = jnp.concatenate([nw, fnw.reshape(1, D), gnw, a, dt], axis=1)
        return jnp.concatenate([b, _pad_cols(row, PW), jnp.zeros((6, PW), F32)], axis=0)

    g_pack = jnp.concatenate([tot[0:1], tot[5:6], jnp.zeros((6, PW), F32)], axis=0)
    d_, m_, v_ = _adamw(packs(b_ada, norm_w, final_norm_w, gdn_norm_w, gdn_a_log, gdn_dt_bias), g_pack,
                        packs(m_b_ada, m_norm_w, m_final_norm_w, m_gdn_norm_w, m_gdn_a_log, m_gdn_dt_bias),
                        packs(v_b_ada, v_norm_w, v_final_norm_w, v_gdn_norm_w, v_gdn_a_log, v_gdn_dt_bias),
                        "adamw_small")
    offs = {"norm_w": (0, D, (1, D)), "final_norm_w": (D, D, (D,)), "gdn_norm_w": (2 * D, HD, (1, HD)),
            "gdn_a_log": (2 * D + HD, H, (1, H)), "gdn_dt_bias": (2 * D + HD + H, H, (1, H))}
    out["b_ada"] = tuple(a[0:1] for a in (g_pack, d_, m_, v_))
    for name, (o, n_, shp) in offs.items():
        out[name] = tuple(a[1, o:o + n_].reshape(shp) for a in (g_pack, d_, m_, v_))

    names = ['w_ada', 'b_ada', 'norm_w', 'w_in', 'gdn_conv_w', 'gdn_a_log', 'gdn_dt_bias', 'gdn_norm_w',
             'w_proj_sb', 'w_proj_gdn', 'w_out', 'final_norm_w']
    return (loss, grad_x.reshape(x.shape), *[out[n][0] for n in names], *[out[n][1] for n in names],
            *[out[n][2] for n in names], *[out[n][3] for n in names])
```

```python
import functools

import jax
import jax.numpy as jnp
from jax import lax
from jax.experimental import pallas as pl
from jax.experimental.pallas import tpu as pltpu

F32 = jnp.float32
BF16 = jnp.bfloat16
HD = 128
CH = 64
CONV_K = 4
NORM_EPS = 1e-6
L2_EPS = 1e-6
ADAM_LR, ADAM_B1, ADAM_B2, ADAM_EPS, ADAM_WD, ADAM_STEP = 0.001, 0.9, 0.999, 1e-08, 0.01, 10

VMEM_LIMIT = 56 * 1024 * 1024
SB_BQ, SB_BK = 512, 256
SB_DEAD = -110.0
GDN_TB = 256
GDN_G = 8
EW_TM = 256
HEAD_TM = 512
CONV_CW = 512
MM_TM, MM_TN, MM_TK = 1024, 1024, 1024
MM_TK_MANY = 512
SWAP_CHUNKS = 4
MESH = pl.DeviceIdType.MESH

NN = (((1,), (0,)), ((), ()))
NT = (((1,), (1,)), ((), ()))
TN = (((0,), (0,)), ((), ()))


def _pcall(body, **kw):
    return pl.pallas_call(body, **kw)


def _cparams(sem=None):
    return pltpu.CompilerParams(dimension_semantics=sem, vmem_limit_bytes=VMEM_LIMIT)


def _dot(a, b, dims=NN):
    return lax.dot_general(a, b, dims, preferred_element_type=F32)


def _dotf(a, b, dims=NN):
    return lax.dot_general(a, b, dims, precision=lax.Precision.HIGHEST, preferred_element_type=F32)


def _bdot_make(dims, da_rule, db_rule):
    @jax.custom_vjp
    def f(a, b):
        return _dot(a.astype(BF16), b.astype(BF16), dims)

    def fwd(a, b):
        return f(a, b), (a, b)

    def bwd(res, g):
        a, b = res
        return da_rule(g, a, b), db_rule(g, a, b)

    f.defvjp(fwd, bwd)
    return f


def _rdot(a, b, dims):
    return _dot(a.astype(BF16), b.astype(BF16), dims)


NNB =(((2,), (1,)), ((0,), (0,)))
NTB = (((2,), (2,)), ((0,), (0,)))
TNB = (((1,), (1,)), ((0,), (0,)))
_bdot_nn = _bdot_make(NNB, lambda g, a, b: _rdot(g, b, NTB), lambda g, a, b: _rdot(a, g, TNB))
_bdot_nt = _bdot_make(NTB, lambda g, a, b: _rdot(g, b, NNB), lambda g, a, b: _rdot(g, a, TNB))
_bdot_tn = _bdot_make(TNB, lambda g, a, b: _rdot(b, g, NTB), lambda g, a, b: _rdot(a, g, NNB))


def _iota(shape, axis):
    return lax.broadcasted_iota(jnp.int32, shape, axis)


def _sigmoid(x):
    e = jnp.exp(-jnp.abs(x))
    return jnp.where(x >= 0, 1.0 / (1.0 + e), e / (1.0 + e))


def _silu(x):
    return x * _sigmoid(x)


def _softplus(x):
    return jnp.maximum(x, 0.0) + jnp.log(1.0 + jnp.exp(-jnp.abs(x)))


def _rms(x, w):
    return x * lax.rsqrt(jnp.mean(x * x, axis=-1, keepdims=True) + NORM_EPS) * w


def _ew(fn, grid, ins, in_specs, outs, out_specs, acc=None, name=None):
    n_in = len(ins)
    acc = acc or [None] * len(outs)

    def body(*refs):
        vals = fn(*[r[...] for r in refs[:n_in]])
        if not isinstance(vals, (tuple, list)):
            vals = (vals,)
        for r, v, a in zip(refs[n_in:], vals, acc):
            if a is None:
                r[...] = v.astype(r.dtype)
                continue
            first = pl.program_id(len(grid) - 1) == 0
            if a == 'all':
                for ax in range(len(grid) - 1):
                    first = jnp.logical_and(first, pl.program_id(ax) == 0)

            @pl.when(first)
            def _():
                r[...] = v.astype(r.dtype)

            @pl.when(jnp.logical_not(first))
            def _():
                r[...] += v.astype(r.dtype)

    res = _pcall(body, grid=grid, in_specs=in_specs, out_specs=out_specs,
                 out_shape=[jax.ShapeDtypeStruct(s, d) for s, d in outs],
                 compiler_params=_cparams(("arbitrary",) * len(grid)), name=name)(*ins)
    return res


def _mm(pairs, mode, out_dtype, name, add=None, tm=None, tn=None, tk=None):
    a0, b0 = pairs[0][0], pairs[0][1]
    M = a0.shape[1] if mode == 'tn' else a0.shape[0]
    N = b0.shape[0] if mode == 'nt' else b0.shape[1]
    tm = min(tm or MM_TM, M)
    tn = min(tn or MM_TN, N)
    tk = tk or MM_TK
    tks = [min(tk, p[4]) for p in pairs]
    nks = [p[4] // t for p, t in zip(pairs, tks)]
    offs = [sum(nks[:i]) for i in range(len(pairs))]
    total = sum(nks)
    assert M % tm == 0 and N % tn == 0 and all(p[4] % t == 0 for p, t in zip(pairs, tks)), (name, M, N)
    dims = {'nn': NN, 'nt': NT, 'tn': TN}[mode]

    in_specs, ins = [], []
    for (a, b, ao, bo, kl), t, nk, off in zip(pairs, tks, nks, offs):
        def kidx(kk, nk=nk, off=off):
            return jnp.minimum(jnp.maximum(kk - off, 0), nk - 1)
        if mode == 'tn':
            in_specs.append(pl.BlockSpec((t, tm), lambda i, j, kk, f=kidx, o=ao // t: (o + f(kk), i)))
        else:
            in_specs.append(pl.BlockSpec((tm, t), lambda i, j, kk, f=kidx, o=ao // t: (i, o + f(kk))))
        if mode == 'nt':
            in_specs.append(pl.BlockSpec((tn, t), lambda i, j, kk, f=kidx, o=bo // t: (j, o + f(kk))))
        else:
            in_specs.append(pl.BlockSpec((t, tn), lambda i, j, kk, f=kidx, o=bo // t: (o + f(kk), j)))
        ins += [a, b]
    if add is not None:
        in_specs.append(pl.BlockSpec((tm, tn), lambda i, j, kk: (i, j)))
        ins.append(add)
    npairs = len(pairs)

    def body(*refs):
        out_ref, acc_ref = refs[-2], refs[-1]
        kk = pl.program_id(2)

        @pl.when(kk == 0)
        def _():
            acc_ref[...] = jnp.zeros_like(acc_ref)

        for p in range(npairs):
            def upd(p=p):
                acc_ref[...] += _dot(refs[2 * p][...].astype(BF16), refs[2 * p + 1][...].astype(BF16), dims)
            if npairs == 1:
                upd()
            else:
                pl.when(jnp.logical_and(kk >= offs[p], kk < offs[p] + nks[p]))(upd)

        @pl.when(kk == total - 1)
        def _():
            r = acc_ref[...]
            if add is not None:
                r = r + refs[2 * npairs][...]
            out_ref[...] = r.astype(out_ref.dtype)

    return _pcall(body, grid=(M // tm, N // tn, total), in_specs=in_specs,
                  out_specs=pl.BlockSpec((tm, tn), lambda i, j, kk: (i, j)),
                  out_shape=jax.ShapeDtypeStruct((M, N), out_dtype),
                  scratch_shapes=[pltpu.VMEM((tm, tn), F32)],
                  compiler_params=_cparams(("parallel", "parallel", "arbitrary")), name=name)(*ins)


def _mm1(a, b, mode, out_dtype, name, **kw):
    k = a.shape[0] if mode == 'tn' else a.shape[1]
    return _mm([(a, b, 0, 0, k)], mode, out_dtype, name, **kw)


def _sb_tiles(T):
    bq = min(SB_BQ, T)
    bk = min(SB_BK, bq)
    return bq, bk, bq // bk


def _sb_block(i, j, masked, q, k_ref, v_ref, accL, bq, bk, UU):
    r = pl.ds(pl.multiple_of(j * bk, bk), bk)
    kj = k_ref[r, :]
    vj = v_ref[r, :]
    z = _dot(q, kj, NT) * (HD ** -0.5)
    lb = jnp.minimum(z, 0.0) - jnp.log(1.0 + jnp.exp(-jnp.abs(z)))
    ln = lb - z
    mask = None
    if masked:
        mask = (j * bk + _iota((bq, bk), 1)) < (i * bq + _iota((bq, bk), 0))
        ln = jnp.where(mask, ln, 0.0)
    hi = ln.astype(BF16)
    lo = (ln - hi.astype(F32)).astype(BF16)
    later = _dot(jnp.concatenate([hi, lo], axis=1), UU) + accL
    a = jnp.exp(lb + later)
    if masked:
        a = jnp.where(mask, a, 0.0)
    return r, kj, vj, mask, ln, lb, a


def _sb_fwd(proj, T, H):
    bq, bk, nd = _sb_tiles(T)

    assert T // bk < HD

    def body(q_ref, kf_ref, vf_ref, o_ref, l_ref, k_ref, v_ref):
        i = pl.program_id(1)

        @pl.when(i == 0)
        def _():
            k_ref[...] = kf_ref[...].astype(BF16)
            v_ref[...] = vf_ref[...].astype(BF16)

        q = q_ref[...].astype(BF16)
        U = (_iota((bk, bk), 0) > _iota((bk, bk), 1)).astype(BF16)
        UU = jnp.concatenate([U, U], axis=0)
        lane = _iota((bq, HD), 1)

        def blk(j, carry, masked):
            acc, accL, saved = carry
            _, _, vj, _, ln, _, a = _sb_block(i, j, masked, q, k_ref, v_ref, accL, bq, bk, UU)
            acc = acc + _dot(a.astype(BF16), vj)
            return acc, accL + jnp.sum(ln, axis=1, keepdims=True), jnp.where(lane == j, accL, saved)

        carry = (jnp.zeros((bq, HD), F32), jnp.zeros((bq, 1), F32), jnp.zeros((bq, HD), F32))
        for jj in range(nd):
            carry = blk(i * nd + (nd - 1 - jj), carry, True)
        def group(it, c):
            for jj in range(nd):
                c = blk((i - 1 - it) * nd + (nd - 1 - jj), c, False)
            return c

        def more(c):
            return jnp.logical_and(c[0] < i, jnp.max(c[2]) > SB_DEAD)

        done = lax.while_loop(more, lambda c: (c[0] + 1,) + group(c[0], c[1:]), (jnp.int32(0),) + carry)
        o_ref[...] = done[1]
        l_ref[...] = jnp.where(lane == HD - 1, done[0].astype(F32), done[3])

    blk_q = pl.BlockSpec((bq, HD), lambda h, i: (i, h))
    return _pcall(
        body, grid=(H, T // bq),
        in_specs=[blk_q, pl.BlockSpec((T, HD), lambda h, i: (0, H + h)),
                  pl.BlockSpec((T, HD), lambda h, i: (0, 2 * H + h))],
        out_specs=[blk_q, blk_q],
        out_shape=[jax.ShapeDtypeStruct((T, H * HD), F32)] * 2,
        scratch_shapes=[pltpu.VMEM((T, HD), BF16), pltpu.VMEM((T, HD), BF16)],
        compiler_params=_cparams(("parallel", "arbitrary")), name="sb_fwd")(proj, proj, proj)


def _sb_bwd(proj, l_sb, do_sb, T, H):
    bq, bk, nd = _sb_tiles(T)
    nq = T // bq

    def body(q_ref, kf_ref, vf_ref, l_ref, do_ref, dq_ref, dk_ref, dv_ref, dk_acc, dv_acc, k_ref, v_ref):
        i = pl.program_id(1)

        @pl.when(i == 0)
        def _():
            dk_acc[...] = jnp.zeros_like(dk_acc)
            dv_acc[...] = jnp.zeros_like(dv_acc)
            k_ref[...] = kf_ref[...].astype(BF16)
            v_ref[...] = vf_ref[...].astype(BF16)

        q = q_ref[...].astype(BF16)
        dob = do_ref[...].astype(BF16)
        saved = l_ref[...]
        lane = _iota((bq, HD), 1)
        U = (_iota((bk, bk), 0) > _iota((bk, bk), 1)).astype(BF16)
        UU = jnp.concatenate([U, U], axis=0)
        Ue = (_iota((bk, bk), 0) < _iota((bk, bk), 1)).astype(BF16)

        def blk(j, carry, masked):
            dq, accP = carry
            accL = jnp.sum(jnp.where(lane == j, saved, 0.0), axis=1, keepdims=True)
            r, kj, vj, mask, _, lb, a = _sb_block(i, j, masked, q, k_ref, v_ref, accL, bq, bk, UU)
            p = a * _dot(dob, vj, NT)
            pre = _dot(p.astype(BF16), Ue) + accP
            sig = jnp.exp(lb)
            dz = p * (1.0 - sig) - pre * sig
            if masked:
                dz = jnp.where(mask, dz, 0.0)
            dzb = (dz * (HD ** -0.5)).astype(BF16)
            dq = dq + _dot(dzb, kj)
            dk_acc[r, :] += _dot(dzb, q, TN)
            dv_acc[r, :] += _dot(a.astype(BF16), dob, TN)
            return dq, accP + jnp.sum(p, axis=1, keepdims=True)

        carry = (jnp.zeros((bq, HD), F32), jnp.zeros((bq, 1), F32))
        def group(it, c):
            for jj in range(nd):
                c = blk(it * nd + jj, c, False)
            return c

        walked = jnp.max(jnp.where(lane == HD - 1, saved, 0.0)).astype(jnp.int32)
        carry = lax.fori_loop(i - walked, i, group, carry)
        for jj in range(nd):
            carry = blk(i * nd + jj, carry, True)
        dq_ref[...] = carry[0].astype(BF16)

        @pl.when(i == nq - 1)
        def _():
            dk_ref[...] = dk_acc[...].astype(BF16)
            dv_ref[...] = dv_acc[...].astype(BF16)

    W = H * HD
    blk_q = pl.BlockSpec((bq, HD), lambda h, i: (i, h))
    blk_t = pl.BlockSpec((T, HD), lambda h, i: (0, h))
    return _pcall(
        body, grid=(H, nq),
        in_specs=[blk_q, pl.BlockSpec((T, HD), lambda h, i: (0, H + h)),
                  pl.BlockSpec((T, HD), lambda h, i: (0, 2 * H + h)), blk_q, blk_q],
        out_specs=[blk_q, blk_t, blk_t],
        out_shape=[jax.ShapeDtypeStruct((T, W), BF16)] * 3,
        scratch_shapes=[pltpu.VMEM((T, HD), F32), pltpu.VMEM((T, HD), F32),
                        pltpu.VMEM((T, HD), BF16), pltpu.VMEM((T, HD), BF16)],
        compiler_params=_cparams(("parallel", "arbitrary")), name="sb_bwd")(proj, proj, proj, l_sb, do_sb)


@jax.custom_vjp
def _unit_lower_inv(lw):
    C = lw.shape[1]
    x = (_iota(lw.shape, 1) == _iota(lw.shape, 2)).astype(F32) - lw
    pw = lw
    for _ in range(C.bit_length() - 2):
        pw = _dotf(pw, pw, NNB)
        x = x + _dotf(x, pw, NNB)
    return x


_unit_lower_inv.defvjp(lambda lw: (_unit_lower_inv(lw),) * 2,
                       lambda x, g: (-_dotf(_dotf(x, g, TNB), x, NTB),))


def _gdn_chunk(q, k, v, bb, gb, S):
    G, C = q.shape[0], q.shape[1]
    ri, ci = _iota((G, C, C), 1), _iota((G, C, C), 2)
    tril, strict = ri >= ci, ri > ci
    g_col = gb[:, :, :C]
    g_row = jnp.swapaxes(g_col, 1, 2)
    decay = jnp.where(tril, jnp.exp(jnp.where(tril, g_col - g_row, 0.0)), 0.0)
    eg = jnp.exp(gb)
    qs = q * (HD ** -0.5)
    kb = k * bb
    lw = jnp.where(strict, _bdot_nt(kb, k) * decay, 0.0)
    x = _unit_lower_inv(lw)
    u = _bdot_nn(x, v * bb)
    w = _bdot_nn(x, kb * eg)
    aq = jnp.where(tril, _bdot_nt(qs, k) * decay, 0.0)
    vnew = u - _bdot_nn(w, S)
    o = _bdot_nn(qs * eg, S) + _bdot_nn(aq, vnew)
    g_last = gb[:, C - 1:C, :]
    s_new = S * jnp.exp(g_last) + _bdot_tn(k * jnp.exp(g_last - gb), vnew)
    return o, s_new


def _gdn_fwd(qn, kn, vc, beta_b, g_b, T, H):
    tb = min(GDN_TB, T)
    nc = tb // CH
    G = min(GDN_G, H)

    def body(q_ref, k_ref, v_ref, b_ref, g_ref, o_ref, s_ref, s_scr):
        @pl.when(pl.program_id(1) == 0)
        def _():
            s_scr[...] = jnp.zeros_like(s_scr)

        def step(c, carry):
            r = pl.ds(pl.multiple_of(c * CH, CH), CH)
            s = s_scr[...]
            s_ref[:, c] = s
            o, s2 = _gdn_chunk(q_ref[:, r, :], k_ref[:, r, :], v_ref[:, r, :], b_ref[:, r, :], g_ref[:, r, :], s)
            o_ref[:, r, :] = o
            s_scr[...] = s2
            return carry

        lax.fori_loop(0, nc, step, 0)

    blk3 = pl.BlockSpec((G, tb, HD), lambda h, t: (h, t, 0))
    return _pcall(
        body, grid=(H // G, T // tb), in_specs=[blk3] * 5,
        out_specs=[blk3, pl.BlockSpec((G, nc, HD, HD), lambda h, t: (h, t, 0, 0))],
        out_shape=[jax.ShapeDtypeStruct((H, T, HD), F32), jax.ShapeDtypeStruct((H, T // CH, HD, HD), F32)],
        scratch_shapes=[pltpu.VMEM((G, HD, HD), F32)],
        compiler_params=_cparams(("parallel", "arbitrary")), name="gdn_fwd")(qn, kn, vc, beta_b, g_b)


def _gdn_bwd(qn, kn, vc, beta_b, g_b, s_all, do, T, H):
    tb = min(GDN_TB, T)
    nc = tb // CH
    nt = T // tb
    G = min(GDN_G, H)

    def body(q_ref, k_ref, v_ref, b_ref, g_ref, s_ref, do_ref, dq_ref, dk_ref, dv_ref, db_ref, dg_ref, ds_scr):
        @pl.when(pl.program_id(1) == 0)
        def _():
            ds_scr[...] = jnp.zeros_like(ds_scr)

        def step(it, carry):
            c = nc - 1 - it
            r = pl.ds(pl.multiple_of(c * CH, CH), CH)
            _, vjp = jax.vjp(_gdn_chunk, q_ref[:, r, :], k_ref[:, r, :], v_ref[:, r, :], b_ref[:, r, :],
                             g_ref[:, r, :], s_ref[:, c])
            dq, dk, dv, db, dg, ds = vjp((do_ref[:, r, :], ds_scr[...]))
            dq_ref[:, r, :] = dq
            dk_ref[:, r, :] = dk
            dv_ref[:, r, :] = dv
            db_ref[:, r, :] = db
            dg_ref[:, r, :] = dg
            ds_scr[...] = ds
            return carry

        lax.fori_loop(0, nc, step, 0)

    blk3 = pl.BlockSpec((G, tb, HD), lambda h, t: (h, nt - 1 - t, 0))
    return _pcall(
        body, grid=(H // G, nt),
        in_specs=[blk3] * 5 + [pl.BlockSpec((G, nc, HD, HD), lambda h, t: (h, nt - 1 - t, 0, 0)), blk3],
        out_specs=[blk3] * 5,
        out_shape=[jax.ShapeDtypeStruct((H, T, HD), F32)] * 5,
        scratch_shapes=[pltpu.VMEM((G, HD, HD), F32)],
        compiler_params=_cparams(("parallel", "arbitrary")), name="gdn_bwd")(qn, kn, vc, beta_b, g_b, s_all, do)


def _conv_tiles(T, C):
    return min(HEAD_TM, T), min(CONV_CW, C)


def _shift_down(main, halo, first, d):
    halo = jnp.where(first, 0.0, halo)
    ext = jnp.concatenate([halo, main], axis=0)
    return pltpu.roll(ext, d, 0)[8:]


def _conv_fwd(proj, conv_w, T, H):
    W = H * HD
    C = 3 * W
    tm, cw = _conv_tiles(T, C)
    col0 = 4 * W // cw

    def body(x_ref, h_ref, w_ref, o_ref):
        first = pl.program_id(1) == 0
        main = x_ref[...]
        halo = h_ref[...]
        w = w_ref[...]
        out = w[CONV_K - 1:CONV_K, :] * main
        for d in range(1, CONV_K):
            out = out + w[CONV_K - 1 - d:CONV_K - d, :] * _shift_down(main, halo, first, d)
        o_ref[...] = out

    return _pcall(
        body, grid=(C // cw, T // tm),
        in_specs=[pl.BlockSpec((tm, cw), lambda cb, i: (i, col0 + cb)),
                  pl.BlockSpec((8, cw), lambda cb, i: (jnp.maximum(i * (tm // 8) - 1, 0), col0 + cb)),
                  pl.BlockSpec((CONV_K, cw), lambda cb, i: (0, cb))],
        out_specs=pl.BlockSpec((tm, cw), lambda cb, i: (i, cb)),
        out_shape=jax.ShapeDtypeStruct((T, C), F32),
        compiler_params=_cparams(("parallel", "arbitrary")), name="conv_fwd")(proj, proj, conv_w)


def _conv_bwd(proj, conv_w, dconv, T, H):
    W = H * HD
    C = 3 * W
    tm, cw = _conv_tiles(T, C)
    col0 = 4 * W // cw
    nt = T // tm

    def body(x_ref, h_ref, w_ref, d_ref, dn_ref, dx_ref, dw_ref):
        i = pl.program_id(1)
        first = i == 0
        main = x_ref[...]
        halo = h_ref[...]
        w = w_ref[...]
        dmain = d_ref[...]
        dnext = jnp.where(i == nt - 1, 0.0, dn_ref[...])
        dext = jnp.concatenate([dmain, dnext], axis=0)
        dx = w[CONV_K - 1:CONV_K, :] * dmain
        rows = [jnp.sum(dmain * main, axis=0, keepdims=True)]
        for d in range(1, CONV_K):
            dx = dx + w[CONV_K - 1 - d:CONV_K - d, :] * pltpu.roll(dext, tm + 8 - d, 0)[:tm]
            rows.append(jnp.sum(dmain * _shift_down(main, halo, first, d), axis=0, keepdims=True))
        dx_ref[...] = dx.astype(BF16)

        @pl.when(first)
        def _():
            dw_ref[...] = jnp.zeros_like(dw_ref)

        for d in range(CONV_K):
            dw_ref[CONV_K - 1 - d:CONV_K - d, :] += rows[d]

    return _pcall(
        body, grid=(C // cw, nt),
        in_specs=[pl.BlockSpec((tm, cw), lambda cb, i: (i, col0 + cb)),
                  pl.BlockSpec((8, cw), lambda cb, i: (jnp.maximum(i * (tm // 8) - 1, 0), col0 + cb)),
                  pl.BlockSpec((CONV_K, cw), lambda cb, i: (0, cb)),
                  pl.BlockSpec((tm, cw), lambda cb, i: (i, cb)),
                  pl.BlockSpec((8, cw), lambda cb, i: (jnp.minimum((i + 1) * (tm // 8), T // 8 - 1), cb))],
        out_specs=[pl.BlockSpec((tm, cw), lambda cb, i: (i, cb)), pl.BlockSpec((CONV_K, cw), lambda cb, i: (0, cb))],
        out_shape=[jax.ShapeDtypeStruct((T, C), BF16), jax.ShapeDtypeStruct((CONV_K, C), F32)],
        compiler_params=_cparams(("parallel", "arbitrary")), name="conv_bwd")(proj, proj, conv_w, dconv, dconv)


def _prep_fn(H, tm):
    def fn(cq, ck, cv, small, al, dtb):
        h = pl.program_id(1)
        lane = _iota((HD, HD), 0)
        oh_b = (lane == h).astype(F32)
        oh_a = (lane == H + h).astype(F32)
        q, k, v = _silu(cq), _silu(ck), _silu(cv)
        qn = q * lax.rsqrt(jnp.sum(q * q, axis=-1, keepdims=True) + L2_EPS)
        kn = k * lax.rsqrt(jnp.sum(k * k, axis=-1, keepdims=True) + L2_EPS)
        beta = _dotf(_sigmoid(small), oh_b)
        g = _dotf(-jnp.exp(al) * _softplus(small + dtb), oh_a)
        nch = tm // CH
        tri = (_iota((nch, CH, CH), 1) >= _iota((nch, CH, CH), 2)).astype(F32)
        gc = _dotf(tri, g.reshape(nch, CH, HD), NNB).reshape(tm, HD)
        return qn, kn, v, beta, gc
    return fn


def _prep_specs(T, H):
    tm = min(HEAD_TM, T)
    W = H * HD
    row = lambda s: pl.BlockSpec((tm, HD), lambda i, h, s=s: (i, s * H + h))
    small = pl.BlockSpec((tm, HD), lambda i, h: (i, 0))
    par = pl.BlockSpec((1, HD), lambda i, h: (0, 0))
    blk = pl.BlockSpec((tm, HD), lambda i, h: (i, h))
    blk3 = pl.BlockSpec((None, tm, HD), lambda i, h: (h, i, 0))
    return tm, W, row, small, par, blk, blk3


def _prep_fwd(conv, small, al, dtb, T, H):
    tm, W, row, sm, par, blk, blk3 = _prep_specs(T, H)
    return _ew(_prep_fn(H, tm), (T // tm, H), [conv, conv, conv, small, al, dtb],
               [row(0), row(1), row(2), sm, par, par],
               [((H, T, HD), F32)] * 5, [blk3] * 5, name="gdn_prep")


def _prep_bwd(conv, small, al, dtb, dqn, dkn, dvc, dbeta, dg, T, H):
    tm, W, row, sm, par, blk, blk3 = _prep_specs(T, H)
    f = _prep_fn(H, tm)

    def fn(cq, ck, cv, small_, al_, dtb_, a, b, c, d, e):
        _, vjp = jax.vjp(f, cq, ck, cv, small_, al_, dtb_)
        return vjp((a, b, c, d, e))

    return _ew(fn, (T // tm, H), [conv, conv, conv, small, al, dtb, dqn, dkn, dvc, dbeta, dg],
               [row(0), row(1), row(2), sm, par, par] + [blk3] * 5,
               [((T, W), F32)] * 3 + [((T, HD), F32), ((1, HD), F32), ((1, HD), F32)],
               [blk, blk, blk, sm, par, par], acc=[None, None, None, 'inner', 'all', 'all'], name="gdn_prep_bwd")


def _gate_fn(o_sb, z_sb, o_g, z_g, gnw):
    a_sb = o_sb * _silu(z_sb)
    a_g = _rms(o_g, gnw) * _silu(z_g)
    return a_sb, a_g


def _gate_specs(T, H):
    tm = min(HEAD_TM, T)
    blk = pl.BlockSpec((tm, HD), lambda i, h: (i, h))
    sec = lambda s: pl.BlockSpec((tm, HD), lambda i, h, s=s: (i, s * H + h))
    par = pl.BlockSpec((1, HD), lambda i, h: (0, 0))
    blk3 = pl.BlockSpec((None, tm, HD), lambda i, h: (h, i, 0))
    return tm, blk, blk3, sec, par


def _gate_fwd(o_sb, proj, o_g, gnw, T, H):
    tm, blk, blk3, sec, par = _gate_specs(T, H)
    W = H * HD
    return _ew(_gate_fn, (T // tm, H), [o_sb, proj, o_g, proj, gnw], [blk, sec(3), blk3, sec(7), par],
               [((T, W), BF16)] * 2, [blk, blk], name="gate_fwd")


def _gate_bwd(o_sb, proj, o_g, gnw, da_sb, da_g, T, H):
    tm, blk, blk3, sec, par = _gate_specs(T, H)
    W = H * HD

    def fn(o_sb_, z_sb, o_g_, z_g, gnw_, da, db):
        _, vjp = jax.vjp(_gate_fn, o_sb_, z_sb, o_g_, z_g, gnw_)
        return vjp((da, db))

    return _ew(fn, (T // tm, H), [o_sb, proj, o_g, proj, gnw, da_sb, da_g], [blk, sec(3), blk3, sec(7), par, blk, blk],
               [((T, W), F32), ((T, W), BF16), ((H, T, HD), F32), ((T, W), BF16), ((1, HD), F32)],
               [blk, blk, blk3, blk, par], acc=[None, None, None, None, 'all'], name="gate_bwd")


def _ada_fn(x, nw, scale, shift):
    return _rms(x, nw) * (1.0 + scale) + shift


def _row_specs(T, D):
    tm = min(EW_TM, T)
    return tm, pl.BlockSpec((tm, D), lambda i: (i, 0)), pl.BlockSpec((1, D), lambda i: (0, 0))


def _ada_fwd(x, nw, scale, shift):
    T, D = x.shape
    tm, row, par = _row_specs(T, D)
    return _ew(_ada_fn, (T // tm,), [x, nw, scale, shift], [row, par, par, par], [((T, D), BF16)], [row],
               name="ada_norm")[0]


def _ada_bwd(x, nw, scale, shift, dh, dx2):
    T, D = x.shape
    tm, row, par = _row_specs(T, D)

    def fn(x_, nw_, sc_, sh_, dh_, dx2_):
        _, vjp = jax.vjp(_ada_fn, x_, nw_, sc_, sh_)
        dx, dnw, dsc, dsh = vjp(dh_)
        return dx + dx2_, dnw, dsc, dsh

    return _ew(fn, (T // tm,), [x, nw, scale, shift, dh, dx2], [row, par, par, par, row, row],
               [((T, D), F32)] + [((1, D), F32)] * 3, [row, par, par, par], acc=[None, 'all', 'all', 'all'],
               name="ada_norm_bwd")


def _merge_fn(m_sb, m_g, p_sb, p_g):
    return _sigmoid(m_sb) * p_sb + _sigmoid(m_g) * p_g


def _merge_specs(T, D, H):
    tm, tc = min(HEAD_TM, T), min(512, D)
    nb = D // tc
    blk = pl.BlockSpec((tm, tc), lambda i, j: (i, j))
    sec = lambda s: pl.BlockSpec((tm, tc), lambda i, j, s=s: (i, s * nb + j))
    return tm, tc, blk, sec


def _merge_fwd(proj, p_sb, p_g, T, D, H):
    tm, tc, blk, sec = _merge_specs(T, D, H)
    return _ew(_merge_fn, (T // tm, D // tc), [proj, proj, p_sb, p_g], [sec(8), sec(9), blk, blk],
               [((T, D), BF16)], [blk], name="merge")[0]


def _merge_bwd(proj, p_sb, p_g, dy, T, D, H):
    tm, tc, blk, sec = _merge_specs(T, D, H)

    def fn(m_sb, m_g, p_sb_, p_g_, dy_):
        _, vjp = jax.vjp(_merge_fn, m_sb, m_g, p_sb_, p_g_)
        return vjp(dy_)

    return _ew(fn, (T // tm, D // tc), [proj, proj, p_sb, p_g, dy], [sec(8), sec(9), blk, blk, blk],
               [((T, D), BF16)] * 4, [blk] * 4, name="merge_bwd")


def _loss_head(x, u, gate, fnw, tgt):
    T, D = x.shape
    tm, row, par = _row_specs(T, D)

    def loss(x_, u_, gate_, fnw_, tgt_):
        y = _rms(x_ + gate_ * u_, fnw_)
        return 0.5 * jnp.sum(jnp.mean(jnp.square(y - tgt_), axis=-1))

    def fn(x_, u_, gate_, fnw_, tgt_):
        val, (dx, du, dgate, dfnw) = jax.value_and_grad(loss, argnums=(0, 1, 2, 3))(x_, u_, gate_, fnw_, tgt_)
        return jnp.full((1, HD), val, F32), dx, du, dgate, dfnw

    return _ew(fn, (T // tm,), [x, u, gate, fnw, tgt], [row, row, par, par, row],
               [((1, HD), F32), ((T, D), F32), ((T, D), BF16), ((1, D), F32), ((1, D), F32)],
               [pl.BlockSpec((1, HD), lambda i: (0, 0)), row, row, par, par],
               acc=['all', None, None, 'all', 'all'], name="loss_head")


def _mod_part(c_all, w_ada, b_ada):
    D, N = w_ada.shape
    tn = min(512, N)

    def fn(c, w, b):
        return _dot(_silu(c).astype(BF16), w.astype(BF16)) + b

    return _ew(fn, (N // tn,), [c_all, w_ada, b_ada],
               [pl.BlockSpec((8, D), lambda j: (0, 0)), pl.BlockSpec((D, tn), lambda j: (0, j)),
                pl.BlockSpec((1, tn), lambda j: (0, j))],
               [((8, N), F32)], [pl.BlockSpec((8, tn), lambda j: (0, j))], name="ada_mod")[0]


def _w_ada_grad(c_all, dmod):
    D, N = c_all.shape[1], dmod.shape[1]
    tn = min(512, N)

    def fn(c, dm):
        return _dot(_silu(c).astype(BF16), dm.astype(BF16), TN)

    return _ew(fn, (N // tn,), [c_all, dmod],
               [pl.BlockSpec((8, D), lambda j: (0, 0)), pl.BlockSpec((8, tn), lambda j: (0, j))],
               [((D, N), F32)], [pl.BlockSpec((D, tn), lambda j: (0, j))], name="w_ada_grad")[0]


def _sum8(packs):
    N = packs.shape[1]

    def fn(p):
        s = p[0:8]
        for d in range(1, 8):
            s = s + p[8 * d:8 * d + 8]
        return s

    return _ew(fn, (1,), [packs], [pl.BlockSpec((64, N), lambda i: (0, 0))], [((8, N), F32)],
               [pl.BlockSpec((8, N), lambda i: (0, 0))], name="sum_devices")[0]


def _sum_chips(q, c, name):
    _, R, C = q.shape
    tr = min(64, R)

    def body(c_ref, q_ref, o_ref):
        p = q_ref[...].astype(F32)
        o_ref[...] = (p[0] + p[1]) + (p[2] + p[3])

    grid_spec = pltpu.PrefetchScalarGridSpec(
        num_scalar_prefetch=1, grid=(R // tr,),
        in_specs=[pl.BlockSpec((4, tr, C), lambda i, c_ref: (0, i, 0))],
        out_specs=pl.BlockSpec((None, tr, C), lambda i, c_ref: (c_ref[0], i, 0)))
    return _pcall(body, grid_spec=grid_spec, out_shape=jax.ShapeDtypeStruct((2, R, C), F32),
                  compiler_params=_cparams(("arbitrary",)), name=name)(jnp.reshape(c, (1,)).astype(jnp.int32), q)


def _add_halves(g, rsib, c, name):
    _, R, C = g.shape
    rh = R // 2
    tr = min(64, rh)
    nb = rh // tr

    def body(c_ref, a_ref, b_ref, o_ref, o2_ref):
        v = (a_ref[...] + b_ref[...]).astype(o_ref.dtype)
        o_ref[...] = v
        o2_ref[...] = v

    spec = pl.BlockSpec((None, tr, C), lambda j, i, c_ref: (j, i, 0))
    grid_spec = pltpu.PrefetchScalarGridSpec(
        num_scalar_prefetch=1, grid=(4, nb),
        in_specs=[pl.BlockSpec((None, tr, C), lambda j, i, c_ref: (j, c_ref[0] * nb + i, 0)), spec],
        out_specs=[spec, spec])
    return _pcall(body, grid_spec=grid_spec, out_shape=[jax.ShapeDtypeStruct((4, rh, C), BF16)] * 2,
                  compiler_params=_cparams(("arbitrary", "arbitrary")), name=name)(
                      jnp.reshape(c, (1,)).astype(jnp.int32), g, rsib)


def _adamw(w, g, m, v, name):
    R, C = w.shape
    tr = R if R <= 64 else 64
    blk = pl.BlockSpec((tr, C), lambda i: (i, 0))

    def fn(w_, g_, m_, v_):
        m2 = ADAM_B1 * m_ + (1.0 - ADAM_B1) * g_
        v2 = ADAM_B2 * v_ + (1.0 - ADAM_B2) * jnp.square(g_)
        m_hat = m2 / (1.0 - ADAM_B1 ** ADAM_STEP)
        v_hat = v2 / (1.0 - ADAM_B2 ** ADAM_STEP)
        delta = -ADAM_LR * (m_hat / (jnp.sqrt(v_hat) + ADAM_EPS) + ADAM_WD * w_)
        return delta, m2, v2

    return _ew(fn, (R // tr,), [w, g, m, v], [blk] * 4, [((R, C), F32)] * 3, [blk] * 3, name=name)


def _place():
    x, y, c = lax.axis_index("x"), lax.axis_index("y"), lax.axis_index("c")
    return x, y, c, [(1 - x, y), (x, 1 - y), (1 - x, 1 - y)]


ANY = pl.BlockSpec(memory_space=pl.ANY)


def _allgather8(blk):
    m_per, n = blk.shape

    def body(x_ref, out_ref, send_sems, recv_sems, local_sem):
        x, y, c, chips = _place()
        me, sibling = (x, y, c), (x, y, 1 - c)

        def rows(px, py, pc):
            return out_ref.at[pl.ds((4 * px + 2 * py + pc) * m_per, m_per), :]

        def copy(k, block, to, src=None):
            return pltpu.make_async_remote_copy(
                src_ref=rows(*block) if src is None else src, dst_ref=rows(*block),
                send_sem=send_sems.at[k], recv_sem=recv_sems.at[k], device_id=to, device_id_type=MESH)

        mine = pltpu.make_async_copy(x_ref, rows(*me), local_sem)
        mine.start()
        first = [copy(0, me, sibling, src=x_ref)]
        first += [copy(1 + j, me, (*chip, c), src=x_ref) for j, chip in enumerate(chips)]
        for cp in first:
            cp.start()
        passed = [copy(4 + j, (*chip, c), sibling) for j, chip in enumerate(chips)]
        for j, chip in enumerate(chips):
            copy(1 + j, (*chip, c), me).wait_recv()
            passed[j].start()
        copy(0, sibling, me).wait_recv()
        for j, chip in enumerate(chips):
            copy(4 + j, (*chip, 1 - c), me).wait_recv()
        for cp in first + passed:
            cp.wait_send()
        mine.wait()

    vm = pl.BlockSpec(memory_space=pltpu.VMEM)
    return _pcall(body, out_shape=jax.ShapeDtypeStruct((8 * m_per, n), blk.dtype), in_specs=[vm], out_specs=vm,
                  scratch_shapes=[pltpu.SemaphoreType.DMA((7,)), pltpu.SemaphoreType.DMA((7,)),
                                  pltpu.SemaphoreType.DMA],
                  compiler_params=pltpu.CompilerParams(vmem_limit_bytes=VMEM_LIMIT), name="allgather8")(blk)


def _gather_weights(shards):
    n = len(shards)

    def body(*refs):
        ins, outs = refs[:n], refs[n:2 * n]
        send_sems, recv_sems, fsend_sems, frecv_sems = refs[2 * n:]
        x, y, c, chips = _place()
        mychip = 2 * x + y
        sends, fwds = [], []
        for a in range(n):
            rh = shards[a].shape[0] // 2
            for k, chip in enumerate(chips):
                cp = pltpu.make_async_remote_copy(
                    src_ref=ins[a].at[pl.ds(c * rh, rh), :], dst_ref=outs[a].at[mychip, pl.ds(c * rh, rh), :],
                    send_sem=send_sems.at[3 * a + k], recv_sem=recv_sems.at[3 * a + k],
                    device_id=(*chip, c), device_id_type=MESH)
                cp.start()
                sends.append(cp)
        for a in range(n):
            rh = shards[a].shape[0] // 2
            for k, (px, py) in enumerate(chips):
                land = outs[a].at[2 * px + py, pl.ds(c * rh, rh), :]
                pltpu.make_async_remote_copy(
                    src_ref=land, dst_ref=land, send_sem=send_sems.at[3 * a + k], recv_sem=recv_sems.at[3 * a + k],
                    device_id=(px, py, c), device_id_type=MESH).wait_recv()
                fw = pltpu.make_async_remote_copy(
                    src_ref=land, dst_ref=land, send_sem=fsend_sems.at[3 * a + k], recv_sem=frecv_sems.at[3 * a + k],
                    device_id=(x, y, 1 - c), device_id_type=MESH)
                fw.start()
                fwds.append(fw)
        for a in range(n):
            rh = shards[a].shape[0] // 2
            for k, (px, py) in enumerate(chips):
                land = outs[a].at[2 * px + py, pl.ds((1 - c) * rh, rh), :]
                pltpu.make_async_remote_copy(
                    src_ref=land, dst_ref=land, send_sem=fsend_sems.at[3 * a + k], recv_sem=frecv_sems.at[3 * a + k],
                    device_id=(x, y, 1 - c), device_id_type=MESH).wait_recv()
        for cp in sends + fwds:
            cp.wait_send()

    return _pcall(body, out_shape=[jax.ShapeDtypeStruct((4,) + s.shape, s.dtype) for s in shards],
                  in_specs=[ANY] * n, out_specs=[ANY] * n,
                  scratch_shapes=[pltpu.SemaphoreType.DMA((3 * n,))] * 4, name="gather_weights")(*shards)


def _row_chunks(rows, n):
    while n > 1 and rows % (8 * n):
        n //= 2
    return [(k * (rows // n), rows // n) for k in range(n)]


def _swap_partial_halves(gs):
    n = len(gs)
    plan = [(a, j, r0, nr) for a in range(n) for j in range(4)
            for r0, nr in _row_chunks(gs[a].shape[1] // 2, SWAP_CHUNKS)]

    def body(*refs):
        ins, outs = refs[:n], refs[n:2 * n]
        send_sems, recv_sems = refs[2 * n:]
        x, y, c, _ = _place()
        cps = []
        for k, (a, j, r0, nr) in enumerate(plan):
            rh = gs[a].shape[1] // 2
            cp = pltpu.make_async_remote_copy(
                src_ref=ins[a].at[j, pl.ds((1 - c) * rh + r0, nr), :], dst_ref=outs[a].at[j, pl.ds(r0, nr), :],
                send_sem=send_sems.at[k], recv_sem=recv_sems.at[k], device_id=(x, y, 1 - c), device_id_type=MESH)
            cp.start()
            cps.append(cp)
        for cp in cps:
            cp.wait()

    half = [jax.ShapeDtypeStruct((4, g.shape[1] // 2, g.shape[2]), g.dtype) for g in gs]
    return _pcall(body, out_shape=half, in_specs=[ANY] * n, out_specs=[ANY] * n,
                  scratch_shapes=[pltpu.SemaphoreType.DMA((len(plan),))] * 2, name="swap_partial_halves")(*gs)


def _chip_scatter(ps, lands):
    n = len(ps)

    def body(*refs):
        ins, outs = refs[:n], refs[2 * n:3 * n]
        send_sems, recv_sems = refs[3 * n:]
        x, y, c, chips = _place()
        mychip = 2 * x + y
        cps = []
        for a in range(n):
            for k, (px, py) in enumerate(chips):
                cp = pltpu.make_async_remote_copy(
                    src_ref=ins[a].at[2 * px + py], dst_ref=outs[a].at[mychip],
                    send_sem=send_sems.at[3 * a + k], recv_sem=recv_sems.at[3 * a + k],
                    device_id=(px, py, c), device_id_type=MESH)
                cp.start()
                cps.append(cp)
        for a in range(n):
            for k, (px, py) in enumerate(chips):
                land = outs[a].at[2 * px + py]
                pltpu.make_async_remote_copy(
                    src_ref=land, dst_ref=land, send_sem=send_sems.at[3 * a + k], recv_sem=recv_sems.at[3 * a + k],
                    device_id=(px, py, c), device_id_type=MESH).wait_recv()
        for cp in cps:
            cp.wait_send()

    return _pcall(body, out_shape=[jax.ShapeDtypeStruct(p.shape, p.dtype) for p in ps],
                  in_specs=[ANY] * (2 * n), out_specs=[ANY] * n,
                  input_output_aliases={n + a: a for a in range(n)},
                  scratch_shapes=[pltpu.SemaphoreType.DMA((3 * n,))] * 2, name="chip_scatter")(*ps, *lands)


def _swap_final_halves(hs):
    n = len(hs)
    plan = [(a, r0, nr) for a in range(n) for r0, nr in _row_chunks(hs[a].shape[1], 2 * SWAP_CHUNKS)]

    def body(*refs):
        outs = refs[n:2 * n]
        send_sems, recv_sems = refs[2 * n:]
        x, y, c, _ = _place()
        cps = []
        for k, (a, r0, nr) in enumerate(plan):
            mine = outs[a].at[c, pl.ds(r0, nr), :]
            cp = pltpu.make_async_remote_copy(
                src_ref=mine, dst_ref=mine, send_sem=send_sems.at[k], recv_sem=recv_sems.at[k],
                device_id=(x, y, 1 - c), device_id_type=MESH)
            cp.start()
            cps.append(cp)
        for k, (a, r0, nr) in enumerate(plan):
            land = outs[a].at[1 - c, pl.ds(r0, nr), :]
            pltpu.make_async_remote_copy(
                src_ref=land, dst_ref=land, send_sem=send_sems.at[k], recv_sem=recv_sems.at[k],
                device_id=(x, y, 1 - c), device_id_type=MESH).wait_recv()
        for cp in cps:
            cp.wait_send()

    return _pcall(body, out_shape=[jax.ShapeDtypeStruct(h.shape, h.dtype) for h in hs],
                  in_specs=[ANY] * n, out_specs=[ANY] * n, input_output_aliases={a: a for a in range(n)},
                  scratch_shapes=[pltpu.SemaphoreType.DMA((len(plan),))] * 2, name="swap_final_halves")(*hs)


def _pad_cols(a, n):
    return jnp.pad(a, ((0, 0), (0, n - a.shape[1])))


def kernel(x, c, w_ada, b_ada, norm_w, w_in, gdn_conv_w, gdn_a_log, gdn_dt_bias, gdn_norm_w, w_proj_sb, w_proj_gdn, w_out, final_norm_w, loss_target, m_w_ada, m_b_ada, m_norm_w, m_w_in, m_gdn_conv_w, m_gdn_a_log, m_gdn_dt_bias, m_gdn_norm_w, m_w_proj_sb, m_w_proj_gdn, m_w_out, m_final_norm_w, v_w_ada, v_b_ada, v_norm_w, v_w_in, v_gdn_conv_w, v_gdn_a_log, v_gdn_dt_bias, v_gdn_norm_w, v_w_proj_sb, v_w_proj_gdn, v_w_out, v_final_norm_w):
    T, D = x.shape[1], x.shape[2]
    H = gdn_a_log.shape[1]
    W = H * HD
    assert W == D and T % CH == 0
    NA = w_ada.shape[2]
    NI = w_in.shape[2]
    CW = gdn_conv_w.shape[2]
    px, py, pc = lax.axis_index("x"), lax.axis_index("y"), lax.axis_index("c")
    chip = 2 * px + py
    me = 2 * chip + pc
    x2d, tgt = x[0], loss_target[0]
    PW = 3 * D

    pack1 = jnp.concatenate([c, _pad_cols(gdn_conv_w[0], D), jnp.zeros((3, D), F32)], axis=0)
    got1 = _allgather8(pack1).reshape(8, 8, D)
    c_all = got1[:, 0, :]
    conv_w = jnp.concatenate([got1[2 * j, 1:1 + CONV_K, :CW] for j in range(4)], axis=1)

    b_shard = lax.dynamic_slice_in_dim(b_ada, chip * NA, NA, axis=1)
    mod_part = _mod_part(c_all, w_ada[0], b_shard)
    got2 = _allgather8(mod_part).reshape(8, 8, NA)
    mod = jnp.concatenate([lax.dynamic_index_in_dim(got2[2 * j], me, 0) for j in range(4)], axis=1)
    shift, scale, gate = mod[:, :D], mod[:, D:2 * D], mod[:, 2 * D:]

    own = [w_in[0].astype(BF16), w_proj_sb[0].astype(BF16), w_proj_gdn[0].astype(BF16), w_out[0].astype(BF16)]
    is_own = (jnp.arange(4) == chip)[:, None, None]
    wg = [jnp.where(is_own, o[None], g) for o, g in zip(own, _gather_weights(own))]
    w_in_full = jnp.concatenate([wg[0][j] for j in range(4)], axis=1)
    w_big = jnp.concatenate([w_in_full[:, :8 * W], w_in_full[:, 8 * W + 2 * H:]], axis=1)
    w_small = _pad_cols(w_in_full[:, 8 * W:8 * W + 2 * H], HD)
    w_psb, w_pg, w_o = (wg[i].reshape(D, D) for i in (1, 2, 3))

    h = _ada_fwd(x2d, norm_w, scale, shift)
    proj = _mm1(h, w_big, 'nn', F32, "proj_big")
    small = _mm1(h, w_small, 'nn', F32, "proj_small")
    o_sb, l_sb = _sb_fwd(proj, T, H)
    conv = _conv_fwd(proj, conv_w, T, H)
    al = jnp.pad(gdn_a_log, ((0, 0), (H, HD - 2 * H)))
    dtb = jnp.pad(gdn_dt_bias, ((0, 0), (H, HD - 2 * H)))
    qn, kn, vc, beta_b, g_b = _prep_fwd(conv, small, al, dtb, T, H)
    o_g, s_all = _gdn_fwd(qn, kn, vc, beta_b, g_b, T, H)
    a_sb, a_g = _gate_fwd(o_sb, proj, o_g, gdn_norm_w, T, H)
    p_sb = _mm1(a_sb, w_psb, 'nn', F32, "proj_sb")
    p_g = _mm1(a_g, w_pg, 'nn', F32, "proj_gdn")
    y = _merge_fwd(proj, p_sb, p_g, T, D, H)
    u = _mm1(y, w_o, 'nn', F32, "proj_out")
    loss_p, dx2, du, dgate, g_fnw = _loss_head(x2d, u, gate, final_norm_w.reshape(1, D), tgt)

    dy = _mm1(du, w_o, 'nt', F32, "d_merge")
    g_w_out = _mm1(y, du, 'tn', F32, "g_w_out")
    dm_sb, dm_g, dp_sb, dp_g = _merge_bwd(proj, p_sb, p_g, dy, T, D, H)
    da_sb = _mm1(dp_sb, w_psb, 'nt', F32, "d_a_sb")
    g_w_psb = _mm1(a_sb, dp_sb, 'tn', F32, "g_w_proj_sb")
    da_g = _mm1(dp_g, w_pg, 'nt', F32, "d_a_gdn")
    g_w_pg = _mm1(a_g, dp_g, 'tn', F32, "g_w_proj_gdn")
    do_sb, dz_sb, do_g, dz_g, g_gnw = _gate_bwd(o_sb, proj, o_g, gdn_norm_w, da_sb, da_g, T, H)
    dq_sb, dk_sb, dv_sb = _sb_bwd(proj, l_sb, do_sb, T, H)
    dqn, dkn, dvc, dbeta_b, dg_b = _gdn_bwd(qn, kn, vc, beta_b, g_b, s_all, do_g, T, H)
    dcq, dck, dcv, dsmall, dal, ddtb = _prep_bwd(conv, small, al, dtb, dqn, dkn, dvc, dbeta_b, dg_b, T, H)
    dpre, g_conv = _conv_bwd(proj, conv_w, jnp.concatenate([dcq, dck, dcv], axis=1), T, H)

    secs = [(dq_sb, 0), (dk_sb, W), (dv_sb, 2 * W), (dz_sb, 3 * W), (dpre, 4 * W), (dz_g, 7 * W),
            (dm_sb, 8 * W), (dm_g, 9 * W)]
    dh = _mm([(a, w_big, 0, off, a.shape[1]) for a, off in secs] + [(dsmall, w_small, 0, 0, HD)],
             'nt', F32, "d_h", tk=min(MM_TK_MANY, W))
    g_secs = [_mm1(h, a, 'tn', F32, "g_w_in_%d" % i) for i, (a, _) in enumerate(secs)]
    g_small = _mm1(h, dsmall, 'tn', F32, "g_w_in_small")
    grad_x, g_nw, dscale, dshift = _ada_bwd(x2d, norm_w, scale, shift, dh, dx2)

    dmod = jnp.concatenate([dshift, dscale, dgate], axis=1)
    misc = jnp.concatenate([g_nw, g_fnw, g_gnw, dal[:, H:2 * H], ddtb[:, H:2 * H], loss_p[:, :1]], axis=1)
    pack3 = jnp.concatenate([dmod, g_conv, _pad_cols(misc, PW), jnp.zeros((2, PW), F32)], axis=0)
    got3 = _allgather8(pack3)
    tot = _sum8(got3)
    dmod_all = got3.reshape(8, 8, PW)[:, 0, :]
    g_w_ada = _w_ada_grad(c_all, lax.dynamic_slice_in_dim(dmod_all, chip * NA, NA, axis=1))
    g_conv_sh = lax.dynamic_slice_in_dim(tot[1:1 + CONV_K], chip * CW, CW, axis=1)
    loss = tot[5, 2 * D + HD + 2 * H]

    g_in = jnp.concatenate(g_secs[:6] + [g_small[:, :2 * H]] + g_secs[6:], axis=1)
    g_full = [jnp.stack([g_in[:, j * NI:(j + 1) * NI] for j in range(4)]),
              g_w_psb.reshape(4, D // 4, D), g_w_pg.reshape(4, D // 4, D), g_w_out.reshape(4, D // 4, D)]
    g_sib = _swap_partial_halves(g_full)
    parts = [_add_halves(g, r, pc, "add_halves_%d" % i) for i, (g, r) in enumerate(zip(g_full, g_sib))]
    got = _chip_scatter([p[0] for p in parts], [p[1] for p in parts])
    halves = [_sum_chips(q, pc, "sum_chips_%d" % i) for i, q in enumerate(got)]
    g_red = [f.reshape(f.shape[1] * 2, f.shape[2]) for f in _swap_final_halves(halves)]

    out = {}

    def upd(name, w, g, m, v, shape):
        d_, m_, v_ = _adamw(w, g, m, v, "adamw_" + name)
        out[name] = (g.reshape(shape), d_.reshape(shape), m_.reshape(shape), v_.reshape(shape))

    upd("w_ada", w_ada[0], g_w_ada, m_w_ada[0], v_w_ada[0], w_ada.shape)
    upd("w_in", w_in[0], g_red[0], m_w_in[0], v_w_in[0], w_in.shape)
    upd("gdn_conv_w", gdn_conv_w[0], g_conv_sh, m_gdn_conv_w[0], v_gdn_conv_w[0], gdn_conv_w.shape)
    upd("w_proj_sb", w_proj_sb[0], g_red[1], m_w_proj_sb[0], v_w_proj_sb[0], w_proj_sb.shape)
    upd("w_proj_gdn", w_proj_gdn[0], g_red[2], m_w_proj_gdn[0], v_w_proj_gdn[0], w_proj_gdn.shape)
    upd("w_out", w_out[0], g_red[3], m_w_out[0], v_w_out[0], w_out.shape)

    def packs(b, nw, fnw, gnw, a, dt):
        row = jnp.concatenate([nw, fnw.reshape(1, D), gnw, a, dt], axis=1)
        return jnp.concatenate([b, _pad_cols(row, PW), jnp.zeros((6, PW), F32)], axis=0)

    g_pack = jnp.concatenate([tot[0:1], tot[5:6], jnp.zeros((6, PW), F32)], axis=0)
    d_, m_, v_ = _adamw(packs(b_ada, norm_w, final_norm_w, gdn_norm_w, gdn_a_log, gdn_dt_bias), g_pack,
                        packs(m_b_ada, m_norm_w, m_final_norm_w, m_gdn_norm_w, m_gdn_a_log, m_gdn_dt_bias),
                        packs(v_b_ada, v_norm_w, v_final_norm_w, v_gdn_norm_w, v_gdn_a_log, v_gdn_dt_bias),
                        "adamw_small")
    offs = {"norm_w": (0, D, (1, D)), "final_norm_w": (D, D, (D,)), "gdn_norm_w": (2 * D, HD, (1, HD)),
            "gdn_a_log": (2 * D + HD, H, (1, H)), "gdn_dt_bias": (2 * D + HD + H, H, (1, H))}
    out["b_ada"] = tuple(a[0:1] for a in (g_pack, d_, m_, v_))
    for name, (o, n_, shp) in offs.items():
        out[name] = tuple(a[1, o:o + n_].reshape(shp) for a in (g_pack, d_, m_, v_))

    names = ['w_ada', 'b_ada', 'norm_w', 'w_in', 'gdn_conv_w', 'gdn_a_log', 'gdn_dt_bias', 'gdn_norm_w',
             'w_proj_sb', 'w_proj_gdn', 'w_out', 'final_norm_w']
    return (loss, grad_x.reshape(x.shape), *[out[n][0] for n in names], *[out[n][1] for n in names],
            *[out[n][2] for n in names], *[out[n][3] for n in names])
```

```python
import functools

import jax
import jax.numpy as jnp
from jax import lax
from jax.experimental import pallas as pl
from jax.experimental.pallas import tpu as pltpu

F32 = jnp.float32
BF16 = jnp.bfloat16
HD = 128
CH = 64
CONV_K = 4
NORM_EPS = 1e-6
L2_EPS = 1e-6
ADAM_LR, ADAM_B1, ADAM_B2, ADAM_EPS, ADAM_WD, ADAM_STEP = 0.001, 0.9, 0.999, 1e-08, 0.01, 10

VMEM_LIMIT = 56 * 1024 * 1024
SB_BQ, SB_BK = 512, 256
SB_DEAD = -110.0
GDN_TB = 256
GDN_G = 8
EW_TM = 256
HEAD_TM = 512
CONV_CW = 512
MM_TM, MM_TN, MM_TK = 1024, 1024, 1024
MM_TK_MANY = 512
SWAP_CHUNKS = 4
MESH = pl.DeviceIdType.MESH

NN = (((1,), (0,)), ((), ()))
NT = (((1,), (1,)), ((), ()))
TN = (((0,), (0,)), ((), ()))


def _pcall(body, **kw):
    return pl.pallas_call(body, **kw)


def _cparams(sem=None):
    return pltpu.CompilerParams(dimension_semantics=sem, vmem_limit_bytes=VMEM_LIMIT)


def _dot(a, b, dims=NN):
    return lax.dot_general(a, b, dims, preferred_element_type=F32)


def _dotf(a, b, dims=NN):
    return lax.dot_general(a, b, dims, precision=lax.Precision.HIGHEST, preferred_element_type=F32)


def _bdot_make(dims, da_rule, db_rule):
    @jax.custom_vjp
    def f(a, b):
        return _dot(a.astype(BF16), b.astype(BF16), dims)

    def fwd(a, b):
        return f(a, b), (a, b)

    def bwd(res, g):
        a, b = res
        return da_rule(g, a, b), db_rule(g, a, b)

    f.defvjp(fwd, bwd)
    return f


def _rdot(a, b, dims):
    return _dot(a.astype(BF16), b.astype(BF16), dims)


NNB =(((2,), (1,)), ((0,), (0,)))
NTB = (((2,), (2,)), ((0,), (0,)))
TNB = (((1,), (1,)), ((0,), (0,)))
_bdot_nn = _bdot_make(NNB, lambda g, a, b: _rdot(g, b, NTB), lambda g, a, b: _rdot(a, g, TNB))
_bdot_nt = _bdot_make(NTB, lambda g, a, b: _rdot(g, b, NNB), lambda g, a, b: _rdot(g, a, TNB))
_bdot_tn = _bdot_make(TNB, lambda g, a, b: _rdot(b, g, NTB), lambda g, a, b: _rdot(a, g, NNB))


def _iota(shape, axis):
    return lax.broadcasted_iota(jnp.int32, shape, axis)


def _sigmoid(x):
    e = jnp.exp(-jnp.abs(x))
    return jnp.where(x >= 0, 1.0 / (1.0 + e), e / (1.0 + e))


def _silu(x):
    return x * _sigmoid(x)


def _softplus(x):
    return jnp.maximum(x, 0.0) + jnp.log(1.0 + jnp.exp(-jnp.abs(x)))


def _rms(x, w):
    return x * lax.rsqrt(jnp.mean(x * x, axis=-1, keepdims=True) + NORM_EPS) * w


def _ew(fn, grid, ins, in_specs, outs, out_specs, acc=None, name=None):
    n_in = len(ins)
    acc = acc or [None] * len(outs)

    def body(*refs):
        vals = fn(*[r[...] for r in refs[:n_in]])
        if not isinstance(vals, (tuple, list)):
            vals = (vals,)
        for r, v, a in zip(refs[n_in:], vals, acc):
            if a is None:
                r[...] = v.astype(r.dtype)
                continue
            first = pl.program_id(len(grid) - 1) == 0
            if a == 'all':
                for ax in range(len(grid) - 1):
                    first = jnp.logical_and(first, pl.program_id(ax) == 0)

            @pl.when(first)
            def _():
                r[...] = v.astype(r.dtype)

            @pl.when(jnp.logical_not(first))
            def _():
                r[...] += v.astype(r.dtype)

    res = _pcall(body, grid=grid, in_specs=in_specs, out_specs=out_specs,
                 out_shape=[jax.ShapeDtypeStruct(s, d) for s, d in outs],
                 compiler_params=_cparams(("arbitrary",) * len(grid)), name=name)(*ins)
    return res


def _mm(pairs, mode, out_dtype, name, add=None, tm=None, tn=None, tk=None):
    a0, b0 = pairs[0][0], pairs[0][1]
    M = a0.shape[1] if mode == 'tn' else a0.shape[0]
    N = b0.shape[0] if mode == 'nt' else b0.shape[1]
    tm = min(tm or MM_TM, M)
    tn = min(tn or MM_TN, N)
    tk = tk or MM_TK
    tks = [min(tk, p[4]) for p in pairs]
    nks = [p[4] // t for p, t in zip(pairs, tks)]
    offs = [sum(nks[:i]) for i in range(len(pairs))]
    total = sum(nks)
    assert M % tm == 0 and N % tn == 0 and all(p[4] % t == 0 for p, t in zip(pairs, tks)), (name, M, N)
    dims = {'nn': NN, 'nt': NT, 'tn': TN}[mode]

    in_specs, ins = [], []
    for (a, b, ao, bo, kl), t, nk, off in zip(pairs, tks, nks, offs):
        def kidx(kk, nk=nk, off=off):
            return jnp.minimum(jnp.maximum(kk - off, 0), nk - 1)
        if mode == 'tn':
            in_specs.append(pl.BlockSpec((t, tm), lambda i, j, kk, f=kidx, o=ao // t: (o + f(kk), i)))
        else:
            in_specs.append(pl.BlockSpec((tm, t), lambda i, j, kk, f=kidx, o=ao // t: (i, o + f(kk))))
        if mode == 'nt':
            in_specs.append(pl.BlockSpec((tn, t), lambda i, j, kk, f=kidx, o=bo // t: (j, o + f(kk))))
        else:
            in_specs.append(pl.BlockSpec((t, tn), lambda i, j, kk, f=kidx, o=bo // t: (o + f(kk), j)))
        ins += [a, b]
    if add is not None:
        in_specs.append(pl.BlockSpec((tm, tn), lambda i, j, kk: (i, j)))
        ins.append(add)
    npairs = len(pairs)

    def body(*refs):
        out_ref, acc_ref = refs[-2], refs[-1]
        kk = pl.program_id(2)

        @pl.when(kk == 0)
        def _():
            acc_ref[...] = jnp.zeros_like(acc_ref)

        for p in range(npairs):
            def upd(p=p):
                acc_ref[...] += _dot(refs[2 * p][...].astype(BF16), refs[2 * p + 1][...].astype(BF16), dims)
            if npairs == 1:
                upd()
            else:
                pl.when(jnp.logical_and(kk >= offs[p], kk < offs[p] + nks[p]))(upd)

        @pl.when(kk == total - 1)
        def _():
            r = acc_ref[...]
            if add is not None:
                r = r + refs[2 * npairs][...]
            out_ref[...] = r.astype(out_ref.dtype)

    return _pcall(body, grid=(M // tm, N // tn, total), in_specs=in_specs,
                  out_specs=pl.BlockSpec((tm, tn), lambda i, j, kk: (i, j)),
                  out_shape=jax.ShapeDtypeStruct((M, N), out_dtype),
                  scratch_shapes=[pltpu.VMEM((tm, tn), F32)],
                  compiler_params=_cparams(("parallel", "parallel", "arbitrary")), name=name)(*ins)


def _mm1(a, b, mode, out_dtype, name, **kw):
    k = a.shape[0] if mode == 'tn' else a.shape[1]
    return _mm([(a, b, 0, 0, k)], mode, out_dtype, name, **kw)


def _sb_tiles(T):
    bq = min(SB_BQ, T)
    bk = min(SB_BK, bq)
    return bq, bk, bq // bk


def _sb_block(i, j, masked, q, k_ref, v_ref, accL, bq, bk, UU):
    r = pl.ds(pl.multiple_of(j * bk, bk), bk)
    kj = k_ref[r, :]
    vj = v_ref[r, :]
    z = _dot(q, kj, NT) * (HD ** -0.5)
    lb = jnp.minimum(z, 0.0) - jnp.log(1.0 + jnp.exp(-jnp.abs(z)))
    ln = lb - z
    mask = None
    if masked:
        mask = (j * bk + _iota((bq, bk), 1)) < (i * bq + _iota((bq, bk), 0))
        ln = jnp.where(mask, ln, 0.0)
    hi = ln.astype(BF16)
    lo = (ln - hi.astype(F32)).astype(BF16)
    later = _dot(jnp.concatenate([hi, lo], axis=1), UU) + accL
    a = jnp.exp(lb + later)
    if masked:
        a = jnp.where(mask, a, 0.0)
    return r, kj, vj, mask, ln, lb, a


def _sb_fwd(proj, T, H):
    bq, bk, nd = _sb_tiles(T)

    assert T // bk < HD

    def body(q_ref, kf_ref, vf_ref, o_ref, l_ref, k_ref, v_ref):
        i = pl.program_id(1)

        @pl.when(i == 0)
        def _():
            k_ref[...] = kf_ref[...].astype(BF16)
            v_ref[...] = vf_ref[...].astype(BF16)

        q = q_ref[...].astype(BF16)
        U = (_iota((bk, bk), 0) > _iota((bk, bk), 1)).astype(BF16)
        UU = jnp.concatenate([U, U], axis=0)
        lane = _iota((bq, HD), 1)

        def blk(j, carry, masked):
            acc, accL, saved = carry
            _, _, vj, _, ln, _, a = _sb_block(i, j, masked, q, k_ref, v_ref, accL, bq, bk, UU)
            acc = acc + _dot(a.astype(BF16), vj)
            return acc, accL + jnp.sum(ln, axis=1, keepdims=True), jnp.where(lane == j, accL, saved)

        carry = (jnp.zeros((bq, HD), F32), jnp.zeros((bq, 1), F32), jnp.zeros((bq, HD), F32))
        for jj in range(nd):
            carry = blk(i * nd + (nd - 1 - jj), carry, True)
        def more(c):
            return jnp.logical_and(c[0] < i * nd, jnp.max(c[2]) > SB_DEAD)

        done = lax.while_loop(more, lambda c: (c[0] + 1,) + blk(i * nd - 1 - c[0], c[1:], False),
                              (jnp.int32(0),) + carry)
        o_ref[...] = done[1]
        l_ref[...] = jnp.where(lane == HD - 1, done[0].astype(F32), done[3])

    blk_q = pl.BlockSpec((bq, HD), lambda h, i: (i, h))
    return _pcall(
        body, grid=(H, T // bq),
        in_specs=[blk_q, pl.BlockSpec((T, HD), lambda h, i: (0, H + h)),
                  pl.BlockSpec((T, HD), lambda h, i: (0, 2 * H + h))],
        out_specs=[blk_q, blk_q],
        out_shape=[jax.ShapeDtypeStruct((T, H * HD), F32)] * 2,
        scratch_shapes=[pltpu.VMEM((T, HD), BF16), pltpu.VMEM((T, HD), BF16)],
        compiler_params=_cparams(("parallel", "arbitrary")), name="sb_fwd")(proj, proj, proj)


def _sb_bwd(proj, l_sb, do_sb, T, H):
    bq, bk, nd = _sb_tiles(T)
    nq = T // bq

    def body(q_ref, kf_ref, vf_ref, l_ref, do_ref, dq_ref, dk_ref, dv_ref, dk_acc, dv_acc, k_ref, v_ref):
        i = pl.program_id(1)

        @pl.when(i == 0)
        def _():
            dk_acc[...] = jnp.zeros_like(dk_acc)
            dv_acc[...] = jnp.zeros_like(dv_acc)
            k_ref[...] = kf_ref[...].astype(BF16)
            v_ref[...] = vf_ref[...].astype(BF16)

        q = q_ref[...].astype(BF16)
        dob = do_ref[...].astype(BF16)
        saved = l_ref[...]
        lane = _iota((bq, HD), 1)
        U = (_iota((bk, bk), 0) > _iota((bk, bk), 1)).astype(BF16)
        UU = jnp.concatenate([U, U], axis=0)
        Ue = (_iota((bk, bk), 0) < _iota((bk, bk), 1)).astype(BF16)

        def blk(j, carry, masked):
            dq, accP = carry
            accL = jnp.sum(jnp.where(lane == j, saved, 0.0), axis=1, keepdims=True)
            r, kj, vj, mask, _, lb, a = _sb_block(i, j, masked, q, k_ref, v_ref, accL, bq, bk, UU)
            p = a * _dot(dob, vj, NT)
            pre = _dot(p.astype(BF16), Ue) + accP
            sig = jnp.exp(lb)
            dz = p * (1.0 - sig) - pre * sig
            if masked:
                dz = jnp.where(mask, dz, 0.0)
            dzb = (dz * (HD ** -0.5)).astype(BF16)
            dq = dq + _dot(dzb, kj)
            dk_acc[r, :] += _dot(dzb, q, TN)
            dv_acc[r, :] += _dot(a.astype(BF16), dob, TN)
            return dq, accP + jnp.sum(p, axis=1, keepdims=True)

        carry = (jnp.zeros((bq, HD), F32), jnp.zeros((bq, 1), F32))
        walked = jnp.max(jnp.where(lane == HD - 1, saved, 0.0)).astype(jnp.int32)
        carry = lax.fori_loop(i * nd - walked, i * nd, lambda j, c: blk(j, c, False), carry)
        for jj in range(nd):
            carry = blk(i * nd + jj, carry, True)
        dq_ref[...] = carry[0].astype(BF16)

        @pl.when(i == nq - 1)
        def _():
            dk_ref[...] = dk_acc[...].astype(BF16)
            dv_ref[...] = dv_acc[...].astype(BF16)

    W = H * HD
    blk_q = pl.BlockSpec((bq, HD), lambda h, i: (i, h))
    blk_t = pl.BlockSpec((T, HD), lambda h, i: (0, h))
    return _pcall(
        body, grid=(H, nq),
        in_specs=[blk_q, pl.BlockSpec((T, HD), lambda h, i: (0, H + h)),
                  pl.BlockSpec((T, HD), lambda h, i: (0, 2 * H + h)), blk_q, blk_q],
        out_specs=[blk_q, blk_t, blk_t],
        out_shape=[jax.ShapeDtypeStruct((T, W), BF16)] * 3,
        scratch_shapes=[pltpu.VMEM((T, HD), F32), pltpu.VMEM((T, HD), F32),
                        pltpu.VMEM((T, HD), BF16), pltpu.VMEM((T, HD), BF16)],
        compiler_params=_cparams(("parallel", "arbitrary")), name="sb_bwd")(proj, proj, proj, l_sb, do_sb)


@jax.custom_vjp
def _unit_lower_inv(lw):
    C = lw.shape[1]
    x = (_iota(lw.shape, 1) == _iota(lw.shape, 2)).astype(F32) - lw
    pw = lw
    for _ in range(C.bit_length() - 2):
        pw = _dotf(pw, pw, NNB)
        x = x + _dotf(x, pw, NNB)
    return x


_unit_lower_inv.defvjp(lambda lw: (_unit_lower_inv(lw),) * 2,
                       lambda x, g: (-_dotf(_dotf(x, g, TNB), x, NTB),))


def _gdn_chunk(q, k, v, bb, gb, S):
    G, C = q.shape[0], q.shape[1]
    ri, ci = _iota((G, C, C), 1), _iota((G, C, C), 2)
    tril, strict = ri >= ci, ri > ci
    g_col = gb[:, :, :C]
    g_row = jnp.swapaxes(g_col, 1, 2)
    decay = jnp.where(tril, jnp.exp(jnp.where(tril, g_col - g_row, 0.0)), 0.0)
    eg = jnp.exp(gb)
    qs = q * (HD ** -0.5)
    kb = k * bb
    lw = jnp.where(strict, _bdot_nt(kb, k) * decay, 0.0)
    x = _unit_lower_inv(lw)
    u = _bdot_nn(x, v * bb)
    w = _bdot_nn(x, kb * eg)
    aq = jnp.where(tril, _bdot_nt(qs, k) * decay, 0.0)
    vnew = u - _bdot_nn(w, S)
    o = _bdot_nn(qs * eg, S) + _bdot_nn(aq, vnew)
    g_last = gb[:, C - 1:C, :]
    s_new = S * jnp.exp(g_last) + _bdot_tn(k * jnp.exp(g_last - gb), vnew)
    return o, s_new


def _gdn_fwd(qn, kn, vc, beta_b, g_b, T, H):
    tb = min(GDN_TB, T)
    nc = tb // CH
    G = min(GDN_G, H)

    def body(q_ref, k_ref, v_ref, b_ref, g_ref, o_ref, s_ref, s_scr):
        @pl.when(pl.program_id(1) == 0)
        def _():
            s_scr[...] = jnp.zeros_like(s_scr)

        def step(c, carry):
            r = pl.ds(pl.multiple_of(c * CH, CH), CH)
            s = s_scr[...]
            s_ref[:, c] = s
            o, s2 = _gdn_chunk(q_ref[:, r, :], k_ref[:, r, :], v_ref[:, r, :], b_ref[:, r, :], g_ref[:, r, :], s)
            o_ref[:, r, :] = o
            s_scr[...] = s2
            return carry

        lax.fori_loop(0, nc, step, 0)

    blk3 = pl.BlockSpec((G, tb, HD), lambda h, t: (h, t, 0))
    return _pcall(
        body, grid=(H // G, T // tb), in_specs=[blk3] * 5,
        out_specs=[blk3, pl.BlockSpec((G, nc, HD, HD), lambda h, t: (h, t, 0, 0))],
        out_shape=[jax.ShapeDtypeStruct((H, T, HD), F32), jax.ShapeDtypeStruct((H, T // CH, HD, HD), F32)],
        scratch_shapes=[pltpu.VMEM((G, HD, HD), F32)],
        compiler_params=_cparams(("parallel", "arbitrary")), name="gdn_fwd")(qn, kn, vc, beta_b, g_b)


def _gdn_bwd(qn, kn, vc, beta_b, g_b, s_all, do, T, H):
    tb = min(GDN_TB, T)
    nc = tb // CH
    nt = T // tb
    G = min(GDN_G, H)

    def body(q_ref, k_ref, v_ref, b_ref, g_ref, s_ref, do_ref, dq_ref, dk_ref, dv_ref, db_ref, dg_ref, ds_scr):
        @pl.when(pl.program_id(1) == 0)
        def _():
            ds_scr[...] = jnp.zeros_like(ds_scr)

        def step(it, carry):
            c = nc - 1 - it
            r = pl.ds(pl.multiple_of(c * CH, CH), CH)
            _, vjp = jax.vjp(_gdn_chunk, q_ref[:, r, :], k_ref[:, r, :], v_ref[:, r, :], b_ref[:, r, :],
                             g_ref[:, r, :], s_ref[:, c])
            dq, dk, dv, db, dg, ds = vjp((do_ref[:, r, :], ds_scr[...]))
            dq_ref[:, r, :] = dq
            dk_ref[:, r, :] = dk
            dv_ref[:, r, :] = dv
            db_ref[:, r, :] = db
            dg_ref[:, r, :] = dg
            ds_scr[...] = ds
            return carry

        lax.fori_loop(0, nc, step, 0)

    blk3 = pl.BlockSpec((G, tb, HD), lambda h, t: (h, nt - 1 - t, 0))
    return _pcall(
        body, grid=(H // G, nt),
        in_specs=[blk3] * 5 + [pl.BlockSpec((G, nc, HD, HD), lambda h, t: (h, nt - 1 - t, 0, 0)), blk3],
        out_specs=[blk3] * 5,
        out_shape=[jax.ShapeDtypeStruct((H, T, HD), F32)] * 5,
        scratch_shapes=[pltpu.VMEM((G, HD, HD), F32)],
        compiler_params=_cparams(("parallel", "arbitrary")), name="gdn_bwd")(qn, kn, vc, beta_b, g_b, s_all, do)


def _conv_tiles(T, C):
    return min(HEAD_TM, T), min(CONV_CW, C)


def _shift_down(main, halo, first, d):
    halo = jnp.where(first, 0.0, halo)
    ext = jnp.concatenate([halo, main], axis=0)
    return pltpu.roll(ext, d, 0)[8:]


def _conv_fwd(proj, conv_w, T, H):
    W = H * HD
    C = 3 * W
    tm, cw = _conv_tiles(T, C)
    col0 = 4 * W // cw

    def body(x_ref, h_ref, w_ref, o_ref):
        first = pl.program_id(1) == 0
        main = x_ref[...]
        halo = h_ref[...]
        w = w_ref[...]
        out = w[CONV_K - 1:CONV_K, :] * main
        for d in range(1, CONV_K):
            out = out + w[CONV_K - 1 - d:CONV_K - d, :] * _shift_down(main, halo, first, d)
        o_ref[...] = out

    return _pcall(
        body, grid=(C // cw, T // tm),
        in_specs=[pl.BlockSpec((tm, cw), lambda cb, i: (i, col0 + cb)),
                  pl.BlockSpec((8, cw), lambda cb, i: (jnp.maximum(i * (tm // 8) - 1, 0), col0 + cb)),
                  pl.BlockSpec((CONV_K, cw), lambda cb, i: (0, cb))],
        out_specs=pl.BlockSpec((tm, cw), lambda cb, i: (i, cb)),
        out_shape=jax.ShapeDtypeStruct((T, C), F32),
        compiler_params=_cparams(("parallel", "arbitrary")), name="conv_fwd")(proj, proj, conv_w)


def _conv_bwd(proj, conv_w, dconv, T, H):
    W = H * HD
    C = 3 * W
    tm, cw = _conv_tiles(T, W)
    col0 = 4 * W // cw
    nbw = W // cw
    nt = T // tm

    def body(x_ref, h_ref, w_ref, d_ref, dn_ref, dx_ref, dw_ref):
        i = pl.program_id(1)
        first = i == 0
        main = x_ref[...]
        halo = h_ref[...]
        w = w_ref[...]
        dmain = d_ref[...]
        dnext = jnp.where(i == nt - 1, 0.0, dn_ref[...])
        dext = jnp.concatenate([dmain, dnext], axis=0)
        dx = w[CONV_K - 1:CONV_K, :] * dmain
        rows = [jnp.sum(dmain * main, axis=0, keepdims=True)]
        for d in range(1, CONV_K):
            dx = dx + w[CONV_K - 1 - d:CONV_K - d, :] * pltpu.roll(dext, tm + 8 - d, 0)[:tm]
            rows.append(jnp.sum(dmain * _shift_down(main, halo, first, d), axis=0, keepdims=True))
        dx_ref[...] = dx.astype(BF16)

        @pl.when(first)
        def _():
            dw_ref[...] = jnp.zeros_like(dw_ref)

        for d in range(CONV_K):
            dw_ref[CONV_K - 1 - d:CONV_K - d, :] += rows[d]

    return _pcall(
        body, grid=(C // cw, nt),
        in_specs=[pl.BlockSpec((tm, cw), lambda cb, i: (i, col0 + cb)),
                  pl.BlockSpec((8, cw), lambda cb, i: (jnp.maximum(i * (tm // 8) - 1, 0), col0 + cb)),
                  pl.BlockSpec((CONV_K, cw), lambda cb, i: (0, cb)),
                  pl.BlockSpec((None, tm, cw), lambda cb, i: (cb // nbw, i, cb % nbw)),
                  pl.BlockSpec((None, 8, cw),
                               lambda cb, i: (cb // nbw, jnp.minimum((i + 1) * (tm // 8), T // 8 - 1), cb % nbw))],
        out_specs=[pl.BlockSpec((tm, cw), lambda cb, i: (i, cb)), pl.BlockSpec((CONV_K, cw), lambda cb, i: (0, cb))],
        out_shape=[jax.ShapeDtypeStruct((T, C), BF16), jax.ShapeDtypeStruct((CONV_K, C), F32)],
        compiler_params=_cparams(("parallel", "arbitrary")), name="conv_bwd")(proj, proj, conv_w, dconv, dconv)


def _prep_fn(H, tm):
    def fn(cq, ck, cv, small, al, dtb):
        h = pl.program_id(1)
        lane = _iota((HD, HD), 0)
        oh_b = (lane == h).astype(F32)
        oh_a = (lane == H + h).astype(F32)
        q, k, v = _silu(cq), _silu(ck), _silu(cv)
        qn = q * lax.rsqrt(jnp.sum(q * q, axis=-1, keepdims=True) + L2_EPS)
        kn = k * lax.rsqrt(jnp.sum(k * k, axis=-1, keepdims=True) + L2_EPS)
        beta = _dotf(_sigmoid(small), oh_b)
        g = _dotf(-jnp.exp(al) * _softplus(small + dtb), oh_a)
        nch = tm // CH
        tri = (_iota((nch, CH, CH), 1) >= _iota((nch, CH, CH), 2)).astype(F32)
        gc = _dotf(tri, g.reshape(nch, CH, HD), NNB).reshape(tm, HD)
        return qn, kn, v, beta, gc
    return fn


def _prep_specs(T, H):
    tm = min(HEAD_TM, T)
    W = H * HD
    row = lambda s: pl.BlockSpec((tm, HD), lambda i, h, s=s: (i, s * H + h))
    small = pl.BlockSpec((tm, HD), lambda i, h: (i, 0))
    par = pl.BlockSpec((1, HD), lambda i, h: (0, 0))
    blk = pl.BlockSpec((tm, HD), lambda i, h: (i, h))
    blk3 = pl.BlockSpec((None, tm, HD), lambda i, h: (h, i, 0))
    return tm, W, row, small, par, blk, blk3


def _prep_fwd(conv, small, al, dtb, T, H):
    tm, W, row, sm, par, blk, blk3 = _prep_specs(T, H)
    return _ew(_prep_fn(H, tm), (T // tm, H), [conv, conv, conv, small, al, dtb],
               [row(0), row(1), row(2), sm, par, par],
               [((H, T, HD), F32)] * 5, [blk3] * 5, name="gdn_prep")


def _prep_bwd(conv, small, al, dtb, dqn, dkn, dvc, dbeta, dg, T, H):
    tm, W, row, sm, par, blk, blk3 = _prep_specs(T, H)
    f = _prep_fn(H, tm)

    def fn(cq, ck, cv, small_, al_, dtb_, a, b, c, d, e):
        _, vjp = jax.vjp(f, cq, ck, cv, small_, al_, dtb_)
        dq_, dk_, dv_, dsmall, dal, ddtb = vjp((a, b, c, d, e))
        return jnp.stack([dq_, dk_, dv_]), dsmall, dal, ddtb

    return _ew(fn, (T // tm, H), [conv, conv, conv, small, al, dtb, dqn, dkn, dvc, dbeta, dg],
               [row(0), row(1), row(2), sm, par, par] + [blk3] * 5,
               [((3, T, W), F32), ((T, HD), F32), ((1, HD), F32), ((1, HD), F32)],
               [pl.BlockSpec((3, tm, HD), lambda i, h: (0, i, h)), sm, par, par],
               acc=[None, 'inner', 'all', 'all'], name="gdn_prep_bwd")


def _gate_fn(o_sb, z_sb, o_g, z_g, gnw):
    a_sb = o_sb * _silu(z_sb)
    a_g = _rms(o_g, gnw) * _silu(z_g)
    return a_sb, a_g


def _gate_specs(T, H):
    tm = min(HEAD_TM, T)
    blk = pl.BlockSpec((tm, HD), lambda i, h: (i, h))
    sec = lambda s: pl.BlockSpec((tm, HD), lambda i, h, s=s: (i, s * H + h))
    par = pl.BlockSpec((1, HD), lambda i, h: (0, 0))
    blk3 = pl.BlockSpec((None, tm, HD), lambda i, h: (h, i, 0))
    return tm, blk, blk3, sec, par


def _gate_fwd(o_sb, proj, o_g, gnw, T, H):
    tm, blk, blk3, sec, par = _gate_specs(T, H)
    W = H * HD
    return _ew(_gate_fn, (T // tm, H), [o_sb, proj, o_g, proj, gnw], [blk, sec(3), blk3, sec(7), par],
               [((T, W), BF16)] * 2, [blk, blk], name="gate_fwd")


def _gate_bwd(o_sb, proj, o_g, gnw, da_sb, da_g, T, H):
    tm, blk, blk3, sec, par = _gate_specs(T, H)
    W = H * HD

    def fn(o_sb_, z_sb, o_g_, z_g, gnw_, da, db):
        _, vjp = jax.vjp(_gate_fn, o_sb_, z_sb, o_g_, z_g, gnw_)
        return vjp((da, db))

    return _ew(fn, (T // tm, H), [o_sb, proj, o_g, proj, gnw, da_sb, da_g], [blk, sec(3), blk3, sec(7), par, blk, blk],
               [((T, W), F32), ((T, W), BF16), ((H, T, HD), F32), ((T, W), BF16), ((1, HD), F32)],
               [blk, blk, blk3, blk, par], acc=[None, None, None, None, 'all'], name="gate_bwd")


def _ada_fn(x, nw, scale, shift):
    return _rms(x, nw) * (1.0 + scale) + shift


def _row_specs(T, D):
    tm = min(EW_TM, T)
    return tm, pl.BlockSpec((tm, D), lambda i: (i, 0)), pl.BlockSpec((1, D), lambda i: (0, 0))


def _ada_fwd(x, nw, scale, shift):
    T, D = x.shape
    tm, row, par = _row_specs(T, D)
    return _ew(_ada_fn, (T // tm,), [x, nw, scale, shift], [row, par, par, par], [((T, D), BF16)], [row],
               name="ada_norm")[0]


def _ada_bwd(x, nw, scale, shift, dh, dx2):
    T, D = x.shape
    tm, row, par = _row_specs(T, D)

    def fn(x_, nw_, sc_, sh_, dh_, dx2_):
        _, vjp = jax.vjp(_ada_fn, x_, nw_, sc_, sh_)
        dx, dnw, dsc, dsh = vjp(dh_)
        return dx + dx2_, dnw, dsc, dsh

    return _ew(fn, (T // tm,), [x, nw, scale, shift, dh, dx2], [row, par, par, par, row, row],
               [((T, D), F32)] + [((1, D), F32)] * 3, [row, par, par, par], acc=[None, 'all', 'all', 'all'],
               name="ada_norm_bwd")


def _merge_fn(m_sb, m_g, p_sb, p_g):
    return _sigmoid(m_sb) * p_sb + _sigmoid(m_g) * p_g


def _merge_specs(T, D, H):
    tm, tc = min(HEAD_TM, T), min(512, D)
    nb = D // tc
    blk = pl.BlockSpec((tm, tc), lambda i, j: (i, j))
    sec = lambda s: pl.BlockSpec((tm, tc), lambda i, j, s=s: (i, s * nb + j))
    return tm, tc, blk, sec


def _merge_fwd(proj, p_sb, p_g, T, D, H):
    tm, tc, blk, sec = _merge_specs(T, D, H)
    return _ew(_merge_fn, (T // tm, D // tc), [proj, proj, p_sb, p_g], [sec(8), sec(9), blk, blk],
               [((T, D), BF16)], [blk], name="merge")[0]


def _merge_bwd(proj, p_sb, p_g, dy, T, D, H):
    tm, tc, blk, sec = _merge_specs(T, D, H)

    def fn(m_sb, m_g, p_sb_, p_g_, dy_):
        _, vjp = jax.vjp(_merge_fn, m_sb, m_g, p_sb_, p_g_)
        return vjp(dy_)

    return _ew(fn, (T // tm, D // tc), [proj, proj, p_sb, p_g, dy], [sec(8), sec(9), blk, blk, blk],
               [((T, D), BF16)] * 4, [blk] * 4, name="merge_bwd")


def _loss_head(x, u, gate, fnw, tgt):
    T, D = x.shape
    tm, row, par = _row_specs(T, D)

    def loss(x_, u_, gate_, fnw_, tgt_):
        y = _rms(x_ + gate_ * u_, fnw_)
        return 0.5 * jnp.sum(jnp.mean(jnp.square(y - tgt_), axis=-1))

    def fn(x_, u_, gate_, fnw_, tgt_):
        val, (dx, du, dgate, dfnw) = jax.value_and_grad(loss, argnums=(0, 1, 2, 3))(x_, u_, gate_, fnw_, tgt_)
        return jnp.full((1, HD), val, F32), dx, du, dgate, dfnw

    return _ew(fn, (T // tm,), [x, u, gate, fnw, tgt], [row, row, par, par, row],
               [((1, HD), F32), ((T, D), F32), ((T, D), BF16), ((1, D), F32), ((1, D), F32)],
               [pl.BlockSpec((1, HD), lambda i: (0, 0)), row, row, par, par],
               acc=['all', None, None, 'all', 'all'], name="loss_head")


def _mod_part(c_all, w_ada, b_ada):
    D, N = w_ada.shape
    tn = min(512, N)

    def fn(c, w, b):
        return _dot(_silu(c).astype(BF16), w.astype(BF16)) + b

    return _ew(fn, (N // tn,), [c_all, w_ada, b_ada],
               [pl.BlockSpec((8, D), lambda j: (0, 0)), pl.BlockSpec((D, tn), lambda j: (0, j)),
                pl.BlockSpec((1, tn), lambda j: (0, j))],
               [((8, N), F32)], [pl.BlockSpec((8, tn), lambda j: (0, j))], name="ada_mod")[0]


def _w_ada_grad(c_all, dmod):
    D, N = c_all.shape[1], dmod.shape[1]
    tn = min(512, N)

    def fn(c, dm):
        return _dot(_silu(c).astype(BF16), dm.astype(BF16), TN)

    return _ew(fn, (N // tn,), [c_all, dmod],
               [pl.BlockSpec((8, D), lambda j: (0, 0)), pl.BlockSpec((8, tn), lambda j: (0, j))],
               [((D, N), F32)], [pl.BlockSpec((D, tn), lambda j: (0, j))], name="w_ada_grad")[0]


def _sum8(packs):
    N = packs.shape[1]

    def fn(p):
        s = p[0:8]
        for d in range(1, 8):
            s = s + p[8 * d:8 * d + 8]
        return s

    return _ew(fn, (1,), [packs], [pl.BlockSpec((64, N), lambda i: (0, 0))], [((8, N), F32)],
               [pl.BlockSpec((8, N), lambda i: (0, 0))], name="sum_devices")[0]


def _sum_chips(q, c, name):
    _, R, C = q.shape
    tr = min(64, R)

    def body(c_ref, q_ref, o_ref):
        p = q_ref[...].astype(F32)
        o_ref[...] = (p[0] + p[1]) + (p[2] + p[3])

    grid_spec = pltpu.PrefetchScalarGridSpec(
        num_scalar_prefetch=1, grid=(R // tr,),
        in_specs=[pl.BlockSpec((4, tr, C), lambda i, c_ref: (0, i, 0))],
        out_specs=pl.BlockSpec((None, tr, C), lambda i, c_ref: (c_ref[0], i, 0)))
    return _pcall(body, grid_spec=grid_spec, out_shape=jax.ShapeDtypeStruct((2, R, C), F32),
                  compiler_params=_cparams(("arbitrary",)), name=name)(jnp.reshape(c, (1,)).astype(jnp.int32), q)


def _add_halves(g, rsib, c, name):
    _, R, C = g.shape
    rh = R // 2
    tr = min(64, rh)
    nb = rh // tr

    def body(c_ref, a_ref, b_ref, o_ref, o2_ref):
        v = (a_ref[...] + b_ref[...]).astype(o_ref.dtype)
        o_ref[...] = v
        o2_ref[...] = v

    spec = pl.BlockSpec((None, tr, C), lambda j, i, c_ref: (j, i, 0))
    grid_spec = pltpu.PrefetchScalarGridSpec(
        num_scalar_prefetch=1, grid=(4, nb),
        in_specs=[pl.BlockSpec((None, tr, C), lambda j, i, c_ref: (j, c_ref[0] * nb + i, 0)), spec],
        out_specs=[spec, spec])
    return _pcall(body, grid_spec=grid_spec, out_shape=[jax.ShapeDtypeStruct((4, rh, C), BF16)] * 2,
                  compiler_params=_cparams(("arbitrary", "arbitrary")), name=name)(
                      jnp.reshape(c, (1,)).astype(jnp.int32), g, rsib)


def _adamw(w, g, m, v, name):
    R, C = w.shape
    tr = R if R <= 64 else 64
    blk = pl.BlockSpec((tr, C), lambda i: (i, 0))

    def fn(w_, g_, m_, v_):
        m2 = ADAM_B1 * m_ + (1.0 - ADAM_B1) * g_
        v2 = ADAM_B2 * v_ + (1.0 - ADAM_B2) * jnp.square(g_)
        m_hat = m2 / (1.0 - ADAM_B1 ** ADAM_STEP)
        v_hat = v2 / (1.0 - ADAM_B2 ** ADAM_STEP)
        delta = -ADAM_LR * (m_hat / (jnp.sqrt(v_hat) + ADAM_EPS) + ADAM_WD * w_)
        return delta, m2, v2

    return _ew(fn, (R // tr,), [w, g, m, v], [blk] * 4, [((R, C), F32)] * 3, [blk] * 3, name=name)


def _place():
    x, y, c = lax.axis_index("x"), lax.axis_index("y"), lax.axis_index("c")
    return x, y, c, [(1 - x, y), (x, 1 - y), (1 - x, 1 - y)]


ANY = pl.BlockSpec(memory_space=pl.ANY)


def _allgather8(blk):
    m_per, n = blk.shape

    def body(x_ref, out_ref, send_sems, recv_sems, local_sem):
        x, y, c, chips = _place()
        me, sibling = (x, y, c), (x, y, 1 - c)

        def rows(px, py, pc):
            return out_ref.at[pl.ds((4 * px + 2 * py + pc) * m_per, m_per), :]

        def copy(k, block, to, src=None):
            return pltpu.make_async_remote_copy(
                src_ref=rows(*block) if src is None else src, dst_ref=rows(*block),
                send_sem=send_sems.at[k], recv_sem=recv_sems.at[k], device_id=to, device_id_type=MESH)

        mine = pltpu.make_async_copy(x_ref, rows(*me), local_sem)
        mine.start()
        first = [copy(0, me, sibling, src=x_ref)]
        first += [copy(1 + j, me, (*chip, c), src=x_ref) for j, chip in enumerate(chips)]
        for cp in first:
            cp.start()
        passed = [copy(4 + j, (*chip, c), sibling) for j, chip in enumerate(chips)]
        for j, chip in enumerate(chips):
            copy(1 + j, (*chip, c), me).wait_recv()
            passed[j].start()
        copy(0, sibling, me).wait_recv()
        for j, chip in enumerate(chips):
            copy(4 + j, (*chip, 1 - c), me).wait_recv()
        for cp in first + passed:
            cp.wait_send()
        mine.wait()

    vm = pl.BlockSpec(memory_space=pltpu.VMEM)
    return _pcall(body, out_shape=jax.ShapeDtypeStruct((8 * m_per, n), blk.dtype), in_specs=[vm], out_specs=vm,
                  scratch_shapes=[pltpu.SemaphoreType.DMA((7,)), pltpu.SemaphoreType.DMA((7,)),
                                  pltpu.SemaphoreType.DMA],
                  compiler_params=pltpu.CompilerParams(vmem_limit_bytes=VMEM_LIMIT), name="allgather8")(blk)


def _gather_weights(shards):
    n = len(shards)

    def body(*refs):
        ins, outs = refs[:n], refs[n:2 * n]
        send_sems, recv_sems, fsend_sems, frecv_sems = refs[2 * n:]
        x, y, c, chips = _place()
        mychip = 2 * x + y
        sends, fwds = [], []
        for a in range(n):
            rh = shards[a].shape[0] // 2
            for k, chip in enumerate(chips):
                cp = pltpu.make_async_remote_copy(
                    src_ref=ins[a].at[pl.ds(c * rh, rh), :], dst_ref=outs[a].at[mychip, pl.ds(c * rh, rh), :],
                    send_sem=send_sems.at[3 * a + k], recv_sem=recv_sems.at[3 * a + k],
                    device_id=(*chip, c), device_id_type=MESH)
                cp.start()
                sends.append(cp)
        for a in range(n):
            rh = shards[a].shape[0] // 2
            for k, (px, py) in enumerate(chips):
                land = outs[a].at[2 * px + py, pl.ds(c * rh, rh), :]
                pltpu.make_async_remote_copy(
                    src_ref=land, dst_ref=land, send_sem=send_sems.at[3 * a + k], recv_sem=recv_sems.at[3 * a + k],
                    device_id=(px, py, c), device_id_type=MESH).wait_recv()
                fw = pltpu.make_async_remote_copy(
                    src_ref=land, dst_ref=land, send_sem=fsend_sems.at[3 * a + k], recv_sem=frecv_sems.at[3 * a + k],
                    device_id=(x, y, 1 - c), device_id_type=MESH)
                fw.start()
                fwds.append(fw)
        for a in range(n):
            rh = shards[a].shape[0] // 2
            for k, (px, py) in enumerate(chips):
                land = outs[a].at[2 * px + py, pl.ds((1 - c) * rh, rh), :]
                pltpu.make_async_remote_copy(
                    src_ref=land, dst_ref=land, send_sem=fsend_sems.at[3 * a + k], recv_sem=frecv_sems.at[3 * a + k],
                    device_id=(x, y, 1 - c), device_id_type=MESH).wait_recv()
        for cp in sends + fwds:
            cp.wait_send()

    return _pcall(body, out_shape=[jax.ShapeDtypeStruct((4,) + s.shape, s.dtype) for s in shards],
                  in_specs=[ANY] * n, out_specs=[ANY] * n,
                  scratch_shapes=[pltpu.SemaphoreType.DMA((3 * n,))] * 4, name="gather_weights")(*shards)


def _row_chunks(rows, n):
    while n > 1 and rows % (8 * n):
        n //= 2
    return [(k * (rows // n), rows // n) for k in range(n)]


def _swap_partial_halves(gs):
    n = len(gs)
    plan = [(a, j, r0, nr) for a in range(n) for j in range(4)
            for r0, nr in _row_chunks(gs[a].shape[1] // 2, SWAP_CHUNKS)]

    def body(*refs):
        ins, outs = refs[:n], refs[n:2 * n]
        send_sems, recv_sems = refs[2 * n:]
        x, y, c, _ = _place()
        cps = []
        for k, (a, j, r0, nr) in enumerate(plan):
            rh = gs[a].shape[1] // 2
            cp = pltpu.make_async_remote_copy(
                src_ref=ins[a].at[j, pl.ds((1 - c) * rh + r0, nr), :], dst_ref=outs[a].at[j, pl.ds(r0, nr), :],
                send_sem=send_sems.at[k], recv_sem=recv_sems.at[k], device_id=(x, y, 1 - c), device_id_type=MESH)
            cp.start()
            cps.append(cp)
        for cp in cps:
            cp.wait()

    half = [jax.ShapeDtypeStruct((4, g.shape[1] // 2, g.shape[2]), g.dtype) for g in gs]
    return _pcall(body, out_shape=half, in_specs=[ANY] * n, out_specs=[ANY] * n,
                  scratch_shapes=[pltpu.SemaphoreType.DMA((len(plan),))] * 2, name="swap_partial_halves")(*gs)


def _chip_scatter(ps, lands):
    n = len(ps)

    def body(*refs):
        ins, outs = refs[:n], refs[2 * n:3 * n]
        send_sems, recv_sems = refs[3 * n:]
        x, y, c, chips = _place()
        mychip = 2 * x + y
        cps = []
        for a in range(n):
            for k, (px, py) in enumerate(chips):
                cp = pltpu.make_async_remote_copy(
                    src_ref=ins[a].at[2 * px + py], dst_ref=outs[a].at[mychip],
                    send_sem=send_sems.at[3 * a + k], recv_sem=recv_sems.at[3 * a + k],
                    device_id=(px, py, c), device_id_type=MESH)
                cp.start()
                cps.append(cp)
        for a in range(n):
            for k, (px, py) in enumerate(chips):
                land = outs[a].at[2 * px + py]
                pltpu.make_async_remote_copy(
                    src_ref=land, dst_ref=land, send_sem=send_sems.at[3 * a + k], recv_sem=recv_sems.at[3 * a + k],
                    device_id=(px, py, c), device_id_type=MESH).wait_recv()
        for cp in cps:
            cp.wait_send()

    return _pcall(body, out_shape=[jax.ShapeDtypeStruct(p.shape, p.dtype) for p in ps],
                  in_specs=[ANY] * (2 * n), out_specs=[ANY] * n,
                  input_output_aliases={n + a: a for a in range(n)},
                  scratch_shapes=[pltpu.SemaphoreType.DMA((3 * n,))] * 2, name="chip_scatter")(*ps, *lands)


def _swap_final_halves(hs):
    n = len(hs)
    plan = [(a, r0, nr) for a in range(n) for r0, nr in _row_chunks(hs[a].shape[1], 2 * SWAP_CHUNKS)]

    def body(*refs):
        outs = refs[n:2 * n]
        send_sems, recv_sems = refs[2 * n:]
        x, y, c, _ = _place()
        cps = []
        for k, (a, r0, nr) in enumerate(plan):
            mine = outs[a].at[c, pl.ds(r0, nr), :]
            cp = pltpu.make_async_remote_copy(
                src_ref=mine, dst_ref=mine, send_sem=send_sems.at[k], recv_sem=recv_sems.at[k],
                device_id=(x, y, 1 - c), device_id_type=MESH)
            cp.start()
            cps.append(cp)
        for k, (a, r0, nr) in enumerate(plan):
            land = outs[a].at[1 - c, pl.ds(r0, nr), :]
            pltpu.make_async_remote_copy(
                src_ref=land, dst_ref=land, send_sem=send_sems.at[k], recv_sem=recv_sems.at[k],
                device_id=(x, y, 1 - c), device_id_type=MESH).wait_recv()
        for cp in cps:
            cp.wait_send()

    return _pcall(body, out_shape=[jax.ShapeDtypeStruct(h.shape, h.dtype) for h in hs],
                  in_specs=[ANY] * n, out_specs=[ANY] * n, input_output_aliases={a: a for a in range(n)},
                  scratch_shapes=[pltpu.SemaphoreType.DMA((len(plan),))] * 2, name="swap_final_halves")(*hs)


def _pad_cols(a, n):
    return jnp.pad(a, ((0, 0), (0, n - a.shape[1])))


def kernel(x, c, w_ada, b_ada, norm_w, w_in, gdn_conv_w, gdn_a_log, gdn_dt_bias, gdn_norm_w, w_proj_sb, w_proj_gdn, w_out, final_norm_w, loss_target, m_w_ada, m_b_ada, m_norm_w, m_w_in, m_gdn_conv_w, m_gdn_a_log, m_gdn_dt_bias, m_gdn_norm_w, m_w_proj_sb, m_w_proj_gdn, m_w_out, m_final_norm_w, v_w_ada, v_b_ada, v_norm_w, v_w_in, v_gdn_conv_w, v_gdn_a_log, v_gdn_dt_bias, v_gdn_norm_w, v_w_proj_sb, v_w_proj_gdn, v_w_out, v_final_norm_w):
    T, D = x.shape[1], x.shape[2]
    H = gdn_a_log.shape[1]
    W = H * HD
    assert W == D and T % CH == 0
    NA = w_ada.shape[2]
    NI = w_in.shape[2]
    CW = gdn_conv_w.shape[2]
    px, py, pc = lax.axis_index("x"), lax.axis_index("y"), lax.axis_index("c")
    chip = 2 * px + py
    me = 2 * chip + pc
    x2d, tgt = x[0], loss_target[0]
    PW = 3 * D

    pack1 = jnp.concatenate([c, _pad_cols(gdn_conv_w[0], D), jnp.zeros((3, D), F32)], axis=0)
    got1 = _allgather8(pack1).reshape(8, 8, D)
    c_all = got1[:, 0, :]
    conv_w = jnp.concatenate([got1[2 * j, 1:1 + CONV_K, :CW] for j in range(4)], axis=1)

    b_shard = lax.dynamic_slice_in_dim(b_ada, chip * NA, NA, axis=1)
    mod_part = _mod_part(c_all, w_ada[0], b_shard)
    got2 = _allgather8(mod_part).reshape(8, 8, NA)
    mod = jnp.concatenate([lax.dynamic_index_in_dim(got2[2 * j], me, 0) for j in range(4)], axis=1)
    shift, scale, gate = mod[:, :D], mod[:, D:2 * D], mod[:, 2 * D:]

    own = [w_in[0].astype(BF16), w_proj_sb[0].astype(BF16), w_proj_gdn[0].astype(BF16), w_out[0].astype(BF16)]
    is_own = (jnp.arange(4) == chip)[:, None, None]
    wg = [jnp.where(is_own, o[None], g) for o, g in zip(own, _gather_weights(own))]
    w_in_full = jnp.concatenate([wg[0][j] for j in range(4)], axis=1)
    w_big = jnp.concatenate([w_in_full[:, :8 * W], w_in_full[:, 8 * W + 2 * H:]], axis=1)
    w_small = _pad_cols(w_in_full[:, 8 * W:8 * W + 2 * H], HD)
    w_psb, w_pg, w_o = (wg[i].reshape(D, D) for i in (1, 2, 3))

    h = _ada_fwd(x2d, norm_w, scale, shift)
    proj = _mm1(h, w_big, 'nn', F32, "proj_big")
    small = _mm1(h, w_small, 'nn', F32, "proj_small")
    o_sb, l_sb = _sb_fwd(proj, T, H)
    conv = _conv_fwd(proj, conv_w, T, H)
    al = jnp.pad(gdn_a_log, ((0, 0), (H, HD - 2 * H)))
    dtb = jnp.pad(gdn_dt_bias, ((0, 0), (H, HD - 2 * H)))
    qn, kn, vc, beta_b, g_b = _prep_fwd(conv, small, al, dtb, T, H)
    o_g, s_all = _gdn_fwd(qn, kn, vc, beta_b, g_b, T, H)
    a_sb, a_g = _gate_fwd(o_sb, proj, o_g, gdn_norm_w, T, H)
    p_sb = _mm1(a_sb, w_psb, 'nn', F32, "proj_sb")
    p_g = _mm1(a_g, w_pg, 'nn', F32, "proj_gdn")
    y = _merge_fwd(proj, p_sb, p_g, T, D, H)
    u = _mm1(y, w_o, 'nn', F32, "proj_out")
    loss_p, dx2, du, dgate, g_fnw = _loss_head(x2d, u, gate, final_norm_w.reshape(1, D), tgt)

    dy = _mm1(du, w_o, 'nt', F32, "d_merge")
    g_w_out = _mm1(y, du, 'tn', F32, "g_w_out")
    dm_sb, dm_g, dp_sb, dp_g = _merge_bwd(proj, p_sb, p_g, dy, T, D, H)
    da_sb = _mm1(dp_sb, w_psb, 'nt', F32, "d_a_sb")
    g_w_psb = _mm1(a_sb, dp_sb, 'tn', F32, "g_w_proj_sb")
    da_g = _mm1(dp_g, w_pg, 'nt', F32, "d_a_gdn")
    g_w_pg = _mm1(a_g, dp_g, 'tn', F32, "g_w_proj_gdn")
    do_sb, dz_sb, do_g, dz_g, g_gnw = _gate_bwd(o_sb, proj, o_g, gdn_norm_w, da_sb, da_g, T, H)
    dq_sb, dk_sb, dv_sb = _sb_bwd(proj, l_sb, do_sb, T, H)
    dqn, dkn, dvc, dbeta_b, dg_b = _gdn_bwd(qn, kn, vc, beta_b, g_b, s_all, do_g, T, H)
    dconv, dsmall, dal, ddtb = _prep_bwd(conv, small, al, dtb, dqn, dkn, dvc, dbeta_b, dg_b, T, H)
    dpre, g_conv = _conv_bwd(proj, conv_w, dconv, T, H)

    secs = [(dq_sb, 0), (dk_sb, W), (dv_sb, 2 * W), (dz_sb, 3 * W), (dpre, 4 * W), (dz_g, 7 * W),
            (dm_sb, 8 * W), (dm_g, 9 * W)]
    dh = _mm([(a, w_big, 0, off, a.shape[1]) for a, off in secs] + [(dsmall, w_small, 0, 0, HD)],
             'nt', F32, "d_h", tk=min(MM_TK_MANY, W))
    g_secs = [_mm1(h, a, 'tn', F32, "g_w_in_%d" % i) for i, (a, _) in enumerate(secs)]
    g_small = _mm1(h, dsmall, 'tn', F32, "g_w_in_small")
    grad_x, g_nw, dscale, dshift = _ada_bwd(x2d, norm_w, scale, shift, dh, dx2)

    dmod = jnp.concatenate([dshift, dscale, dgate], axis=1)
    misc = jnp.concatenate([g_nw, g_fnw, g_gnw, dal[:, H:2 * H], ddtb[:, H:2 * H], loss_p[:, :1]], axis=1)
    pack3 = jnp.concatenate([dmod, g_conv, _pad_cols(misc, PW), jnp.zeros((2, PW), F32)], axis=0)
    got3 = _allgather8(pack3)
    tot = _sum8(got3)
    dmod_all = got3.reshape(8, 8, PW)[:, 0, :]
    g_w_ada = _w_ada_grad(c_all, lax.dynamic_slice_in_dim(dmod_all, chip * NA, NA, axis=1))
    g_conv_sh = lax.dynamic_slice_in_dim(tot[1:1 + CONV_K], chip * CW, CW, axis=1)
    loss = tot[5, 2 * D + HD + 2 * H]

    g_in = jnp.concatenate(g_secs[:6] + [g_small[:, :2 * H]] + g_secs[6:], axis=1)
    g_full = [jnp.stack([g_in[:, j * NI:(j + 1) * NI] for j in range(4)]),
              g_w_psb.reshape(4, D // 4, D), g_w_pg.reshape(4, D // 4, D), g_w_out.reshape(4, D // 4, D)]
    g_sib = _swap_partial_halves(g_full)
    parts = [_add_halves(g, r, pc, "add_halves_%d" % i) for i, (g, r) in enumerate(zip(g_full, g_sib))]
    got = _chip_scatter([p[0] for p in parts], [p[1] for p in parts])
    halves = [_sum_chips(q, pc, "sum_chips_%d" % i) for i, q in enumerate(got)]
    g_red = [f.reshape(f.shape[1] * 2, f.shape[2]) for f in _swap_final_halves(halves)]

    out = {}

    def upd(name, w, g, m, v, shape):
        d_, m_, v_ = _adamw(w, g, m, v, "adamw_" + name)
        out[name] = (g.reshape(shape), d_.reshape(shape), m_.reshape(shape), v_.reshape(shape))

    upd("w_ada", w_ada[0], g_w_ada, m_w_ada[0], v_w_ada[0], w_ada.shape)
    upd("w_in", w_in[0], g_red[0], m_w_in[0], v_w_in[0], w_in.shape)
    upd("gdn_conv_w", gdn_conv_w[0], g_conv_sh, m_gdn_conv_w[0], v_gdn_conv_w[0], gdn_conv_w.shape)
    upd("w_proj_sb", w_proj_sb[0], g_red[1], m_w_proj_sb[0], v_w_proj_sb[0], w_proj_sb.shape)
    upd("w_proj_gdn", w_proj_gdn[0], g_red[2], m_w_proj_gdn[0], v_w_proj_gdn[0], w_proj_gdn.shape)
    upd("w_out", w_out[0], g_red[3], m_w_out[0], v_w_out[0], w_out.shape)

    def packs(b, nw, fnw, gnw, a, dt):
        row = jnp.concatenate([nw, fnw.reshape(1, D), gnw, a, dt], axis=1)
        return jnp.concatenate([b, _pad_cols(row, PW), jnp.zeros((6, PW), F32)], axis=0)

    g_pack = jnp.concatenate([tot[0:1], tot[5:6], jnp.zeros((6, PW), F32)], axis=0)
    d_, m_, v_ = _adamw(packs(b_ada, norm_w, final_norm_w, gdn_norm_w, gdn_a_log, gdn_dt_bias), g_pack,
                        packs(m_b_ada, m_norm_w, m_final_norm_w, m_gdn_norm_w, m_gdn_a_log, m_gdn_dt_bias),
                        packs(v_b_ada, v_norm_w, v_final_norm_w, v_gdn_norm_w, v_gdn_a_log, v_gdn_dt_bias),
                        "adamw_small")
    offs = {"norm_w": (0, D, (1, D)), "final_norm_w": (D, D, (D,)), "gdn_norm_w": (2 * D, HD, (1, HD)),
            "gdn_a_log": (2 * D + HD, H, (1, H)), "gdn_dt_bias": (2 * D + HD + H, H, (1, H))}
    out["b_ada"] = tuple(a[0:1] for a in (g_pack, d_, m_, v_))
    for name, (o, n_, shp) in offs.items():
        out[name] = tuple(a[1, o:o + n_].reshape(shp) for a in (g_pack, d_, m_, v_))

    names = ['w_ada', 'b_ada', 'norm_w', 'w_in', 'gdn_conv_w', 'gdn_a_log', 'gdn_dt_bias', 'gdn_norm_w',
             'w_proj_sb', 'w_proj_gdn', 'w_out', 'final_norm_w']
    return (loss, grad_x.reshape(x.shape), *[out[n][0] for n in names], *[out[n][1] for n in names],
            *[out[n][2] for n in names], *[out[n][3] for n in names])
```

```python
import functools

import jax
import jax.numpy as jnp
from jax import lax
from jax.experimental import pallas as pl
from jax.experimental.pallas import tpu as pltpu

F32 = jnp.float32
BF16 = jnp.bfloat16
HD = 128
CH = 64
CONV_K = 4
NORM_EPS = 1e-6
L2_EPS = 1e-6
ADAM_LR, ADAM_B1, ADAM_B2, ADAM_EPS, ADAM_WD, ADAM_STEP = 0.001, 0.9, 0.999, 1e-08, 0.01, 10

VMEM_LIMIT = 56 * 1024 * 1024
SB_BQ, SB_BK = 512, 256
SB_DEAD = -110.0
GDN_TB = 256
GDN_G = 8
EW_TM = 256
HEAD_TM = 512
CONV_CW = 512
MM_TM, MM_TN, MM_TK = 1024, 1024, 1024
MM_TK_MANY = 512
SWAP_CHUNKS = 4
MESH = pl.DeviceIdType.MESH

NN = (((1,), (0,)), ((), ()))
NT = (((1,), (1,)), ((), ()))
TN = (((0,), (0,)), ((), ()))


def _pcall(body, **kw):
    return pl.pallas_call(body, **kw)


def _cparams(sem=None):
    return pltpu.CompilerParams(dimension_semantics=sem, vmem_limit_bytes=VMEM_LIMIT)


def _dot(a, b, dims=NN):
    return lax.dot_general(a, b, dims, preferred_element_type=F32)


def _dotf(a, b, dims=NN):
    return lax.dot_general(a, b, dims, precision=lax.Precision.HIGHEST, preferred_element_type=F32)


def _bdot_make(dims, da_rule, db_rule):
    @jax.custom_vjp
    def f(a, b):
        return _dot(a.astype(BF16), b.astype(BF16), dims)

    def fwd(a, b):
        return f(a, b), (a, b)

    def bwd(res, g):
        a, b = res
        return da_rule(g, a, b), db_rule(g, a, b)

    f.defvjp(fwd, bwd)
    return f


def _rdot(a, b, dims):
    return _dot(a.astype(BF16), b.astype(BF16), dims)


NNB =(((2,), (1,)), ((0,), (0,)))
NTB = (((2,), (2,)), ((0,), (0,)))
TNB = (((1,), (1,)), ((0,), (0,)))
_bdot_nn = _bdot_make(NNB, lambda g, a, b: _rdot(g, b, NTB), lambda g, a, b: _rdot(a, g, TNB))
_bdot_nt = _bdot_make(NTB, lambda g, a, b: _rdot(g, b, NNB), lambda g, a, b: _rdot(g, a, TNB))
_bdot_tn = _bdot_make(TNB, lambda g, a, b: _rdot(b, g, NTB), lambda g, a, b: _rdot(a, g, NNB))


def _iota(shape, axis):
    return lax.broadcasted_iota(jnp.int32, shape, axis)


def _sigmoid(x):
    e = jnp.exp(-jnp.abs(x))
    return jnp.where(x >= 0, 1.0 / (1.0 + e), e / (1.0 + e))


def _silu(x):
    return x * _sigmoid(x)


def _softplus(x):
    return jnp.maximum(x, 0.0) + jnp.log(1.0 + jnp.exp(-jnp.abs(x)))


def _rms(x, w):
    return x * lax.rsqrt(jnp.mean(x * x, axis=-1, keepdims=True) + NORM_EPS) * w


def _ew(fn, grid, ins, in_specs, outs, out_specs, acc=None, name=None):
    n_in = len(ins)
    acc = acc or [None] * len(outs)

    def body(*refs):
        vals = fn(*[r[...] for r in refs[:n_in]])
        if not isinstance(vals, (tuple, list)):
            vals = (vals,)
        for r, v, a in zip(refs[n_in:], vals, acc):
            if a is None:
                r[...] = v.astype(r.dtype)
                continue
            first = pl.program_id(len(grid) - 1) == 0
            if a == 'all':
                for ax in range(len(grid) - 1):
                    first = jnp.logical_and(first, pl.program_id(ax) == 0)

            @pl.when(first)
            def _():
                r[...] = v.astype(r.dtype)

            @pl.when(jnp.logical_not(first))
            def _():
                r[...] += v.astype(r.dtype)

    res = _pcall(body, grid=grid, in_specs=in_specs, out_specs=out_specs,
                 out_shape=[jax.ShapeDtypeStruct(s, d) for s, d in outs],
                 compiler_params=_cparams(("arbitrary",) * len(grid)), name=name)(*ins)
    return res


def _mm(pairs, mode, out_dtype, name, add=None, tm=None, tn=None, tk=None):
    a0, b0 = pairs[0][0], pairs[0][1]
    M = a0.shape[1] if mode == 'tn' else a0.shape[0]
    N = b0.shape[0] if mode == 'nt' else b0.shape[1]
    tm = min(tm or MM_TM, M)
    tn = min(tn or MM_TN, N)
    tk = tk or MM_TK
    tks = [min(tk, p[4]) for p in pairs]
    nks = [p[4] // t for p, t in zip(pairs, tks)]
    offs = [sum(nks[:i]) for i in range(len(pairs))]
    total = sum(nks)
    assert M % tm == 0 and N % tn == 0 and all(p[4] % t == 0 for p, t in zip(pairs, tks)), (name, M, N)
    dims = {'nn': NN, 'nt': NT, 'tn': TN}[mode]

    in_specs, ins = [], []
    for (a, b, ao, bo, kl), t, nk, off in zip(pairs, tks, nks, offs):
        def kidx(kk, nk=nk, off=off):
            return jnp.minimum(jnp.maximum(kk - off, 0), nk - 1)
        if mode == 'tn':
            in_specs.append(pl.BlockSpec((t, tm), lambda i, j, kk, f=kidx, o=ao // t: (o + f(kk), i)))
        else:
            in_specs.append(pl.BlockSpec((tm, t), lambda i, j, kk, f=kidx, o=ao // t: (i, o + f(kk))))
        if mode == 'nt':
            in_specs.append(pl.BlockSpec((tn, t), lambda i, j, kk, f=kidx, o=bo // t: (j, o + f(kk))))
        else:
            in_specs.append(pl.BlockSpec((t, tn), lambda i, j, kk, f=kidx, o=bo // t: (o + f(kk), j)))
        ins += [a, b]
    if add is not None:
        in_specs.append(pl.BlockSpec((tm, tn), lambda i, j, kk: (i, j)))
        ins.append(add)
    npairs = len(pairs)

    def body(*refs):
        out_ref, acc_ref = refs[-2], refs[-1]
        kk = pl.program_id(2)

        @pl.when(kk == 0)
        def _():
            acc_ref[...] = jnp.zeros_like(acc_ref)

        for p in range(npairs):
            def upd(p=p):
                acc_ref[...] += _dot(refs[2 * p][...].astype(BF16), refs[2 * p + 1][...].astype(BF16), dims)
            if npairs == 1:
                upd()
            else:
                pl.when(jnp.logical_and(kk >= offs[p], kk < offs[p] + nks[p]))(upd)

        @pl.when(kk == total - 1)
        def _():
            r = acc_ref[...]
            if add is not None:
                r = r + refs[2 * npairs][...]
            out_ref[...] = r.astype(out_ref.dtype)

    return _pcall(body, grid=(M // tm, N // tn, total), in_specs=in_specs,
                  out_specs=pl.BlockSpec((tm, tn), lambda i, j, kk: (i, j)),
                  out_shape=jax.ShapeDtypeStruct((M, N), out_dtype),
                  scratch_shapes=[pltpu.VMEM((tm, tn), F32)],
                  compiler_params=_cparams(("parallel", "parallel", "arbitrary")), name=name)(*ins)


def _mm1(a, b, mode, out_dtype, name, **kw):
    k = a.shape[0] if mode == 'tn' else a.shape[1]
    return _mm([(a, b, 0, 0, k)], mode, out_dtype, name, **kw)


def _sb_tiles(T):
    bq = min(SB_BQ, T)
    bk = min(SB_BK, bq)
    return bq, bk, bq // bk


def _sb_block(i, j, masked, q, k_ref, v_ref, accL, bq, bk, UU, row0=0):
    r = pl.ds(pl.multiple_of(j * bk, bk), bk)
    kj = k_ref[r, :]
    vj = v_ref[r, :]
    z = _dot(q, kj, NT) * (HD ** -0.5)
    lb = jnp.minimum(z, 0.0) - jnp.log(1.0 + jnp.exp(-jnp.abs(z)))
    ln = lb - z
    mask = None
    if masked:
        rows = q.shape[0]
        mask = (j * bk + _iota((rows, bk), 1)) < (i * bq + row0 + _iota((rows, bk), 0))
        ln = jnp.where(mask, ln, 0.0)
    hi = ln.astype(BF16)
    lo = (ln - hi.astype(F32)).astype(BF16)
    later = _dot(jnp.concatenate([hi, lo], axis=1), UU) + accL
    a = jnp.exp(lb + later)
    if masked:
        a = jnp.where(mask, a, 0.0)
    return r, kj, vj, mask, ln, lb, a


def _sb_fwd(proj, T, H):
    bq, bk, nd = _sb_tiles(T)

    assert T // bk < HD

    def body(q_ref, kf_ref, vf_ref, o_ref, l_ref, k_ref, v_ref):
        i = pl.program_id(1)

        @pl.when(i == 0)
        def _():
            k_ref[...] = kf_ref[...].astype(BF16)
            v_ref[...] = vf_ref[...].astype(BF16)

        q = q_ref[...].astype(BF16)
        U = (_iota((bk, bk), 0) > _iota((bk, bk), 1)).astype(BF16)
        UU = jnp.concatenate([U, U], axis=0)
        lane = _iota((bq, HD), 1)

        def blk(j, carry, masked, row0=0):
            acc, accL, saved = (c[row0:] for c in carry)
            _, _, vj, _, ln, _, a = _sb_block(i, j, masked, q[row0:], k_ref, v_ref, accL, bq, bk, UU, row0)
            new = (acc + _dot(a.astype(BF16), vj), accL + jnp.sum(ln, axis=1, keepdims=True),
                   jnp.where(lane[row0:] == j, accL, saved))
            if row0:
                new = tuple(jnp.concatenate([c[:row0], n], axis=0) for c, n in zip(carry, new))
            return new

        carry = (jnp.zeros((bq, HD), F32), jnp.zeros((bq, 1), F32), jnp.zeros((bq, HD), F32))
        for jj in range(nd):
            carry = blk(i * nd + (nd - 1 - jj), carry, True, (nd - 1 - jj) * bk)
        def more(c):
            return jnp.logical_and(c[0] < i * nd, jnp.max(c[2]) > SB_DEAD)

        done = lax.while_loop(more, lambda c: (c[0] + 1,) + blk(i * nd - 1 - c[0], c[1:], False),
                              (jnp.int32(0),) + carry)
        o_ref[...] = done[1]
        l_ref[...] = jnp.where(lane == HD - 1, done[0].astype(F32), done[3])

    blk_q = pl.BlockSpec((bq, HD), lambda h, i: (i, h))
    return _pcall(
        body, grid=(H, T // bq),
        in_specs=[blk_q, pl.BlockSpec((T, HD), lambda h, i: (0, H + h)),
                  pl.BlockSpec((T, HD), lambda h, i: (0, 2 * H + h))],
        out_specs=[blk_q, blk_q],
        out_shape=[jax.ShapeDtypeStruct((T, H * HD), F32)] * 2,
        scratch_shapes=[pltpu.VMEM((T, HD), BF16), pltpu.VMEM((T, HD), BF16)],
        compiler_params=_cparams(("parallel", "arbitrary")), name="sb_fwd")(proj, proj, proj)


def _sb_bwd(proj, l_sb, do_sb, T, H):
    bq, bk, nd = _sb_tiles(T)
    nq = T // bq

    def body(q_ref, kf_ref, vf_ref, l_ref, do_ref, dq_ref, dk_ref, dv_ref, dk_acc, dv_acc, k_ref, v_ref):
        i = pl.program_id(1)

        @pl.when(i == 0)
        def _():
            dk_acc[...] = jnp.zeros_like(dk_acc)
            dv_acc[...] = jnp.zeros_like(dv_acc)
            k_ref[...] = kf_ref[...].astype(BF16)
            v_ref[...] = vf_ref[...].astype(BF16)

        q = q_ref[...].astype(BF16)
        dob = do_ref[...].astype(BF16)
        saved = l_ref[...]
        lane = _iota((bq, HD), 1)
        U = (_iota((bk, bk), 0) > _iota((bk, bk), 1)).astype(BF16)
        UU = jnp.concatenate([U, U], axis=0)
        Ue = (_iota((bk, bk), 0) < _iota((bk, bk), 1)).astype(BF16)

        def blk(j, carry, masked, row0=0):
            dq, accP = (c[row0:] for c in carry)
            qs, dos = q[row0:], dob[row0:]
            accL = jnp.sum(jnp.where(lane[row0:] == j, saved[row0:], 0.0), axis=1, keepdims=True)
            r, kj, vj, mask, _, lb, a = _sb_block(i, j, masked, qs, k_ref, v_ref, accL, bq, bk, UU, row0)
            p = a * _dot(dos, vj, NT)
            pre = _dot(p.astype(BF16), Ue) + accP
            sig = jnp.exp(lb)
            dz = p * (1.0 - sig) - pre * sig
            if masked:
                dz = jnp.where(mask, dz, 0.0)
            dzb = (dz * (HD ** -0.5)).astype(BF16)
            dk_acc[r, :] += _dot(dzb, qs, TN)
            dv_acc[r, :] += _dot(a.astype(BF16), dos, TN)
            new = (dq + _dot(dzb, kj), accP + jnp.sum(p, axis=1, keepdims=True))
            if row0:
                new = tuple(jnp.concatenate([c[:row0], n], axis=0) for c, n in zip(carry, new))
            return new

        carry = (jnp.zeros((bq, HD), F32), jnp.zeros((bq, 1), F32))
        walked = jnp.max(jnp.where(lane == HD - 1, saved, 0.0)).astype(jnp.int32)
        carry = lax.fori_loop(i * nd - walked, i * nd, lambda j, c: blk(j, c, False), carry)
        for jj in range(nd):
            carry = blk(i * nd + jj, carry, True, jj * bk)
        dq_ref[...] = carry[0].astype(BF16)

        @pl.when(i == nq - 1)
        def _():
            dk_ref[...] = dk_acc[...].astype(BF16)
            dv_ref[...] = dv_acc[...].astype(BF16)

    W = H * HD
    blk_q = pl.BlockSpec((bq, HD), lambda h, i: (i, h))
    blk_t = pl.BlockSpec((T, HD), lambda h, i: (0, h))
    return _pcall(
        body, grid=(H, nq),
        in_specs=[blk_q, pl.BlockSpec((T, HD), lambda h, i: (0, H + h)),
                  pl.BlockSpec((T, HD), lambda h, i: (0, 2 * H + h)), blk_q, blk_q],
        out_specs=[blk_q, blk_t, blk_t],
        out_shape=[jax.ShapeDtypeStruct((T, W), BF16)] * 3,
        scratch_shapes=[pltpu.VMEM((T, HD), F32), pltpu.VMEM((T, HD), F32),
                        pltpu.VMEM((T, HD), BF16), pltpu.VMEM((T, HD), BF16)],
        compiler_params=_cparams(("parallel", "arbitrary")), name="sb_bwd")(proj, proj, proj, l_sb, do_sb)


@jax.custom_vjp
def _unit_lower_inv(lw):
    C = lw.shape[1]
    x = (_iota(lw.shape, 1) == _iota(lw.shape, 2)).astype(F32) - lw
    pw = lw
    for _ in range(C.bit_length() - 2):
        pw = _dotf(pw, pw, NNB)
        x = x + _dotf(x, pw, NNB)
    return x


_unit_lower_inv.defvjp(lambda lw: (_unit_lower_inv(lw),) * 2,
                       lambda x, g: (-_dotf(_dotf(x, g, TNB), x, NTB),))


def _gdn_chunk(q, k, v, bb, gb, S):
    G, C = q.shape[0], q.shape[1]
    ri, ci = _iota((G, C, C), 1), _iota((G, C, C), 2)
    tril, strict = ri >= ci, ri > ci
    g_col = gb[:, :, :C]
    g_row = jnp.swapaxes(g_col, 1, 2)
    decay = jnp.where(tril, jnp.exp(jnp.where(tril, g_col - g_row, 0.0)), 0.0)
    eg = jnp.exp(gb)
    qs = q * (HD ** -0.5)
    kb = k * bb
    lw = jnp.where(strict, _bdot_nt(kb, k) * decay, 0.0)
    x = _unit_lower_inv(lw)
    u = _bdot_nn(x, v * bb)
    w = _bdot_nn(x, kb * eg)
    aq = jnp.where(tril, _bdot_nt(qs, k) * decay, 0.0)
    vnew = u - _bdot_nn(w, S)
    o = _bdot_nn(qs * eg, S) + _bdot_nn(aq, vnew)
    g_last = gb[:, C - 1:C, :]
    s_new = S * jnp.exp(g_last) + _bdot_tn(k * jnp.exp(g_last - gb), vnew)
    return o, s_new


def _gdn_fwd(qn, kn, vc, beta_b, g_b, T, H):
    tb = min(GDN_TB, T)
    nc = tb // CH
    G = min(GDN_G, H)

    def body(q_ref, k_ref, v_ref, b_ref, g_ref, o_ref, s_ref, s_scr):
        @pl.when(pl.program_id(1) == 0)
        def _():
            s_scr[...] = jnp.zeros_like(s_scr)

        def step(c, carry):
            r = pl.ds(pl.multiple_of(c * CH, CH), CH)
            s = s_scr[...]
            s_ref[:, c] = s
            o, s2 = _gdn_chunk(q_ref[:, r, :], k_ref[:, r, :], v_ref[:, r, :], b_ref[:, r, :], g_ref[:, r, :], s)
            o_ref[:, r, :] = o
            s_scr[...] = s2
            return carry

        lax.fori_loop(0, nc, step, 0)

    blk3 = pl.BlockSpec((G, tb, HD), lambda h, t: (h, t, 0))
    return _pcall(
        body, grid=(H // G, T // tb), in_specs=[blk3] * 5,
        out_specs=[blk3, pl.BlockSpec((G, nc, HD, HD), lambda h, t: (h, t, 0, 0))],
        out_shape=[jax.ShapeDtypeStruct((H, T, HD), F32), jax.ShapeDtypeStruct((H, T // CH, HD, HD), F32)],
        scratch_shapes=[pltpu.VMEM((G, HD, HD), F32)],
        compiler_params=_cparams(("parallel", "arbitrary")), name="gdn_fwd")(qn, kn, vc, beta_b, g_b)


def _gdn_bwd(qn, kn, vc, beta_b, g_b, s_all, do, T, H):
    tb = min(GDN_TB, T)
    nc = tb // CH
    nt = T // tb
    G = min(GDN_G, H)

    def body(q_ref, k_ref, v_ref, b_ref, g_ref, s_ref, do_ref, dq_ref, dk_ref, dv_ref, db_ref, dg_ref, ds_scr):
        @pl.when(pl.program_id(1) == 0)
        def _():
            ds_scr[...] = jnp.zeros_like(ds_scr)

        def step(it, carry):
            c = nc - 1 - it
            r = pl.ds(pl.multiple_of(c * CH, CH), CH)
            _, vjp = jax.vjp(_gdn_chunk, q_ref[:, r, :], k_ref[:, r, :], v_ref[:, r, :], b_ref[:, r, :],
                             g_ref[:, r, :], s_ref[:, c])
            dq, dk, dv, db, dg, ds = vjp((do_ref[:, r, :], ds_scr[...]))
            dq_ref[:, r, :] = dq
            dk_ref[:, r, :] = dk
            dv_ref[:, r, :] = dv
            db_ref[:, r, :] = db
            dg_ref[:, r, :] = dg
            ds_scr[...] = ds
            return carry

        lax.fori_loop(0, nc, step, 0)

    blk3 = pl.BlockSpec((G, tb, HD), lambda h, t: (h, nt - 1 - t, 0))
    return _pcall(
        body, grid=(H // G, nt),
        in_specs=[blk3] * 5 + [pl.BlockSpec((G, nc, HD, HD), lambda h, t: (h, nt - 1 - t, 0, 0)), blk3],
        out_specs=[blk3] * 5,
        out_shape=[jax.ShapeDtypeStruct((H, T, HD), F32)] * 5,
        scratch_shapes=[pltpu.VMEM((G, HD, HD), F32)],
        compiler_params=_cparams(("parallel", "arbitrary")), name="gdn_bwd")(qn, kn, vc, beta_b, g_b, s_all, do)


def _conv_tiles(T, C):
    return min(HEAD_TM, T), min(CONV_CW, C)


def _shift_down(main, halo, first, d):
    halo = jnp.where(first, 0.0, halo)
    ext = jnp.concatenate([halo, main], axis=0)
    return pltpu.roll(ext, d, 0)[8:]


def _conv_fwd(proj, conv_w, T, H):
    W = H * HD
    C = 3 * W
    tm, cw = _conv_tiles(T, C)
    col0 = 4 * W // cw

    def body(x_ref, h_ref, w_ref, o_ref):
        first = pl.program_id(1) == 0
        main = x_ref[...]
        halo = h_ref[...]
        w = w_ref[...]
        out = w[CONV_K - 1:CONV_K, :] * main
        for d in range(1, CONV_K):
            out = out + w[CONV_K - 1 - d:CONV_K - d, :] * _shift_down(main, halo, first, d)
        o_ref[...] = out

    return _pcall(
        body, grid=(C // cw, T // tm),
        in_specs=[pl.BlockSpec((tm, cw), lambda cb, i: (i, col0 + cb)),
                  pl.BlockSpec((8, cw), lambda cb, i: (jnp.maximum(i * (tm // 8) - 1, 0), col0 + cb)),
                  pl.BlockSpec((CONV_K, cw), lambda cb, i: (0, cb))],
        out_specs=pl.BlockSpec((tm, cw), lambda cb, i: (i, cb)),
        out_shape=jax.ShapeDtypeStruct((T, C), F32),
        compiler_params=_cparams(("parallel", "arbitrary")), name="conv_fwd")(proj, proj, conv_w)


def _conv_bwd(proj, conv_w, dconv, T, H):
    W = H * HD
    C = 3 * W
    tm, cw = _conv_tiles(T, W)
    col0 = 4 * W // cw
    nbw = W // cw
    nt = T // tm

    def body(x_ref, h_ref, w_ref, d_ref, dn_ref, dx_ref, dw_ref):
        i = pl.program_id(1)
        first = i == 0
        main = x_ref[...]
        halo = h_ref[...]
        w = w_ref[...]
        dmain = d_ref[...]
        dnext = jnp.where(i == nt - 1, 0.0, dn_ref[...])
        dext = jnp.concatenate([dmain, dnext], axis=0)
        dx = w[CONV_K - 1:CONV_K, :] * dmain
        rows = [jnp.sum(dmain * main, axis=0, keepdims=True)]
        for d in range(1, CONV_K):
            dx = dx + w[CONV_K - 1 - d:CONV_K - d, :] * pltpu.roll(dext, tm + 8 - d, 0)[:tm]
            rows.append(jnp.sum(dmain * _shift_down(main, halo, first, d), axis=0, keepdims=True))
        dx_ref[...] = dx.astype(BF16)

        @pl.when(first)
        def _():
            dw_ref[...] = jnp.zeros_like(dw_ref)

        for d in range(CONV_K):
            dw_ref[CONV_K - 1 - d:CONV_K - d, :] += rows[d]

    return _pcall(
        body, grid=(C // cw, nt),
        in_specs=[pl.BlockSpec((tm, cw), lambda cb, i: (i, col0 + cb)),
                  pl.BlockSpec((8, cw), lambda cb, i: (jnp.maximum(i * (tm // 8) - 1, 0), col0 + cb)),
                  pl.BlockSpec((CONV_K, cw), lambda cb, i: (0, cb)),
                  pl.BlockSpec((None, tm, cw), lambda cb, i: (cb // nbw, i, cb % nbw)),
                  pl.BlockSpec((None, 8, cw),
                               lambda cb, i: (cb // nbw, jnp.minimum((i + 1) * (tm // 8), T // 8 - 1), cb % nbw))],
        out_specs=[pl.BlockSpec((tm, cw), lambda cb, i: (i, cb)), pl.BlockSpec((CONV_K, cw), lambda cb, i: (0, cb))],
        out_shape=[jax.ShapeDtypeStruct((T, C), BF16), jax.ShapeDtypeStruct((CONV_K, C), F32)],
        compiler_params=_cparams(("parallel", "arbitrary")), name="conv_bwd")(proj, proj, conv_w, dconv, dconv)


def _prep_fn(H, tm):
    def fn(cq, ck, cv, small, al, dtb):
        h = pl.program_id(1)
        lane = _iota((HD, HD), 0)
        oh_b = (lane == h).astype(F32)
        oh_a = (lane == H + h).astype(F32)
        q, k, v = _silu(cq), _silu(ck), _silu(cv)
        qn = q * lax.rsqrt(jnp.sum(q * q, axis=-1, keepdims=True) + L2_EPS)
        kn = k * lax.rsqrt(jnp.sum(k * k, axis=-1, keepdims=True) + L2_EPS)
        beta = _dotf(_sigmoid(small), oh_b)
        g = _dotf(-jnp.exp(al) * _softplus(small + dtb), oh_a)
        nch = tm // CH
        tri = (_iota((nch, CH, CH), 1) >= _iota((nch, CH, CH), 2)).astype(F32)
        gc = _dotf(tri, g.reshape(nch, CH, HD), NNB).reshape(tm, HD)
        return qn, kn, v, beta, gc
    return fn


def _prep_specs(T, H):
    tm = min(HEAD_TM, T)
    W = H * HD
    row = lambda s: pl.BlockSpec((tm, HD), lambda i, h, s=s: (i, s * H + h))
    small = pl.BlockSpec((tm, HD), lambda i, h: (i, 0))
    par = pl.BlockSpec((1, HD), lambda i, h: (0, 0))
    blk = pl.BlockSpec((tm, HD), lambda i, h: (i, h))
    blk3 = pl.BlockSpec((None, tm, HD), lambda i, h: (h, i, 0))
    return tm, W, row, small, par, blk, blk3


def _prep_fwd(conv, small, al, dtb, T, H):
    tm, W, row, sm, par, blk, blk3 = _prep_specs(T, H)
    return _ew(_prep_fn(H, tm), (T // tm, H), [conv, conv, conv, small, al, dtb],
               [row(0), row(1), row(2), sm, par, par],
               [((H, T, HD), F32)] * 5, [blk3] * 5, name="gdn_prep")


def _prep_bwd(conv, small, al, dtb, dqn, dkn, dvc, dbeta, dg, T, H):
    tm, W, row, sm, par, blk, blk3 = _prep_specs(T, H)
    f = _prep_fn(H, tm)

    def fn(cq, ck, cv, small_, al_, dtb_, a, b, c, d, e):
        _, vjp = jax.vjp(f, cq, ck, cv, small_, al_, dtb_)
        dq_, dk_, dv_, dsmall, dal, ddtb = vjp((a, b, c, d, e))
        return jnp.stack([dq_, dk_, dv_]), dsmall, dal, ddtb

    return _ew(fn, (T // tm, H), [conv, conv, conv, small, al, dtb, dqn, dkn, dvc, dbeta, dg],
               [row(0), row(1), row(2), sm, par, par] + [blk3] * 5,
               [((3, T, W), F32), ((T, HD), F32), ((1, HD), F32), ((1, HD), F32)],
               [pl.BlockSpec((3, tm, HD), lambda i, h: (0, i, h)), sm, par, par],
               acc=[None, 'inner', 'all', 'all'], name="gdn_prep_bwd")


def _gate_fn(o_sb, z_sb, o_g, z_g, gnw):
    a_sb = o_sb * _silu(z_sb)
    a_g = _rms(o_g, gnw) * _silu(z_g)
    return a_sb, a_g


def _gate_specs(T, H):
    tm = min(HEAD_TM, T)
    blk = pl.BlockSpec((tm, HD), lambda i, h: (i, h))
    sec = lambda s: pl.BlockSpec((tm, HD), lambda i, h, s=s: (i, s * H + h))
    par = pl.BlockSpec((1, HD), lambda i, h: (0, 0))
    blk3 = pl.BlockSpec((None, tm, HD), lambda i, h: (h, i, 0))
    return tm, blk, blk3, sec, par


def _gate_fwd(o_sb, proj, o_g, gnw, T, H):
    tm, blk, blk3, sec, par = _gate_specs(T, H)
    W = H * HD
    return _ew(_gate_fn, (T // tm, H), [o_sb, proj, o_g, proj, gnw], [blk, sec(3), blk3, sec(7), par],
               [((T, W), BF16)] * 2, [blk, blk], name="gate_fwd")


def _gate_bwd(o_sb, proj, o_g, gnw, da_sb, da_g, T, H):
    tm, blk, blk3, sec, par = _gate_specs(T, H)
    W = H * HD

    def fn(o_sb_, z_sb, o_g_, z_g, gnw_, da, db):
        _, vjp = jax.vjp(_gate_fn, o_sb_, z_sb, o_g_, z_g, gnw_)
        return vjp((da, db))

    return _ew(fn, (T // tm, H), [o_sb, proj, o_g, proj, gnw, da_sb, da_g], [blk, sec(3), blk3, sec(7), par, blk, blk],
               [((T, W), F32), ((T, W), BF16), ((H, T, HD), F32), ((T, W), BF16), ((1, HD), F32)],
               [blk, blk, blk3, blk, par], acc=[None, None, None, None, 'all'], name="gate_bwd")


def _ada_fn(x, nw, scale, shift):
    return _rms(x, nw) * (1.0 + scale) + shift


def _row_specs(T, D):
    tm = min(EW_TM, T)
    return tm, pl.BlockSpec((tm, D), lambda i: (i, 0)), pl.BlockSpec((1, D), lambda i: (0, 0))


def _ada_fwd(x, nw, scale, shift):
    T, D = x.shape
    tm, row, par = _row_specs(T, D)
    return _ew(_ada_fn, (T // tm,), [x, nw, scale, shift], [row, par, par, par], [((T, D), BF16)], [row],
               name="ada_norm")[0]


def _ada_bwd(x, nw, scale, shift, dh, dx2):
    T, D = x.shape
    tm, row, par = _row_specs(T, D)

    def fn(x_, nw_, sc_, sh_, dh_, dx2_):
        _, vjp = jax.vjp(_ada_fn, x_, nw_, sc_, sh_)
        dx, dnw, dsc, dsh = vjp(dh_)
        return dx + dx2_, dnw, dsc, dsh

    return _ew(fn, (T // tm,), [x, nw, scale, shift, dh, dx2], [row, par, par, par, row, row],
               [((T, D), F32)] + [((1, D), F32)] * 3, [row, par, par, par], acc=[None, 'all', 'all', 'all'],
               name="ada_norm_bwd")


def _merge_fn(m_sb, m_g, p_sb, p_g):
    return _sigmoid(m_sb) * p_sb + _sigmoid(m_g) * p_g


def _merge_specs(T, D, H):
    tm, tc = min(HEAD_TM, T), min(512, D)
    nb = D // tc
    blk = pl.BlockSpec((tm, tc), lambda i, j: (i, j))
    sec = lambda s: pl.BlockSpec((tm, tc), lambda i, j, s=s: (i, s * nb + j))
    return tm, tc, blk, sec


def _merge_fwd(proj, p_sb, p_g, T, D, H):
    tm, tc, blk, sec = _merge_specs(T, D, H)
    return _ew(_merge_fn, (T // tm, D // tc), [proj, proj, p_sb, p_g], [sec(8), sec(9), blk, blk],
               [((T, D), BF16)], [blk], name="merge")[0]


def _merge_bwd(proj, p_sb, p_g, dy, T, D, H):
    tm, tc, blk, sec = _merge_specs(T, D, H)

    def fn(m_sb, m_g, p_sb_, p_g_, dy_):
        _, vjp = jax.vjp(_merge_fn, m_sb, m_g, p_sb_, p_g_)
        return vjp(dy_)

    return _ew(fn, (T // tm, D // tc), [proj, proj, p_sb, p_g, dy], [sec(8), sec(9), blk, blk, blk],
               [((T, D), BF16)] * 4, [blk] * 4, name="merge_bwd")


def _loss_head(x, u, gate, fnw, tgt):
    T, D = x.shape
    tm, row, par = _row_specs(T, D)

    def loss(x_, u_, gate_, fnw_, tgt_):
        y = _rms(x_ + gate_ * u_, fnw_)
        return 0.5 * jnp.sum(jnp.mean(jnp.square(y - tgt_), axis=-1))

    def fn(x_, u_, gate_, fnw_, tgt_):
        val, (dx, du, dgate, dfnw) = jax.value_and_grad(loss, argnums=(0, 1, 2, 3))(x_, u_, gate_, fnw_, tgt_)
        return jnp.full((1, HD), val, F32), dx, du, dgate, dfnw

    return _ew(fn, (T // tm,), [x, u, gate, fnw, tgt], [row, row, par, par, row],
               [((1, HD), F32), ((T, D), F32), ((T, D), BF16), ((1, D), F32), ((1, D), F32)],
               [pl.BlockSpec((1, HD), lambda i: (0, 0)), row, row, par, par],
               acc=['all', None, None, 'all', 'all'], name="loss_head")


def _mod_part(c_all, w_ada, b_ada):
    D, N = w_ada.shape
    tn = min(512, N)

    def fn(c, w, b):
        return _dot(_silu(c).astype(BF16), w.astype(BF16)) + b

    return _ew(fn, (N // tn,), [c_all, w_ada, b_ada],
               [pl.BlockSpec((8, D), lambda j: (0, 0)), pl.BlockSpec((D, tn), lambda j: (0, j)),
                pl.BlockSpec((1, tn), lambda j: (0, j))],
               [((8, N), F32)], [pl.BlockSpec((8, tn), lambda j: (0, j))], name="ada_mod")[0]


def _w_ada_grad(c_all, dmod):
    D, N = c_all.shape[1], dmod.shape[1]
    tn = min(512, N)

    def fn(c, dm):
        return _dot(_silu(c).astype(BF16), dm.astype(BF16), TN)

    return _ew(fn, (N // tn,), [c_all, dmod],
               [pl.BlockSpec((8, D), lambda j: (0, 0)), pl.BlockSpec((8, tn), lambda j: (0, j))],
               [((D, N), F32)], [pl.BlockSpec((D, tn), lambda j: (0, j))], name="w_ada_grad")[0]


def _sum8(packs):
    N = packs.shape[1]

    def fn(p):
        s = p[0:8]
        for d in range(1, 8):
            s = s + p[8 * d:8 * d + 8]
        return s

    return _ew(fn, (1,), [packs], [pl.BlockSpec((64, N), lambda i: (0, 0))], [((8, N), F32)],
               [pl.BlockSpec((8, N), lambda i: (0, 0))], name="sum_devices")[0]


def _sum_chips(q, c, name):
    _, R, C = q.shape
    tr = min(64, R)

    def body(c_ref, q_ref, o_ref):
        p = q_ref[...].astype(F32)
        o_ref[...] = (p[0] + p[1]) + (p[2] + p[3])

    grid_spec = pltpu.PrefetchScalarGridSpec(
        num_scalar_prefetch=1, grid=(R // tr,),
        in_specs=[pl.BlockSpec((4, tr, C), lambda i, c_ref: (0, i, 0))],
        out_specs=pl.BlockSpec((None, tr, C), lambda i, c_ref: (c_ref[0], i, 0)))
    return _pcall(body, grid_spec=grid_spec, out_shape=jax.ShapeDtypeStruct((2, R, C), F32),
                  compiler_params=_cparams(("arbitrary",)), name=name)(jnp.reshape(c, (1,)).astype(jnp.int32), q)


def _add_halves(g, rsib, c, name):
    _, R, C = g.shape
    rh = R // 2
    tr = min(64, rh)
    nb = rh // tr

    def body(c_ref, a_ref, b_ref, o_ref, o2_ref):
        v = (a_ref[...] + b_ref[...]).astype(o_ref.dtype)
        o_ref[...] = v
        o2_ref[...] = v

    spec = pl.BlockSpec((None, tr, C), lambda j, i, c_ref: (j, i, 0))
    grid_spec = pltpu.PrefetchScalarGridSpec(
        num_scalar_prefetch=1, grid=(4, nb),
        in_specs=[pl.BlockSpec((None, tr, C), lambda j, i, c_ref: (j, c_ref[0] * nb + i, 0)), spec],
        out_specs=[spec, spec])
    return _pcall(body, grid_spec=grid_spec, out_shape=[jax.ShapeDtypeStruct((4, rh, C), BF16)] * 2,
                  compiler_params=_cparams(("arbitrary", "arbitrary")), name=name)(
                      jnp.reshape(c, (1,)).astype(jnp.int32), g, rsib)


def _adamw(w, g, m, v, name):
    R, C = w.shape
    tr = R if R <= 64 else 64
    blk = pl.BlockSpec((tr, C), lambda i: (i, 0))

    def fn(w_, g_, m_, v_):
        m2 = ADAM_B1 * m_ + (1.0 - ADAM_B1) * g_
        v2 = ADAM_B2 * v_ + (1.0 - ADAM_B2) * jnp.square(g_)
        m_hat = m2 / (1.0 - ADAM_B1 ** ADAM_STEP)
        v_hat = v2 / (1.0 - ADAM_B2 ** ADAM_STEP)
        delta = -ADAM_LR * (m_hat / (jnp.sqrt(v_hat) + ADAM_EPS) + ADAM_WD * w_)
        return delta, m2, v2

    return _ew(fn, (R // tr,), [w, g, m, v], [blk] * 4, [((R, C), F32)] * 3, [blk] * 3, name=name)


def _place():
    x, y, c = lax.axis_index("x"), lax.axis_index("y"), lax.axis_index("c")
    return x, y, c, [(1 - x, y), (x, 1 - y), (1 - x, 1 - y)]


ANY = pl.BlockSpec(memory_space=pl.ANY)


def _allgather8(blk):
    m_per, n = blk.shape

    def body(x_ref, out_ref, send_sems, recv_sems, local_sem):
        x, y, c, chips = _place()
        me, sibling = (x, y, c), (x, y, 1 - c)

        def rows(px, py, pc):
            return out_ref.at[pl.ds((4 * px + 2 * py + pc) * m_per, m_per), :]

        def copy(k, block, to, src=None):
            return pltpu.make_async_remote_copy(
                src_ref=rows(*block) if src is None else src, dst_ref=rows(*block),
                send_sem=send_sems.at[k], recv_sem=recv_sems.at[k], device_id=to, device_id_type=MESH)

        mine = pltpu.make_async_copy(x_ref, rows(*me), local_sem)
        mine.start()
        first = [copy(0, me, sibling, src=x_ref)]
        first += [copy(1 + j, me, (*chip, c), src=x_ref) for j, chip in enumerate(chips)]
        for cp in first:
            cp.start()
        passed = [copy(4 + j, (*chip, c), sibling) for j, chip in enumerate(chips)]
        for j, chip in enumerate(chips):
            copy(1 + j, (*chip, c), me).wait_recv()
            passed[j].start()
        copy(0, sibling, me).wait_recv()
        for j, chip in enumerate(chips):
            copy(4 + j, (*chip, 1 - c), me).wait_recv()
        for cp in first + passed:
            cp.wait_send()
        mine.wait()

    vm = pl.BlockSpec(memory_space=pltpu.VMEM)
    return _pcall(body, out_shape=jax.ShapeDtypeStruct((8 * m_per, n), blk.dtype), in_specs=[vm], out_specs=vm,
                  scratch_shapes=[pltpu.SemaphoreType.DMA((7,)), pltpu.SemaphoreType.DMA((7,)),
                                  pltpu.SemaphoreType.DMA],
                  compiler_params=pltpu.CompilerParams(vmem_limit_bytes=VMEM_LIMIT), name="allgather8")(blk)


def _gather_weights(shards):
    n = len(shards)

    def body(*refs):
        ins, outs = refs[:n], refs[n:2 * n]
        send_sems, recv_sems, fsend_sems, frecv_sems = refs[2 * n:]
        x, y, c, chips = _place()
        mychip = 2 * x + y
        sends, fwds = [], []
        for a in range(n):
            rh = shards[a].shape[0] // 2
            for k, chip in enumerate(chips):
                cp = pltpu.make_async_remote_copy(
                    src_ref=ins[a].at[pl.ds(c * rh, rh), :], dst_ref=outs[a].at[mychip, pl.ds(c * rh, rh), :],
                    send_sem=send_sems.at[3 * a + k], recv_sem=recv_sems.at[3 * a + k],
                    device_id=(*chip, c), device_id_type=MESH)
                cp.start()
                sends.append(cp)
        for a in range(n):
            rh = shards[a].shape[0] // 2
            for k, (px, py) in enumerate(chips):
                land = outs[a].at[2 * px + py, pl.ds(c * rh, rh), :]
                pltpu.make_async_remote_copy(
                    src_ref=land, dst_ref=land, send_sem=send_sems.at[3 * a + k], recv_sem=recv_sems.at[3 * a + k],
                    device_id=(px, py, c), device_id_type=MESH).wait_recv()
                fw = pltpu.make_async_remote_copy(
                    src_ref=land, dst_ref=land, send_sem=fsend_sems.at[3 * a + k], recv_sem=frecv_sems.at[3 * a + k],
                    device_id=(x, y, 1 - c), device_id_type=MESH)
                fw.start()
                fwds.append(fw)
        for a in range(n):
            rh = shards[a].shape[0] // 2
            for k, (px, py) in enumerate(chips):
                land = outs[a].at[2 * px + py, pl.ds((1 - c) * rh, rh), :]
                pltpu.make_async_remote_copy(
                    src_ref=land, dst_ref=land, send_sem=fsend_sems.at[3 * a + k], recv_sem=frecv_sems.at[3 * a + k],
                    device_id=(x, y, 1 - c), device_id_type=MESH).wait_recv()
        for cp in sends + fwds:
            cp.wait_send()

    return _pcall(body, out_shape=[jax.ShapeDtypeStruct((4,) + s.shape, s.dtype) for s in shards],
                  in_specs=[ANY] * n, out_specs=[ANY] * n,
                  scratch_shapes=[pltpu.SemaphoreType.DMA((3 * n,))] * 4, name="gather_weights")(*shards)


def _row_chunks(rows, n):
    while n > 1 and rows % (8 * n):
        n //= 2
    return [(k * (rows // n), rows // n) for k in range(n)]


def _swap_partial_halves(gs):
    n = len(gs)
    plan = [(a, j, r0, nr) for a in range(n) for j in range(4)
            for r0, nr in _row_chunks(gs[a].shape[1] // 2, SWAP_CHUNKS)]

    def body(*refs):
        ins, outs = refs[:n], refs[n:2 * n]
        send_sems, recv_sems = refs[2 * n:]
        x, y, c, _ = _place()
        cps = []
        for k, (a, j, r0, nr) in enumerate(plan):
            rh = gs[a].shape[1] // 2
            cp = pltpu.make_async_remote_copy(
                src_ref=ins[a].at[j, pl.ds((1 - c) * rh + r0, nr), :], dst_ref=outs[a].at[j, pl.ds(r0, nr), :],
                send_sem=send_sems.at[k], recv_sem=recv_sems.at[k], device_id=(x, y, 1 - c), device_id_type=MESH)
            cp.start()
            cps.append(cp)
        for cp in cps:
            cp.wait()

    half = [jax.ShapeDtypeStruct((4, g.shape[1] // 2, g.shape[2]), g.dtype) for g in gs]
    return _pcall(body, out_shape=half, in_specs=[ANY] * n, out_specs=[ANY] * n,
                  scratch_shapes=[pltpu.SemaphoreType.DMA((len(plan),))] * 2, name="swap_partial_halves")(*gs)


def _chip_scatter(ps, lands):
    n = len(ps)

    def body(*refs):
        ins, outs = refs[:n], refs[2 * n:3 * n]
        send_sems, recv_sems = refs[3 * n:]
        x, y, c, chips = _place()
        mychip = 2 * x + y
        cps = []
        for a in range(n):
            for k, (px, py) in enumerate(chips):
                cp = pltpu.make_async_remote_copy(
                    src_ref=ins[a].at[2 * px + py], dst_ref=outs[a].at[mychip],
                    send_sem=send_sems.at[3 * a + k], recv_sem=recv_sems.at[3 * a + k],
                    device_id=(px, py, c), device_id_type=MESH)
                cp.start()
                cps.append(cp)
        for a in range(n):
            for k, (px, py) in enumerate(chips):
                land = outs[a].at[2 * px + py]
                pltpu.make_async_remote_copy(
                    src_ref=land, dst_ref=land, send_sem=send_sems.at[3 * a + k], recv_sem=recv_sems.at[3 * a + k],
                    device_id=(px, py, c), device_id_type=MESH).wait_recv()
        for cp in cps:
            cp.wait_send()

    return _pcall(body, out_shape=[jax.ShapeDtypeStruct(p.shape, p.dtype) for p in ps],
                  in_specs=[ANY] * (2 * n), out_specs=[ANY] * n,
                  input_output_aliases={n + a: a for a in range(n)},
                  scratch_shapes=[pltpu.SemaphoreType.DMA((3 * n,))] * 2, name="chip_scatter")(*ps, *lands)


def _swap_final_halves(hs):
    n = len(hs)
    plan = [(a, r0, nr) for a in range(n) for r0, nr in _row_chunks(hs[a].shape[1], 2 * SWAP_CHUNKS)]

    def body(*refs):
        outs = refs[n:2 * n]
        send_sems, recv_sems = refs[2 * n:]
        x, y, c, _ = _place()
        cps = []
        for k, (a, r0, nr) in enumerate(plan):
            mine = outs[a].at[c, pl.ds(r0, nr), :]
            cp = pltpu.make_async_remote_copy(
                src_ref=mine, dst_ref=mine, send_sem=send_sems.at[k], recv_sem=recv_sems.at[k],
                device_id=(x, y, 1 - c), device_id_type=MESH)
            cp.start()
            cps.append(cp)
        for k, (a, r0, nr) in enumerate(plan):
            land = outs[a].at[1 - c, pl.ds(r0, nr), :]
            pltpu.make_async_remote_copy(
                src_ref=land, dst_ref=land, send_sem=send_sems.at[k], recv_sem=recv_sems.at[k],
                device_id=(x, y, 1 - c), device_id_type=MESH).wait_recv()
        for cp in cps:
            cp.wait_send()

    return _pcall(body, out_shape=[jax.ShapeDtypeStruct(h.shape, h.dtype) for h in hs],
                  in_specs=[ANY] * n, out_specs=[ANY] * n, input_output_aliases={a: a for a in range(n)},
                  scratch_shapes=[pltpu.SemaphoreType.DMA((len(plan),))] * 2, name="swap_final_halves")(*hs)


def _pad_cols(a, n):
    return jnp.pad(a, ((0, 0), (0, n - a.shape[1])))


def kernel(x, c, w_ada, b_ada, norm_w, w_in, gdn_conv_w, gdn_a_log, gdn_dt_bias, gdn_norm_w, w_proj_sb, w_proj_gdn, w_out, final_norm_w, loss_target, m_w_ada, m_b_ada, m_norm_w, m_w_in, m_gdn_conv_w, m_gdn_a_log, m_gdn_dt_bias, m_gdn_norm_w, m_w_proj_sb, m_w_proj_gdn, m_w_out, m_final_norm_w, v_w_ada, v_b_ada, v_norm_w, v_w_in, v_gdn_conv_w, v_gdn_a_log, v_gdn_dt_bias, v_gdn_norm_w, v_w_proj_sb, v_w_proj_gdn, v_w_out, v_final_norm_w):
    T, D = x.shape[1], x.shape[2]
    H = gdn_a_log.shape[1]
    W = H * HD
    assert W == D and T % CH == 0
    NA = w_ada.shape[2]
    NI = w_in.shape[2]
    CW = gdn_conv_w.shape[2]
    px, py, pc = lax.axis_index("x"), lax.axis_index("y"), lax.axis_index("c")
    chip = 2 * px + py
    me = 2 * chip + pc
    x2d, tgt = x[0], loss_target[0]
    PW = 3 * D

    pack1 = jnp.concatenate([c, _pad_cols(gdn_conv_w[0], D), jnp.zeros((3, D), F32)], axis=0)
    got1 = _allgather8(pack1).reshape(8, 8, D)
    c_all = got1[:, 0, :]
    conv_w = jnp.concatenate([got1[2 * j, 1:1 + CONV_K, :CW] for j in range(4)], axis=1)

    b_shard = lax.dynamic_slice_in_dim(b_ada, chip * NA, NA, axis=1)
    mod_part = _mod_part(c_all, w_ada[0], b_shard)
    got2 = _allgather8(mod_part).reshape(8, 8, NA)
    mod = jnp.concatenate([lax.dynamic_index_in_dim(got2[2 * j], me, 0) for j in range(4)], axis=1)
    shift, scale, gate = mod[:, :D], mod[:, D:2 * D], mod[:, 2 * D:]

    own = [w_in[0].astype(BF16), w_proj_sb[0].astype(BF16), w_proj_gdn[0].astype(BF16), w_out[0].astype(BF16)]
    is_own = (jnp.arange(4) == chip)[:, None, None]
    wg = [jnp.where(is_own, o[None], g) for o, g in zip(own, _gather_weights(own))]
    w_in_full = jnp.concatenate([wg[0][j] for j in range(4)], axis=1)
    w_big = jnp.concatenate([w_in_full[:, :8 * W], w_in_full[:, 8 * W + 2 * H:]], axis=1)
    w_small = _pad_cols(w_in_full[:, 8 * W:8 * W + 2 * H], HD)
    w_psb, w_pg, w_o = (wg[i].reshape(D, D) for i in (1, 2, 3))

    h = _ada_fwd(x2d, norm_w, scale, shift)
    proj = _mm1(h, w_big, 'nn', F32, "proj_big")
    small = _mm1(h, w_small, 'nn', F32, "proj_small")
    o_sb, l_sb = _sb_fwd(proj, T, H)
    conv = _conv_fwd(proj, conv_w, T, H)
    al = jnp.pad(gdn_a_log, ((0, 0), (H, HD - 2 * H)))
    dtb = jnp.pad(gdn_dt_bias, ((0, 0), (H, HD - 2 * H)))
    qn, kn, vc, beta_b, g_b = _prep_fwd(conv, small, al, dtb, T, H)
    o_g, s_all = _gdn_fwd(qn, kn, vc, beta_b, g_b, T, H)
    a_sb, a_g = _gate_fwd(o_sb, proj, o_g, gdn_norm_w, T, H)
    p_sb = _mm1(a_sb, w_psb, 'nn', F32, "proj_sb")
    p_g = _mm1(a_g, w_pg, 'nn', F32, "proj_gdn")
    y = _merge_fwd(proj, p_sb, p_g, T, D, H)
    u = _mm1(y, w_o, 'nn', F32, "proj_out")
    loss_p, dx2, du, dgate, g_fnw = _loss_head(x2d, u, gate, final_norm_w.reshape(1, D), tgt)

    dy = _mm1(du, w_o, 'nt', F32, "d_merge")
    g_w_out = _mm1(y, du, 'tn', F32, "g_w_out")
    dm_sb, dm_g, dp_sb, dp_g = _merge_bwd(proj, p_sb, p_g, dy, T, D, H)
    da_sb = _mm1(dp_sb, w_psb, 'nt', F32, "d_a_sb")
    g_w_psb = _mm1(a_sb, dp_sb, 'tn', F32, "g_w_proj_sb")
    da_g = _mm1(dp_g, w_pg, 'nt', F32, "d_a_gdn")
    g_w_pg = _mm1(a_g, dp_g, 'tn', F32, "g_w_proj_gdn")
    do_sb, dz_sb, do_g, dz_g, g_gnw = _gate_bwd(o_sb, proj, o_g, gdn_norm_w, da_sb, da_g, T, H)
    dq_sb, dk_sb, dv_sb = _sb_bwd(proj, l_sb, do_sb, T, H)
    dqn, dkn, dvc, dbeta_b, dg_b = _gdn_bwd(qn, kn, vc, beta_b, g_b, s_all, do_g, T, H)
    dconv, dsmall, dal, ddtb = _prep_bwd(conv, small, al, dtb, dqn, dkn, dvc, dbeta_b, dg_b, T, H)
    dpre, g_conv = _conv_bwd(proj, conv_w, dconv, T, H)

    secs = [(dq_sb, 0), (dk_sb, W), (dv_sb, 2 * W), (dz_sb, 3 * W), (dpre, 4 * W), (dz_g, 7 * W),
            (dm_sb, 8 * W), (dm_g, 9 * W)]
    dh = _mm([(a, w_big, 0, off, a.shape[1]) for a, off in secs] + [(dsmall, w_small, 0, 0, HD)],
             'nt', F32, "d_h", tk=min(MM_TK_MANY, W))
    g_secs = [_mm1(h, a, 'tn', F32, "g_w_in_%d" % i) for i, (a, _) in enumerate(secs)]
    g_small = _mm1(h, dsmall, 'tn', F32, "g_w_in_small")
    grad_x, g_nw, dscale, dshift = _ada_bwd(x2d, norm_w, scale, shift, dh, dx2)

    dmod = jnp.concatenate([dshift, dscale, dgate], axis=1)
    misc = jnp.concatenate([g_nw, g_fnw, g_gnw, dal[:, H:2 * H], ddtb[:, H:2 * H], loss_p[:, :1]], axis=1)
    pack3 = jnp.concatenate([dmod, g_conv, _pad_cols(misc, PW), jnp.zeros((2, PW), F32)], axis=0)
    got3 = _allgather8(pack3)
    tot = _sum8(got3)
    dmod_all = got3.reshape(8, 8, PW)[:, 0, :]
    g_w_ada = _w_ada_grad(c_all, lax.dynamic_slice_in_dim(dmod_all, chip * NA, NA, axis=1))
    g_conv_sh = lax.dynamic_slice_in_dim(tot[1:1 + CONV_K], chip * CW, CW, axis=1)
    loss = tot[5, 2 * D + HD + 2 * H]

    g_in = jnp.concatenate(g_secs[:6] + [g_small[:, :2 * H]] + g_secs[6:], axis=1)
    g_full = [jnp.stack([g_in[:, j * NI:(j + 1) * NI] for j in range(4)]),
              g_w_psb.reshape(4, D // 4, D), g_w_pg.reshape(4, D // 4, D), g_w_out.reshape(4, D // 4, D)]
    g_sib = _swap_partial_halves(g_full)
    parts = [_add_halves(g, r, pc, "add_halves_%d" % i) for i, (g, r) in enumerate(zip(g_full, g_sib))]
    got = _chip_scatter([p[0] for p in parts], [p[1] for p in parts])
    halves = [_sum_chips(q, pc, "sum_chips_%d" % i) for i, q in enumerate(got)]
    g_red = [f.reshape(f.shape[1] * 2, f.shape[2]) for f in _swap_final_halves(halves)]

    out = {}

    def upd(name, w, g, m, v, shape):
        d_, m_, v_ = _adamw(w, g, m, v, "adamw_" + name)
        out[name] = (g.reshape(shape), d_.reshape(shape), m_.reshape(shape), v_.reshape(shape))

    upd("w_ada", w_ada[0], g_w_ada, m_w_ada[0], v_w_ada[0], w_ada.shape)
    upd("w_in", w_in[0], g_red[0], m_w_in[0], v_w_in[0], w_in.shape)
    upd("gdn_conv_w", gdn_conv_w[0], g_conv_sh, m_gdn_conv_w[0], v_gdn_conv_w[0], gdn_conv_w.shape)
    upd("w_proj_sb", w_proj_sb[0], g_red[1], m_w_proj_sb[0], v_w_proj_sb[0], w_proj_sb.shape)
    upd("w_proj_gdn", w_proj_gdn[0], g_red[2], m_w_proj_gdn[0], v_w_proj_gdn[0], w_proj_gdn.shape)
    upd("w_out", w_out[0], g_red[3], m_w_out[0], v_w_out[0], w_out.shape)

    def packs(b, nw, fnw, gnw, a, dt):
        row = jnp.concatenate([nw, fnw.reshape(1, D), gnw, a, dt], axis=1)
        return jnp.concatenate([b, _pad_cols(row, PW), jnp.zeros((6, PW), F32)], axis=0)

    g_pack = jnp.concatenate([tot[0:1], tot[5:6], jnp.zeros((6, PW), F32)], axis=0)
    d_, m_, v_ = _adamw(packs(b_ada, norm_w, final_norm_w, gdn_norm_w, gdn_a_log, gdn_dt_bias), g_pack,
                        packs(m_b_ada, m_norm_w, m_final_norm_w, m_gdn_norm_w, m_gdn_a_log, m_gdn_dt_bias),
                        packs(v_b_ada, v_norm_w, v_final_norm_w, v_gdn_norm_w, v_gdn_a_log, v_gdn_dt_bias),
                        "adamw_small")
    offs = {"norm_w": (0, D, (1, D)), "final_norm_w": (D, D, (D,)), "gdn_norm_w": (2 * D, HD, (1, HD)),
            "gdn_a_log": (2 * D + HD, H, (1, H)), "gdn_dt_bias": (2 * D + HD + H, H, (1, H))}
    out["b_ada"] = tuple(a[0:1] for a in (g_pack, d_, m_, v_))
    for name, (o, n_, shp) in offs.items():
        out[name] = tuple(a[1, o:o + n_].reshape(shp) for a in (g_pack, d_, m_, v_))

    names = ['w_ada', 'b_ada', 'norm_w', 'w_in', 'gdn_conv_w', 'gdn_a_log', 'gdn_dt_bias', 'gdn_norm_w',
             'w_proj_sb', 'w_proj_gdn', 'w_out', 'final_norm_w']
    return (loss, grad_x.reshape(x.shape), *[out[n][0] for n in names], *[out[n][1] for n in names],
            *[out[n][2] for n in names], *[out[n][3] for n in names])
```
